```python
import math
import jax, jax.numpy as jnp
from jax import lax
import numpy as np

D_MODEL = 4096
BATCH = 1
SEQ = 16384
DEPTH = 2

HEAD_DIM = 128
ROPE_THETA = 10000.0
QBLOCK = 128

DSW_HEADS = D_MODEL // (4 * HEAD_DIM)
DSW_PATTERNS = ((128, 1), (512, 4), (2048, 16))
NSA_HEADS = D_MODEL // (4 * HEAD_DIM)
NSA_KV_HEADS = NSA_HEADS // 4
NSA_CMP_LEN = 32
NSA_CMP_STRIDE = 16
NSA_CMP_HIDDEN = 256
NSA_SEL_BLOCK = 64
NSA_TOP_N = 16
NSA_WINDOW = 512
MLA_HEADS = D_MODEL // (2 * HEAD_DIM)
MLA_Q_LORA = 1536
MLA_KV_LORA = 512
MLA_NOPE_DIM = 128
MLA_ROPE_DIM = 64
MLA_V_DIM = 128
D_FF = 4 * D_MODEL
ALPHA = (2 * DEPTH) ** 0.25
BETA = (8 * DEPTH) ** -0.25

DSW_W = DSW_HEADS * HEAD_DIM
NSA_QW = NSA_HEADS * HEAD_DIM
NSA_KVW = NSA_KV_HEADS * HEAD_DIM
IN_SIZES = (DSW_W, DSW_W, DSW_W,
            NSA_QW, NSA_KVW, NSA_KVW, NSA_KVW, NSA_KVW, NSA_KVW, NSA_KVW,
            3 * NSA_HEADS,
            MLA_Q_LORA, MLA_KV_LORA, MLA_ROPE_DIM)
D_IN = sum(IN_SIZES)

kernel_name = "hybrid_dilated_nsa_mla_deepnorm_adaln"


def layer_norm(x, g, b, eps=1e-5):
    xf = x.astype(jnp.float32)
    mu = jnp.mean(xf, -1, keepdims=True)
    var = jnp.mean(jnp.square(xf - mu), -1, keepdims=True)
    return ((xf - mu) * lax.rsqrt(var + eps) * g + b).astype(x.dtype)


def rms_norm(x, g, eps=1e-6):
    xf = x.astype(jnp.float32)
    return (xf * lax.rsqrt(jnp.mean(xf * xf, -1, keepdims=True) + eps) * g).astype(x.dtype)


def rope_tables(positions, dim):
    inv = ROPE_THETA ** (-jnp.arange(0, dim, 2, dtype=jnp.float32) / dim)
    ang = positions.astype(jnp.float32)[..., None] * inv
    return jnp.cos(ang), jnp.sin(ang)


def apply_rope(x, cos, sin):
    xf = x.astype(jnp.float32)
    x1, x2 = jnp.split(xf, 2, axis=-1)
    c, s = cos[:, :, None, :], sin[:, :, None, :]
    return jnp.concatenate([x1 * c - x2 * s, x1 * s + x2 * c], axis=-1).astype(x.dtype)


def masked_softmax(s, mask):
    s = jnp.where(mask, s, -jnp.inf)
    m = jnp.max(s, -1, keepdims=True)
    e = jnp.exp(s - jnp.where(jnp.isfinite(m), m, 0.0))
    l = jnp.sum(e, -1, keepdims=True)
    return e / jnp.where(l > 0, l, 1.0)


def banded_window_stats(q, k, v, max_delta, scale):
    B, L, H, D = q.shape
    hkv = k.shape[2]
    g = H // hkv
    nb = -(-L // QBLOCK)
    Lp = nb * QBLOCK
    nprev = -(-max_delta // QBLOCK)
    pad = ((0, 0), (0, Lp - L), (0, 0), (0, 0))
    qb = jnp.pad(q, pad).reshape(B, nb, QBLOCK, hkv, g, D)

    def band(t):
        tb = jnp.pad(t, pad).reshape(B, nb, QBLOCK, hkv, D)
        tb = jnp.pad(tb, ((0, 0), (nprev, 0), (0, 0), (0, 0), (0, 0)))
        return jnp.concatenate([tb[:, i:i + nb] for i in range(nprev + 1)], axis=2)

    kb, vb = band(k), band(v)
    s = jnp.einsum('bnqhgd,bnkhd->bnhgqk', qb, kb, preferred_element_type=jnp.float32) * scale
    qi = jnp.arange(QBLOCK)[:, None]
    ki = jnp.arange((nprev + 1) * QBLOCK)[None, :]
    delta = nprev * QBLOCK + qi - ki
    kpos = jnp.arange(nb)[:, None, None] * QBLOCK + ki[None] - nprev * QBLOCK
    mask = (delta >= 0) & (delta <= max_delta) & (kpos >= 0)
    s = jnp.where(mask[None, :, None, None], s, -jnp.inf)
    m = jnp.max(s, -1, keepdims=True)
    p = jnp.exp(s - m)
    l = jnp.sum(p, -1)
    o = jnp.einsum('bnhgqk,bnkhd->bnqhgd', p.astype(v.dtype), vb).reshape(B, Lp, H, D)[:, :L]

    def heads_last(t):
        return t.transpose(0, 1, 4, 2, 3).reshape(B, Lp, H)[:, :L]

    return heads_last(m[..., 0]), heads_last(l), o


def dilated_mixture_attention(q, k, v):
    B, S, H, D = q.shape
    scale = D ** -0.5
    ms, ls, os_ = [], [], []
    for window, dil in DSW_PATTERNS:
        L = S // dil

        def to_strided(t):
            return t.reshape(B, L, dil, H, D).transpose(0, 2, 1, 3, 4).reshape(B * dil, L, H, D)

        def from_strided(t):
            rest = t.shape[2:]
            return t.reshape(B, dil, L, *rest).swapaxes(1, 2).reshape(B, S, *rest)

        m, l, o = banded_window_stats(to_strided(q), to_strided(k), to_strided(v), window // dil, scale)
        ms.append(from_strided(m))
        ls.append(from_strided(l))
        os_.append(from_strided(o))
    m_all = jnp.stack(ms)
    w = jnp.exp(m_all - jnp.max(m_all, 0, keepdims=True))
    den = jnp.sum(w * jnp.stack(ls), 0)
    num = sum(w[i][..., None] * os_[i] for i in range(len(DSW_PATTERNS)))
    return (num / den[..., None]).astype(q.dtype)


def compress_tokens(t, pe, w1, w2):
    B, S, hkv, D = t.shape
    n_c = (S - NSA_CMP_LEN) // NSA_CMP_STRIDE + 1
    idx = jnp.arange(n_c)[:, None] * NSA_CMP_STRIDE + jnp.arange(NSA_CMP_LEN)[None, :]
    blk = t[:, idx] + pe[None, None, :, None, :]
    flat = blk.transpose(0, 1, 3, 2, 4).reshape(B, n_c, hkv, NSA_CMP_LEN * D)
    return jax.nn.gelu(flat @ w1) @ w2


def native_sparse_attention(q, kc, vc, ks, vs, kw, vw, gate_logits, cmp_pe, cmp_w1, cmp_w2):
    B, S, H, D = q.shape
    hkv = NSA_KV_HEADS
    g = H // hkv
    scale = D ** -0.5
    n_c = (S - NSA_CMP_LEN) // NSA_CMP_STRIDE + 1
    n_sel = S // NSA_SEL_BLOCK
    n_top = min(NSA_TOP_N, n_sel)
    nb = S // QBLOCK

    k_cmp = compress_tokens(kc, cmp_pe[0], cmp_w1[0], cmp_w2[0])
    v_cmp = compress_tokens(vc, cmp_pe[1], cmp_w1[1], cmp_w2[1])
    cmp_start = jnp.arange(n_c) * NSA_CMP_STRIDE
    cmp_end = cmp_start + NSA_CMP_LEN - 1
    sel_start = jnp.arange(n_sel) * NSA_SEL_BLOCK
    cover = ((cmp_start[:, None] < sel_start[None, :] + NSA_SEL_BLOCK)
             & (cmp_start[:, None] + NSA_CMP_LEN > sel_start[None, :])).astype(jnp.float32)
    ks_t = ks.transpose(0, 2, 1, 3)
    vs_t = vs.transpose(0, 2, 1, 3)
    b_ix = jnp.arange(B)[:, None, None]
    h_ix = jnp.arange(hkv)[None, :, None]
    blk_tok = jnp.arange(NSA_SEL_BLOCK)
    j_sel = jnp.arange(n_sel)[None, :]

    def block(args):
        n, qb = args
        qpos = n * QBLOCK + jnp.arange(QBLOCK)
        qg = qb.reshape(B, QBLOCK, hkv, g, D)
        s_c = jnp.einsum('bqhgd,bchd->bhgqc', qg, k_cmp, preferred_element_type=jnp.float32) * scale
        p_c = masked_softmax(s_c, cmp_end[None, :] <= qpos[:, None])
        o_c = jnp.einsum('bhgqc,bchd->bqhgd', p_c.astype(v_cmp.dtype), v_cmp)
        imp = jnp.sum(p_c, axis=2) @ cover
        cur = qpos[:, None] // NSA_SEL_BLOCK
        forced = (j_sel == 0) | (j_sel == cur) | (j_sel == cur - 1)
        valid = sel_start[None, :] <= qpos[:, None]
        score = jnp.where(valid, jnp.where(forced, jnp.inf, imp), -jnp.inf)
        _, top = lax.top_k(score, n_top)
        tok = (top[..., None] * NSA_SEL_BLOCK + blk_tok).reshape(B, hkv, QBLOCK, n_top * NSA_SEL_BLOCK)
        flat = tok.reshape(B, hkv, -1)
        k_g = ks_t[b_ix, h_ix, flat].reshape(B, hkv, QBLOCK, -1, D)
        v_g = vs_t[b_ix, h_ix, flat].reshape(B, hkv, QBLOCK, -1, D)
        s_s = jnp.einsum('bqhgd,bhqtd->bhgqt', qg, k_g, preferred_element_type=jnp.float32) * scale
        p_s = masked_softmax(s_s, (tok <= qpos[:, None])[:, :, None])
        o_s = jnp.einsum('bhgqt,bhqtd->bqhgd', p_s.astype(v_g.dtype), v_g)
        return o_c.reshape(B, QBLOCK, H, D), o_s.reshape(B, QBLOCK, H, D)

    qblocks = q.reshape(B, nb, QBLOCK, H, D).swapaxes(0, 1)
    o_cmp, o_sel = lax.map(block, (jnp.arange(nb), qblocks))
    o_cmp = o_cmp.swapaxes(0, 1).reshape(B, S, H, D)
    o_sel = o_sel.swapaxes(0, 1).reshape(B, S, H, D)
    _, l_w, o_w = banded_window_stats(q, kw, vw, NSA_WINDOW - 1, scale)
    o_win = o_w / l_w[..., None]
    gates = jax.nn.sigmoid(gate_logits.astype(jnp.float32))
    out = gates[..., 0:1] * o_cmp + gates[..., 1:2] * o_sel + gates[..., 2:3] * o_win
    return out.astype(q.dtype)


def causal_dense_attention(q, k, v, scale):
    B, S, H, Dk = q.shape
    Dv = v.shape[-1]
    nb = S // QBLOCK
    kpos = jnp.arange(S)

    def block(args):
        n, qb = args
        qpos = n * QBLOCK + jnp.arange(QBLOCK)
        s = jnp.einsum('bqhd,bkhd->bhqk', qb, k, preferred_element_type=jnp.float32) * scale
        s = jnp.where(kpos[None, :] <= qpos[:, None], s, -jnp.inf)
        p = jax.nn.softmax(s, axis=-1)
        return jnp.einsum('bhqk,bkhd->bqhd', p.astype(v.dtype), v)

    qblocks = q.reshape(B, nb, QBLOCK, H, Dk).swapaxes(0, 1)
    o = lax.map(block, (jnp.arange(nb), qblocks))
    return o.swapaxes(0, 1).reshape(B, S, H, Dv)


def latent_attention(c_q, c_kv, k_rope_raw, q_norm, kv_norm, w_uq, w_ukv, cos64, sin64):
    B, S, _ = c_q.shape
    q = (rms_norm(c_q, q_norm) @ w_uq).reshape(B, S, MLA_HEADS, MLA_NOPE_DIM + MLA_ROPE_DIM)
    q = jnp.concatenate([q[..., :MLA_NOPE_DIM], apply_rope(q[..., MLA_NOPE_DIM:], cos64, sin64)], -1)
    kv = (rms_norm(c_kv, kv_norm) @ w_ukv).reshape(B, S, MLA_HEADS, MLA_NOPE_DIM + MLA_V_DIM)
    k_nope, v = kv[..., :MLA_NOPE_DIM], kv[..., MLA_NOPE_DIM:]
    k_rope = apply_rope(k_rope_raw[:, :, None, :], cos64, sin64)
    k = jnp.concatenate([k_nope, jnp.broadcast_to(k_rope, (B, S, MLA_HEADS, MLA_ROPE_DIM))], -1)
    o = causal_dense_attention(q, k, v, (MLA_NOPE_DIM + MLA_ROPE_DIM) ** -0.5)
    return o.reshape(B, S, MLA_HEADS * MLA_V_DIM)


def hybrid_mixer(h, rope128, rope64, w_in, cmp_pe, cmp_w1, cmp_w2, q_norm, kv_norm, w_uq, w_ukv, w_out):
    B, S, _ = h.shape
    split_idx = [int(i) for i in np.cumsum(IN_SIZES)[:-1]]
    (a_q, a_k, a_v, n_q, n_kc, n_vc, n_ks, n_vs, n_kw, n_vw, n_gate,
     m_cq, m_ckv, m_kr) = jnp.split(h @ w_in, split_idx, axis=-1)
    cos, sin = rope128

    def heads(t, n):
        return t.reshape(B, S, n, HEAD_DIM)

    def rot(t, n):
        return apply_rope(heads(t, n), cos, sin)

    out_a = dilated_mixture_attention(rot(a_q, DSW_HEADS), rot(a_k, DSW_HEADS), heads(a_v, DSW_HEADS))
    out_b = native_sparse_attention(
        rot(n_q, NSA_HEADS),
        rot(n_kc, NSA_KV_HEADS), heads(n_vc, NSA_KV_HEADS),
        rot(n_ks, NSA_KV_HEADS), heads(n_vs, NSA_KV_HEADS),
        rot(n_kw, NSA_KV_HEADS), heads(n_vw, NSA_KV_HEADS),
        n_gate.reshape(B, S, NSA_HEADS, 3), cmp_pe, cmp_w1, cmp_w2)
    out_c = latent_attention(m_cq, m_ckv, m_kr, q_norm, kv_norm, w_uq, w_ukv, rope64[0], rope64[1])
    mixed = jnp.concatenate([out_a.reshape(B, S, DSW_W), out_b.reshape(B, S, NSA_QW), out_c], axis=-1)
    return mixed @ w_out


def squared_relu_mlp(h, w1, w2):
    return jnp.square(jax.nn.relu(h @ w1)) @ w2


def setup_inputs(seed: int = 0) -> dict:
    key = jax.random.key(seed)
    ks = jax.random.split(key, 20)

    def nrm(k, shape, s):
        return jax.random.normal(k, shape, jnp.float32) * s

    x = nrm(ks[0], (BATCH, SEQ, D_MODEL), 1.0)
    c = nrm(ks[1], (BATCH, D_MODEL), 1.0)
    offset = jax.random.randint(ks[2], (BATCH, 1), 0, 1024, dtype=jnp.int32)
    positions = offset + jnp.arange(SEQ, dtype=jnp.int32)[None, :]
    w_ada = nrm(ks[3], (DEPTH, D_MODEL, 6 * D_MODEL), 0.1 * D_MODEL ** -0.5)
    b_ada = nrm(ks[4], (DEPTH, 6 * D_MODEL), 0.01)
    w_in = nrm(ks[5], (DEPTH, D_MODEL, D_IN), D_MODEL ** -0.5)
    nsa_cmp_pe = nrm(ks[6], (DEPTH, 2, NSA_CMP_LEN, HEAD_DIM), 0.02)
    nsa_cmp_w1 = nrm(ks[7], (DEPTH, 2, NSA_CMP_LEN * HEAD_DIM, NSA_CMP_HIDDEN), (NSA_CMP_LEN * HEAD_DIM) ** -0.5)
    nsa_cmp_w2 = nrm(ks[8], (DEPTH, 2, NSA_CMP_HIDDEN, HEAD_DIM), NSA_CMP_HIDDEN ** -0.5)
    mla_q_norm = 1.0 + nrm(ks[9], (DEPTH, MLA_Q_LORA), 0.01)
    mla_kv_norm = 1.0 + nrm(ks[10], (DEPTH, MLA_KV_LORA), 0.01)
    mla_w_uq = nrm(ks[11], (DEPTH, MLA_Q_LORA, MLA_HEADS * (MLA_NOPE_DIM + MLA_ROPE_DIM)), MLA_Q_LORA ** -0.5)
    mla_w_ukv = nrm(ks[12], (DEPTH, MLA_KV_LORA, MLA_HEADS * (MLA_NOPE_DIM + MLA_V_DIM)), MLA_KV_LORA ** -0.5)
    w_out = nrm(ks[13], (DEPTH, D_MODEL, D_MODEL), BETA * D_MODEL ** -0.5)
    ln1_g = 1.0 + nrm(ks[14], (DEPTH, D_MODEL), 0.01)
    ln1_b = nrm(ks[15], (DEPTH, D_MODEL), 0.01)
    mlp_w1 = nrm(ks[16], (DEPTH, D_MODEL, D_FF), D_MODEL ** -0.5)
    mlp_w2 = nrm(ks[17], (DEPTH, D_FF, D_MODEL), BETA * D_FF ** -0.5)
    ln2_g = 1.0 + nrm(ks[18], (DEPTH, D_MODEL), 0.01)
    ln2_b = nrm(ks[19], (DEPTH, D_MODEL), 0.01)
    return {"x": x, "c": c, "positions": positions, "w_ada": w_ada, "b_ada": b_ada, "w_in": w_in,
            "nsa_cmp_pe": nsa_cmp_pe, "nsa_cmp_w1": nsa_cmp_w1, "nsa_cmp_w2": nsa_cmp_w2,
            "mla_q_norm": mla_q_norm, "mla_kv_norm": mla_kv_norm, "mla_w_uq": mla_w_uq,
            "mla_w_ukv": mla_w_ukv, "w_out": w_out, "ln1_g": ln1_g, "ln1_b": ln1_b,
            "mlp_w1": mlp_w1, "mlp_w2": mlp_w2, "ln2_g": ln2_g, "ln2_b": ln2_b}


def reference(x, c, positions, w_ada, b_ada, w_in, nsa_cmp_pe, nsa_cmp_w1, nsa_cmp_w2,
              mla_q_norm, mla_kv_norm, mla_w_uq, mla_w_ukv, w_out, ln1_g, ln1_b,
              mlp_w1, mlp_w2, ln2_g, ln2_b):
    rope128 = rope_tables(positions, HEAD_DIM)
    rope64 = rope_tables(positions, MLA_ROPE_DIM)
    for l in range(DEPTH):
        mod = (c @ w_ada[l] + b_ada[l])[:, None, :]
        sh1, sc1, g1, sh2, sc2, g2 = jnp.split(mod, 6, axis=-1)
        h = x * (1.0 + sc1) + sh1
        y = hybrid_mixer(h, rope128, rope64, w_in[l], nsa_cmp_pe[l], nsa_cmp_w1[l], nsa_cmp_w2[l],
                         mla_q_norm[l], mla_kv_norm[l], mla_w_uq[l], mla_w_ukv[l], w_out[l])
        x = layer_norm(ALPHA * x + (1.0 + g1) * y, ln1_g[l], ln1_b[l])
        h = x * (1.0 + sc2) + sh2
        y = squared_relu_mlp(h, mlp_w1[l], mlp_w2[l])
        x = layer_norm(ALPHA * x + (1.0 + g2) * y, ln2_g[l], ln2_b[l])
    return x
```

```python
import functools

import jax
import jax.numpy as jnp
import numpy as np
from jax import lax
from jax.experimental import pallas as pl
from jax.experimental.pallas import tpu as pltpu

D_MODEL = 4096
DEPTH = 2
HEAD_DIM = 128
ROPE_THETA = 10000.0
QBLOCK = 128
DSW_HEADS = 8
DSW_PATTERNS = ((128, 1), (512, 4), (2048, 16))
NSA_HEADS = 8
NSA_KV_HEADS = 2
NSA_GROUP = NSA_HEADS // NSA_KV_HEADS
NSA_CMP_LEN = 32
NSA_CMP_STRIDE = 16
NSA_CMP_HIDDEN = 256
NSA_SEL_BLOCK = 64
NSA_TOP_N = 16
NSA_WINDOW = 512
MLA_HEADS = 16
MLA_Q_LORA = 1536
MLA_KV_LORA = 512
MLA_NOPE_DIM = 128
MLA_ROPE_DIM = 64
MLA_V_DIM = 128
D_FF = 4 * D_MODEL
ALPHA = (2 * DEPTH) ** 0.25

DSW_W = DSW_HEADS * HEAD_DIM
NSA_QW = NSA_HEADS * HEAD_DIM
NSA_KVW = NSA_KV_HEADS * HEAD_DIM

LANES = 128
V7X_VMEM_LIMIT = 56 * 1024 * 1024

F32 = jnp.float32
BF16 = jnp.bfloat16
NEG_INF = float("-inf")
NT_DIMS = (((1,), (1,)), ((), ()))


def _params(sem, vmem=V7X_VMEM_LIMIT):
    return pltpu.CompilerParams(dimension_semantics=sem, vmem_limit_bytes=vmem)


def _adaln_body(cb_ref, w_ref, b_ref, o_ref, *, tn):
    cb = cb_ref[...]
    for s in range(tn // LANES):
        sl = slice(s * LANES, (s + 1) * LANES)
        o_ref[:, sl] = jnp.sum(w_ref[:, sl] * cb, axis=0, keepdims=True) + b_ref[:, sl]


def adaln(c, w, b, tn=512):
    d, n = w.shape
    cb = jnp.broadcast_to(c.reshape(d, 1), (d, LANES))
    return pl.pallas_call(
        functools.partial(_adaln_body, tn=tn),
        grid=(n // tn,),
        in_specs=[
            pl.BlockSpec((d, LANES), lambda j: (0, 0)),
            pl.BlockSpec((d, tn), lambda j: (0, j)),
            pl.BlockSpec((1, tn), lambda j: (0, j)),
        ],
        out_specs=pl.BlockSpec((1, tn), lambda j: (0, j)),
        out_shape=jax.ShapeDtypeStruct((1, n), F32),
        compiler_params=_params(("arbitrary",)),
        name="adaln",
    )(cb, w, b.reshape(1, n))


def _mm_fullk_body(*refs, prologue, epilogue, nsub):
    refs = list(refs)
    mode_ref = refs.pop(0) if epilogue == "rope" else None
    x_ref, p1_ref = refs.pop(0), refs.pop(0)
    p2_ref = refs.pop(0) if prologue == "mod" else None
    w_ref = refs.pop(0)
    if epilogue == "rope":
        ta_ref, tb_ref = refs.pop(0), refs.pop(0)
    o_ref, h_scr = refs
    j = pl.program_id(1)

    @pl.when(j == 0)
    def _():
        x = x_ref[...]
        if prologue == "mod":
            h = x * (1.0 + p1_ref[...]) + p2_ref[...]
        else:
            h = x * lax.rsqrt(jnp.mean(x * x, axis=-1, keepdims=True) + 1e-6) * p1_ref[...]
        h_scr[...] = h.astype(BF16)

    acc = jnp.dot(h_scr[...], w_ref[...], preferred_element_type=F32)
    if epilogue == "relu2":
        r = jnp.maximum(acc, 0.0)
        o_ref[...] = (r * r).astype(o_ref.dtype)
    elif epilogue == "rope":
        for s in range(nsub):
            sl = slice(s * LANES, (s + 1) * LANES)
            md = mode_ref[j * nsub + s]
            sub = acc[:, sl]
            o_ref[:, sl] = (sub * ta_ref[md] + pltpu.roll(sub, LANES // 2, 1) * tb_ref[md]).astype(o_ref.dtype)
    else:
        o_ref[...] = acc.astype(o_ref.dtype)


def mm_fullk(x, x_col_block, k, w, *, prologue, p1, p2=None, epilogue="none", modes=None, ta=None, tb=None,
             out_dtype=BF16, tm=512, tn=1024):
    m = x.shape[0]
    n = w.shape[1]
    tm = min(tm, m)
    assert m % tm == 0 and n % tn == 0 and w.shape[0] == k
    nsub = tn // LANES
    rope = epilogue == "rope"
    npre = 1 if rope else 0

    def im(f):
        return (lambda i, j, *_: f(i, j))

    in_specs = [pl.BlockSpec((tm, k), im(lambda i, j: (i, x_col_block))),
                pl.BlockSpec((1, k), im(lambda i, j: (0, 0)))]
    args = [x, p1.reshape(1, k)]
    if prologue == "mod":
        in_specs.append(pl.BlockSpec((1, k), im(lambda i, j: (0, 0))))
        args.append(p2.reshape(1, k))
    in_specs.append(pl.BlockSpec((k, tn), im(lambda i, j: (0, j))))
    args.append(w)
    if rope:
        nmode = ta.shape[0]
        in_specs += [pl.BlockSpec((nmode, tm, LANES), im(lambda i, j: (0, i, 0)))] * 2
        args += [ta, tb]
    grid_spec = pltpu.PrefetchScalarGridSpec(
        num_scalar_prefetch=npre,
        grid=(m // tm, n // tn),
        in_specs=in_specs,
        out_specs=pl.BlockSpec((tm, tn), im(lambda i, j: (i, j))),
        scratch_shapes=[pltpu.VMEM((tm, k), BF16)],
    )
    fn = pl.pallas_call(
        functools.partial(_mm_fullk_body, prologue=prologue, epilogue=epilogue, nsub=nsub),
        grid_spec=grid_spec,
        out_shape=jax.ShapeDtypeStruct((m, n), out_dtype),
        compiler_params=_params(("parallel", "arbitrary")),
        name="mm_fullk_" + prologue + "_" + epilogue,
    )
    if rope:
        return fn(jnp.asarray(modes, jnp.int32), *args)
    return fn(*args)


def _mm_ln_body(lhs_ref, w_ref, x_ref, g_ref, lng_ref, lnb_ref, o_ref, acc_ref, *, nk):
    kk = pl.program_id(1)

    @pl.when(kk == 0)
    def _():
        acc_ref[...] = jnp.zeros_like(acc_ref)

    acc_ref[...] += jnp.dot(lhs_ref[...], w_ref[...], preferred_element_type=F32)

    @pl.when(kk == nk - 1)
    def _():
        z = ALPHA * x_ref[...] + (1.0 + g_ref[...]) * acc_ref[...]
        mu = jnp.mean(z, axis=-1, keepdims=True)
        zc = z - mu
        var = jnp.mean(zc * zc, axis=-1, keepdims=True)
        o_ref[...] = zc * lax.rsqrt(var + 1e-5) * lng_ref[...] + lnb_ref[...]


def mm_ln(lhs, w, x, gate, ln_g, ln_b, tm=512, tk=512):
    m, k = lhs.shape
    n = w.shape[1]
    tm = min(tm, m)
    nk = k // tk
    row = lambda i, kk: (0, 0)
    return pl.pallas_call(
        functools.partial(_mm_ln_body, nk=nk),
        grid=(m // tm, nk),
        in_specs=[
            pl.BlockSpec((tm, tk), lambda i, kk: (i, kk)),
            pl.BlockSpec((tk, n), lambda i, kk: (kk, 0)),
            pl.BlockSpec((tm, n), lambda i, kk: (i, 0), pipeline_mode=pl.Buffered(1)),
            pl.BlockSpec((1, n), row), pl.BlockSpec((1, n), row), pl.BlockSpec((1, n), row),
        ],
        out_specs=pl.BlockSpec((tm, n), lambda i, kk: (i, 0)),
        out_shape=jax.ShapeDtypeStruct((m, n), F32),
        scratch_shapes=[pltpu.VMEM((tm, n), F32)],
        compiler_params=_params(("parallel", "arbitrary")),
        name="mm_ln",
    )(lhs, w, x, gate.reshape(1, n), ln_g.reshape(1, n), ln_b.reshape(1, n))


def _dil_body(q_ref, kc_ref, kp_ref, vc_ref, vp_ref, o_ref, lse_ref, *, scale, max_delta):
    n = pl.program_id(1)
    qi = lax.broadcasted_iota(jnp.int32, (QBLOCK, 2 * QBLOCK), 0)
    ki = lax.broadcasted_iota(jnp.int32, (QBLOCK, 2 * QBLOCK), 1)
    delta = QBLOCK + qi - ki
    valid = (delta >= 0) & (delta <= max_delta) & ((ki >= QBLOCK) | (n > 0))
    for h in range(DSW_HEADS):
        sl = slice(h * HEAD_DIM, (h + 1) * HEAD_DIM)
        k = jnp.concatenate([kp_ref[:, sl], kc_ref[:, sl]], axis=0)
        v = jnp.concatenate([vp_ref[:, sl], vc_ref[:, sl]], axis=0)
        s = lax.dot_general(q_ref[:, sl], k, NT_DIMS, preferred_element_type=F32) * scale
        s = jnp.where(valid, s, NEG_INF)
        m = jnp.max(s, axis=-1, keepdims=True)
        p = jnp.exp(s - m)
        l = jnp.sum(p, axis=-1, keepdims=True)
        o = jnp.dot(p.astype(BF16), v, preferred_element_type=F32)
        o_ref[:, sl] = o / l
        lse_ref[:, h:h + 1] = m + jnp.log(l)


def dilated_pattern(qkv, s_len, width, q_blk, k_blk, v_blk, window, dil):
    l_len = s_len // dil
    nblk = l_len // QBLOCK
    max_delta = window // dil
    assert max_delta == QBLOCK
    per_row = width // DSW_W
    view = qkv.reshape(l_len, dil * width)
    cur = lambda blk: (lambda r, n: (n, r * per_row + blk))
    prev = lambda blk: (lambda r, n: (jnp.maximum(n - 1, 0), r * per_row + blk))
    bs = lambda f: pl.BlockSpec((QBLOCK, DSW_W), f)
    o, lse = pl.pallas_call(
        functools.partial(_dil_body, scale=HEAD_DIM ** -0.5, max_delta=max_delta),
        grid=(dil, nblk),
        in_specs=[bs(cur(q_blk)), bs(cur(k_blk)), bs(prev(k_blk)), bs(cur(v_blk)), bs(prev(v_blk))],
        out_specs=[pl.BlockSpec((QBLOCK, DSW_W), lambda r, n: (n, r)),
                   pl.BlockSpec((None, QBLOCK, DSW_HEADS), lambda r, n: (r, n, 0))],
        out_shape=[jax.ShapeDtypeStruct((l_len, dil * DSW_W), F32),
                   jax.ShapeDtypeStruct((dil, l_len, DSW_HEADS), F32)],
        compiler_params=_params(("parallel", "parallel")),
        name="dilated_d%d" % dil,
    )(view, view, view, view, view)
    o = o.reshape(s_len, DSW_W)
    lse = lse.transpose(1, 0, 2).reshape(s_len, DSW_HEADS)
    return o, lse


def _dil_combine_body(o1_ref, o2_ref, o3_ref, l1_ref, l2_ref, l3_ref, out_ref):
    ls = [l1_ref[...], l2_ref[...], l3_ref[...]]
    mx = jnp.maximum(jnp.maximum(ls[0], ls[1]), ls[2])
    ws = [jnp.exp(t - mx) for t in ls]
    den = ws[0] + ws[1] + ws[2]
    os_ = [o1_ref, o2_ref, o3_ref]
    for h in range(DSW_HEADS):
        sl = slice(h * HEAD_DIM, (h + 1) * HEAD_DIM)
        num = sum(ws[i][:, h:h + 1] * os_[i][:, sl] for i in range(3))
        out_ref[:, sl] = (num / den[:, h:h + 1]).astype(out_ref.dtype)


def dilated_combine(os_, lses, tm=512):
    s_len = os_[0].shape[0]
    tm = min(tm, s_len)
    ob = pl.BlockSpec((tm, DSW_W), lambda i: (i, 0))
    lb = pl.BlockSpec((tm, DSW_HEADS), lambda i: (i, 0))
    return pl.pallas_call(
        _dil_combine_body,
        grid=(s_len // tm,),
        in_specs=[ob, ob, ob, lb, lb, lb],
        out_specs=ob,
        out_shape=jax.ShapeDtypeStruct((s_len, DSW_W), BF16),
        compiler_params=_params(("parallel",)),
        name="dilated_combine",
    )(*os_, *lses)


def _cmp_body(x_ref, pe_ref, w1_ref, w2_ref, o_ref, *, ncp):
    x = x_ref[...]
    half = NSA_CMP_STRIDE * HEAD_DIM
    a = jnp.dot((x + pe_ref[0:1, :]).astype(BF16), w1_ref[0:half, :], preferred_element_type=F32)
    b = jnp.dot((x + pe_ref[1:2, :]).astype(BF16), w1_ref[half:2 * half, :], preferred_element_type=F32)
    hid = a + pltpu.roll(b, ncp - 1, 0)
    act = jax.nn.gelu(hid)
    o_ref[...] = jnp.dot(act.astype(BF16), w2_ref[...], preferred_element_type=F32).astype(o_ref.dtype)


def nsa_compress(xs, pe, w1, w2):
    ncp = xs.shape[2]
    half = NSA_CMP_STRIDE * HEAD_DIM
    return pl.pallas_call(
        functools.partial(_cmp_body, ncp=ncp),
        grid=(2, NSA_KV_HEADS),
        in_specs=[
            pl.BlockSpec((None, None, ncp, half), lambda a, h: (a, h, 0, 0)),
            pl.BlockSpec((None, 2, half), lambda a, h: (a, 0, 0)),
            pl.BlockSpec((None, 2 * half, NSA_CMP_HIDDEN), lambda a, h: (a, 0, 0)),
            pl.BlockSpec((None, NSA_CMP_HIDDEN, HEAD_DIM), lambda a, h: (a, 0, 0)),
        ],
        out_specs=pl.BlockSpec((None, None, ncp, HEAD_DIM), lambda a, h: (a, h, 0, 0)),
        out_shape=jax.ShapeDtypeStruct((2, NSA_KV_HEADS, ncp, HEAD_DIM), BF16),
        compiler_params=_params(("parallel", "parallel")),
        name="nsa_compress",
    )(xs, pe.reshape(2, 2, half), w1, w2)


def _stack_heads(q):
    return jnp.concatenate([q[:, g * HEAD_DIM:(g + 1) * HEAD_DIM] for g in range(NSA_GROUP)], axis=0)


def _nsa_body(q_ref, ks_ref, vs_ref, kc_ref, vc_ref, covt_ref, e0_ref, gate_ref, u_ref, sel_scr,
              *, scale, ncp, nselp, ntop, tk):
    n = pl.program_id(1)
    rows = NSA_GROUP * QBLOCK
    q4 = _stack_heads(q_ref[...])

    s = lax.dot_general(q4, kc_ref[...], NT_DIMS, preferred_element_type=F32) * scale
    qpos_r = n * QBLOCK + (lax.broadcasted_iota(jnp.int32, (rows, ncp), 0) & (QBLOCK - 1))
    cidx = lax.broadcasted_iota(jnp.int32, (rows, ncp), 1)
    s = jnp.where(cidx * NSA_CMP_STRIDE + (NSA_CMP_LEN - 1) <= qpos_r, s, NEG_INF)
    m = jnp.max(s, axis=-1, keepdims=True)
    e = jnp.exp(s - jnp.where(m == NEG_INF, 0.0, m))
    l = jnp.sum(e, axis=-1, keepdims=True)
    p = e / jnp.where(l > 0, l, 1.0)
    o_c = jnp.dot(p.astype(BF16), vc_ref[...], preferred_element_type=F32)

    psum = p[0:QBLOCK]
    for g in range(1, NSA_GROUP):
        psum = psum + p[g * QBLOCK:(g + 1) * QBLOCK]
    p_hi = psum.astype(BF16)
    p_lo = (psum - p_hi.astype(F32)).astype(BF16)
    covt = covt_ref[...]
    imp_t = (lax.dot_general(covt, p_hi, NT_DIMS, preferred_element_type=F32)
             + lax.dot_general(covt, p_lo, NT_DIMS, preferred_element_type=F32))

    jblk = lax.broadcasted_iota(jnp.int32, (nselp, QBLOCK), 0)
    qpos_c = n * QBLOCK + lax.broadcasted_iota(jnp.int32, (nselp, QBLOCK), 1)
    cur = qpos_c // NSA_SEL_BLOCK
    forced = (jblk == 0) | (jblk == cur) | (jblk == cur - 1)
    valid = jblk * NSA_SEL_BLOCK <= qpos_c
    score0 = jnp.where(valid, jnp.where(forced, jnp.inf, imp_t), NEG_INF)

    def pick(_, carry):
        score, picked = carry
        mx = jnp.max(score, axis=0, keepdims=True)
        idx = jnp.min(jnp.where(score == mx, jblk, nselp), axis=0, keepdims=True)
        hit = jblk == idx
        return jnp.where(hit, NEG_INF, score), jnp.where(hit, 1.0, picked)

    _, picked = lax.fori_loop(0, ntop, pick, (score0, jnp.zeros((nselp, QBLOCK), F32)))
    sel_scr[...] = jnp.where(valid, picked, 0.0).T

    blocks_per_tile = tk // NSA_SEL_BLOCK
    qpos_t = n * QBLOCK + (lax.broadcasted_iota(jnp.int32, (rows, tk), 0) & (QBLOCK - 1))
    kidx_t = lax.broadcasted_iota(jnp.int32, (rows, tk), 1)

    def tile(t, carry, causal):
        m_i, l_i, acc = carry
        k0 = pl.multiple_of(t * tk, tk)
        st = lax.dot_general(q4, ks_ref[pl.ds(k0, tk), :], NT_DIMS, preferred_element_type=F32) * scale
        shift = (nselp - t * blocks_per_tile) % nselp
        sel_t = pltpu.roll(sel_scr[...], shift, 1)[:, 0:LANES].astype(BF16)
        mk = jnp.dot(sel_t, e0_ref[...], preferred_element_type=F32)
        ok = jnp.concatenate([mk] * NSA_GROUP, axis=0) > 0.5
        if causal:
            ok = ok & (k0 + kidx_t <= qpos_t)
        st = jnp.where(ok, st, NEG_INF)
        m_new = jnp.maximum(m_i, jnp.max(st, axis=-1, keepdims=True))
        a = jnp.exp(m_i - m_new)
        pt = jnp.exp(st - m_new)
        l_new = a * l_i + jnp.sum(pt, axis=-1, keepdims=True)
        acc_new = a * acc + jnp.dot(pt.astype(BF16), vs_ref[pl.ds(k0, tk), :], preferred_element_type=F32)
        return m_new, l_new, acc_new

    last = (n * QBLOCK + QBLOCK - 1) // tk
    init = (jnp.full((rows, 1), NEG_INF, F32), jnp.zeros((rows, 1), F32), jnp.zeros((rows, HEAD_DIM), F32))
    carry = lax.fori_loop(0, last, functools.partial(tile, causal=False), init)
    _, l_s, acc_s = tile(last, carry, True)
    o_s = acc_s / l_s

    gates = jax.nn.sigmoid(gate_ref[...])
    for g in range(NSA_GROUP):
        rs = slice(g * QBLOCK, (g + 1) * QBLOCK)
        u_ref[:, g * HEAD_DIM:(g + 1) * HEAD_DIM] = (gates[:, 3 * g:3 * g + 1] * o_c[rs]
                                                     + gates[:, 3 * g + 1:3 * g + 2] * o_s[rs])


def nsa_cmp_sel(qkv, s_len, q_blk, ks_blk, vs_blk, kv_cmp, gates, tk=256):
    nb = s_len // QBLOCK
    ncp = s_len // NSA_CMP_STRIDE
    nsel = s_len // NSA_SEL_BLOCK
    nselp = -(-nsel // LANES) * LANES
    ntop = min(NSA_TOP_N, nsel)
    ci = np.arange(ncp)[None, :] * NSA_CMP_STRIDE
    sj = np.arange(nselp)[:, None] * NSA_SEL_BLOCK
    cov = (ci < sj + NSA_SEL_BLOCK) & (ci + NSA_CMP_LEN > sj) & (np.arange(ncp)[None, :] < ncp - 1) & (sj < s_len)
    covt = jnp.asarray(cov.astype(np.float32), BF16)
    e0 = jnp.asarray((np.arange(tk)[None, :] // NSA_SEL_BLOCK == np.arange(LANES)[:, None]).astype(np.float32), BF16)
    gw = NSA_GROUP * HEAD_DIM
    return pl.pallas_call(
        functools.partial(_nsa_body, scale=HEAD_DIM ** -0.5, ncp=ncp, nselp=nselp, ntop=ntop, tk=tk),
        grid=(NSA_KV_HEADS, nb),
        in_specs=[
            pl.BlockSpec((QBLOCK, gw), lambda h, n: (n, q_blk + h)),
            pl.BlockSpec((s_len, HEAD_DIM), lambda h, n: (0, ks_blk + h)),
            pl.BlockSpec((s_len, HEAD_DIM), lambda h, n: (0, vs_blk + h)),
            pl.BlockSpec((None, None, ncp, HEAD_DIM), lambda h, n: (0, h, 0, 0)),
            pl.BlockSpec((None, None, ncp, HEAD_DIM), lambda h, n: (1, h, 0, 0)),
            pl.BlockSpec((nselp, ncp), lambda h, n: (0, 0)),
            pl.BlockSpec((LANES, tk), lambda h, n: (0, 0)),
            pl.BlockSpec((None, QBLOCK, 3 * NSA_GROUP), lambda h, n: (h, n, 0)),
        ],
        out_specs=pl.BlockSpec((QBLOCK, gw), lambda h, n: (n, h)),
        out_shape=jax.ShapeDtypeStruct((s_len, NSA_QW), F32),
        scratch_shapes=[pltpu.VMEM((QBLOCK, nselp), F32)],
        compiler_params=_params(("parallel", "arbitrary")),
        name="nsa_cmp_sel",
    )(qkv, qkv, qkv, kv_cmp, kv_cmp, covt, e0, gates)


def _win_body(q_ref, kw_ref, vw_ref, gate_ref, u_ref, o_ref, *, scale, span):
    n = pl.program_id(1)
    rows = NSA_GROUP * QBLOCK
    q4 = _stack_heads(q_ref[...])
    start = pl.multiple_of(jnp.maximum(n * QBLOCK - (span - QBLOCK), 0), QBLOCK)
    s = lax.dot_general(q4, kw_ref[pl.ds(start, span), :], NT_DIMS, preferred_element_type=F32) * scale
    qpos = n * QBLOCK + (lax.broadcasted_iota(jnp.int32, (rows, span), 0) & (QBLOCK - 1))
    delta = qpos - (start + lax.broadcasted_iota(jnp.int32, (rows, span), 1))
    s = jnp.where((delta >= 0) & (delta <= NSA_WINDOW - 1), s, NEG_INF)
    m = jnp.max(s, axis=-1, keepdims=True)
    p = jnp.exp(s - m)
    l = jnp.sum(p, axis=-1, keepdims=True)
    o_w = jnp.dot(p.astype(BF16), vw_ref[pl.ds(start, span), :], preferred_element_type=F32) / l
    gates = jax.nn.sigmoid(gate_ref[...])
    for g in range(NSA_GROUP):
        sl = slice(g * HEAD_DIM, (g + 1) * HEAD_DIM)
        o_ref[:, sl] = (u_ref[:, sl] + gates[:, 3 * g + 2:3 * g + 3] * o_w[g * QBLOCK:(g + 1) * QBLOCK]
                        ).astype(o_ref.dtype)


def nsa_window(qkv, s_len, q_blk, kw_blk, vw_blk, gates, u):
    nb = s_len // QBLOCK
    span = (-(-(NSA_WINDOW - 1) // QBLOCK) + 1) * QBLOCK
    gw = NSA_GROUP * HEAD_DIM
    return pl.pallas_call(
        functools.partial(_win_body, scale=HEAD_DIM ** -0.5, span=span),
        grid=(NSA_KV_HEADS, nb),
        in_specs=[
            pl.BlockSpec((QBLOCK, gw), lambda h, n: (n, q_blk + h)),
            pl.BlockSpec((s_len, HEAD_DIM), lambda h, n: (0, kw_blk + h)),
            pl.BlockSpec((s_len, HEAD_DIM), lambda h, n: (0, vw_blk + h)),
            pl.BlockSpec((None, QBLOCK, 3 * NSA_GROUP), lambda h, n: (h, n, 0)),
            pl.BlockSpec((QBLOCK, gw), lambda h, n: (n, h)),
        ],
        out_specs=pl.BlockSpec((QBLOCK, gw), lambda h, n: (n, h)),
        out_shape=jax.ShapeDtypeStruct((s_len, NSA_QW), BF16),
        compiler_params=_params(("parallel", "parallel")),
        name="nsa_window",
    )(qkv, qkv, qkv, gates, u)


def _mla_body(q_ref, kn_ref, kr_ref, v_ref, o_ref, *, scale, tq):
    qi = pl.program_id(1)
    q = q_ref[...]
    row = lax.broadcasted_iota(jnp.int32, (tq, tq), 0)
    col = lax.broadcasted_iota(jnp.int32, (tq, tq), 1)

    def tile(t, carry, diag):
        m_i, l_i, acc = carry
        k0 = pl.multiple_of(t * tq, tq)
        kt = jnp.concatenate([kn_ref[pl.ds(k0, tq), :], kr_ref[pl.ds(k0, tq), :]], axis=1)
        s = lax.dot_general(q, kt, NT_DIMS, preferred_element_type=F32) * scale
        if diag:
            s = jnp.where(col <= row, s, NEG_INF)
        m_new = jnp.maximum(m_i, jnp.max(s, axis=-1, keepdims=True))
        a = jnp.exp(m_i - m_new)
        p = jnp.exp(s - m_new)
        l_new = a * l_i + jnp.sum(p, axis=-1, keepdims=True)
        acc_new = a * acc + jnp.dot(p.astype(BF16), v_ref[pl.ds(k0, tq), :], preferred_element_type=F32)
        return m_new, l_new, acc_new

    init = (jnp.full((tq, 1), NEG_INF, F32), jnp.zeros((tq, 1), F32), jnp.zeros((tq, MLA_V_DIM), F32))
    carry = tile(qi, init, True)
    _, l_f, acc_f = lax.fori_loop(0, qi, functools.partial(tile, diag=False), carry)
    o_ref[...] = (acc_f / l_f).astype(o_ref.dtype)


def mla_attention(q, kv, k_rope, s_len, tq=512):
    tq = min(tq, s_len)
    qw = 2 * LANES
    return pl.pallas_call(
        functools.partial(_mla_body, scale=(MLA_NOPE_DIM + MLA_ROPE_DIM) ** -0.5, tq=tq),
        grid=(MLA_HEADS, s_len // tq),
        in_specs=[
            pl.BlockSpec((tq, qw), lambda h, i: (i, h)),
            pl.BlockSpec((s_len, MLA_NOPE_DIM), lambda h, i: (0, h)),
            pl.BlockSpec((s_len, LANES), lambda h, i: (0, 0)),
            pl.BlockSpec((s_len, MLA_V_DIM), lambda h, i: (0, MLA_HEADS + h)),
        ],
        out_specs=pl.BlockSpec((tq, MLA_V_DIM), lambda h, i: (i, h)),
        out_shape=jax.ShapeDtypeStruct((s_len, MLA_HEADS * MLA_V_DIM), BF16),
        compiler_params=_params(("parallel", "parallel")),
        name="mla_attention",
    )(q, kv, k_rope, kv)


_IN_SIZES = (DSW_W, DSW_W, DSW_W, NSA_QW, NSA_KVW, NSA_KVW, NSA_KVW, NSA_KVW, NSA_KVW, NSA_KVW,
             3 * NSA_HEADS, MLA_Q_LORA, MLA_KV_LORA, MLA_ROPE_DIM)
_IN_NAMES = ("a_q", "a_k", "a_v", "n_q", "n_kc", "n_vc", "n_ks", "n_vs", "n_kw", "n_vw", "n_gate",
             "m_cq", "m_ckv", "m_kr")
_IN_OFF = dict(zip(_IN_NAMES, np.concatenate([[0], np.cumsum(_IN_SIZES)[:-1]]).tolist()))
_IN_LEN = dict(zip(_IN_NAMES, _IN_SIZES))

_B_ORDER = ("a_q", "a_k", "n_q", "n_ks", "n_kw", "n_vs", "n_vw", "a_v")
_B_ROPE = 2 * DSW_W + NSA_QW + 2 * NSA_KVW
_B_WIDTH = sum(_IN_LEN[k] for k in _B_ORDER)
_B_COL = dict(zip(_B_ORDER, np.concatenate([[0], np.cumsum([_IN_LEN[k] for k in _B_ORDER])[:-1]]).tolist()))
_F_COL = {"m_cq": 0, "m_ckv": MLA_Q_LORA, "n_kc": 2048, "n_vc": 2304, "m_kr": 2560, "n_gate": 2688}
_F_WIDTH = 3072
_HALF_ROPE = MLA_ROPE_DIM // 2


def _spread_rope_cols(w):
    z = jnp.zeros((w.shape[0], _HALF_ROPE), w.dtype)
    return jnp.concatenate([w[:, :_HALF_ROPE], z, w[:, _HALF_ROPE:], z], axis=1)


def _prep_w_in(w):
    col = lambda name: w[:, _IN_OFF[name]:_IN_OFF[name] + _IN_LEN[name]]
    wb = jnp.concatenate([col(k) for k in _B_ORDER], axis=1).astype(BF16)
    used = _F_COL["n_gate"] + _IN_LEN["n_gate"]
    wf = jnp.concatenate([col("m_cq"), col("m_ckv"), col("n_kc"), col("n_vc"), _spread_rope_cols(col("m_kr")),
                          col("n_gate"), jnp.zeros((w.shape[0], _F_WIDTH - used), w.dtype)], axis=1).astype(BF16)
    return wb, wf


_B_MODES = [1] * (_B_ROPE // LANES) + [0] * ((_B_WIDTH - _B_ROPE) // LANES)
_F_MODES = [0] * 16 + [1, 1, 0, 0, 2] + [0] * 3
_Q_MODES = [0, 2] * MLA_HEADS


def _prep_w_uq(w):
    w = w.reshape(MLA_Q_LORA, MLA_HEADS, MLA_NOPE_DIM + MLA_ROPE_DIM)
    z = jnp.zeros((MLA_Q_LORA, MLA_HEADS, _HALF_ROPE), w.dtype)
    w = jnp.concatenate([w[..., :MLA_NOPE_DIM], w[..., MLA_NOPE_DIM:MLA_NOPE_DIM + _HALF_ROPE], z,
                         w[..., MLA_NOPE_DIM + _HALF_ROPE:], z], axis=-1)
    return w.reshape(MLA_Q_LORA, MLA_HEADS * 2 * LANES).astype(BF16)


def _prep_w_ukv(w):
    w = w.reshape(MLA_KV_LORA, MLA_HEADS, 2, MLA_NOPE_DIM).transpose(0, 2, 1, 3)
    return w.reshape(MLA_KV_LORA, 2 * MLA_HEADS * MLA_NOPE_DIM).astype(BF16)


def _rope_tables(positions):
    pos = positions.astype(F32)[:, None]

    def cs(dim):
        inv = ROPE_THETA ** (-jnp.arange(0, dim, 2, dtype=F32) / dim)
        ang = pos * inv
        return jnp.cos(ang), jnp.sin(ang)

    c128, s128 = cs(HEAD_DIM)
    c64, s64 = cs(MLA_ROPE_DIM)
    z = jnp.zeros_like(c64)
    ta = jnp.stack([jnp.ones((pos.shape[0], LANES), F32), jnp.concatenate([c128, c128], 1),
                    jnp.concatenate([c64, z, c64, z], 1)])
    tb = jnp.stack([jnp.zeros((pos.shape[0], LANES), F32), jnp.concatenate([-s128, s128], 1),
                    jnp.concatenate([-s64, z, s64, z], 1)])
    return ta, tb


def _mixer(x, sc1, sh1, ta, tb, w_in, cmp_pe, cmp_w1, cmp_w2, q_norm, kv_norm, w_uq, w_ukv):
    s_len = x.shape[0]
    wb, wf = _prep_w_in(w_in)
    pb = mm_fullk(x, 0, D_MODEL, wb, prologue="mod", p1=sc1, p2=sh1, epilogue="rope", modes=_B_MODES,
                  ta=ta, tb=tb, out_dtype=BF16)
    pf = mm_fullk(x, 0, D_MODEL, wf, prologue="mod", p1=sc1, p2=sh1, epilogue="rope", modes=_F_MODES,
                  ta=ta, tb=tb, out_dtype=F32)

    blk = lambda name: _B_COL[name] // DSW_W
    outs = [dilated_pattern(pb, s_len, _B_WIDTH, blk("a_q"), blk("a_k"), blk("a_v"), window, dil)
            for window, dil in DSW_PATTERNS]
    out_a = dilated_combine([o for o, _ in outs], [t for _, t in outs])

    ncp = s_len // NSA_CMP_STRIDE

    def blocks16(name):
        t = pf[:, _F_COL[name]:_F_COL[name] + NSA_KVW]
        return t.reshape(ncp, NSA_CMP_STRIDE, NSA_KV_HEADS, HEAD_DIM).transpose(2, 0, 1, 3).reshape(
            NSA_KV_HEADS, ncp, NSA_CMP_STRIDE * HEAD_DIM)

    kv_cmp = nsa_compress(jnp.stack([blocks16("n_kc"), blocks16("n_vc")]), cmp_pe,
                          cmp_w1.astype(BF16), cmp_w2.astype(BF16))
    gates = pf[:, _F_COL["n_gate"]:_F_COL["n_gate"] + 3 * NSA_HEADS]
    gates = gates.reshape(s_len, NSA_KV_HEADS, 3 * NSA_GROUP).transpose(1, 0, 2)
    gw = NSA_GROUP * HEAD_DIM
    u = nsa_cmp_sel(pb, s_len, _B_COL["n_q"] // gw, _B_COL["n_ks"] // HEAD_DIM, _B_COL["n_vs"] // HEAD_DIM,
                    kv_cmp, gates)
    out_b = nsa_window(pb, s_len, _B_COL["n_q"] // gw, _B_COL["n_kw"] // HEAD_DIM, _B_COL["n_vw"] // HEAD_DIM,
                       gates, u)

    q = mm_fullk(pf, 0, MLA_Q_LORA, _prep_w_uq(w_uq), prologue="rms", p1=q_norm, epilogue="rope",
                 modes=_Q_MODES, ta=ta, tb=tb, out_dtype=BF16)
    kv = mm_fullk(pf, _F_COL["m_ckv"] // MLA_KV_LORA, MLA_KV_LORA, _prep_w_ukv(w_ukv), prologue="rms",
                  p1=kv_norm, out_dtype=BF16)
    k_rope = pf[:, _F_COL["m_kr"]:_F_COL["m_kr"] + LANES].astype(BF16)
    out_c = mla_attention(q, kv, k_rope, s_len)
    return jnp.concatenate([out_a, out_b, out_c], axis=1)


def kernel(x, c, positions, w_ada, b_ada, w_in, nsa_cmp_pe, nsa_cmp_w1, nsa_cmp_w2, mla_q_norm, mla_kv_norm,
           mla_w_uq, mla_w_ukv, w_out, ln1_g, ln1_b, mlp_w1, mlp_w2, ln2_g, ln2_b):
    assert x.shape[0] == 1, "kernel handles batch size 1"
    xs = x[0]
    d = xs.shape[1]
    ta, tb = _rope_tables(positions[0])
    for l in range(DEPTH):
        mod = adaln(c, w_ada[l], b_ada[l])
        sh1, sc1, g1, sh2, sc2, g2 = [mod[:, i * d:(i + 1) * d] for i in range(6)]
        mixed = _mixer(xs, sc1, sh1, ta, tb, w_in[l], nsa_cmp_pe[l], nsa_cmp_w1[l], nsa_cmp_w2[l],
                       mla_q_norm[l], mla_kv_norm[l], mla_w_uq[l], mla_w_ukv[l])
        xs = mm_ln(mixed, w_out[l].astype(BF16), xs, g1, ln1_g[l], ln1_b[l])
        act = mm_fullk(xs, 0, d, mlp_w1[l].astype(BF16), prologue="mod", p1=sc2, p2=sh2, epilogue="relu2",
                       out_dtype=BF16)
        xs = mm_ln(act, mlp_w2[l].astype(BF16), xs, g2, ln2_g[l], ln2_b[l])
    return xs[None]
```

```python
import functools

import jax
import jax.numpy as jnp
import numpy as np
from jax import lax
from jax.experimental import pallas as pl
from jax.experimental.pallas import tpu as pltpu

D_MODEL = 4096
DEPTH = 2
HEAD_DIM = 128
ROPE_THETA = 10000.0
QBLOCK = 128
DSW_HEADS = 8
DSW_PATTERNS = ((128, 1), (512, 4), (2048, 16))
NSA_HEADS = 8
NSA_KV_HEADS = 2
NSA_GROUP = NSA_HEADS // NSA_KV_HEADS
NSA_CMP_LEN = 32
NSA_CMP_STRIDE = 16
NSA_CMP_HIDDEN = 256
NSA_SEL_BLOCK = 64
NSA_TOP_N = 16
NSA_WINDOW = 512
MLA_HEADS = 16
MLA_Q_LORA = 1536
MLA_KV_LORA = 512
MLA_NOPE_DIM = 128
MLA_ROPE_DIM = 64
MLA_V_DIM = 128
D_FF = 4 * D_MODEL
ALPHA = (2 * DEPTH) ** 0.25

DSW_W = DSW_HEADS * HEAD_DIM
NSA_QW = NSA_HEADS * HEAD_DIM
NSA_KVW = NSA_KV_HEADS * HEAD_DIM

LANES = 128
V7X_VMEM_LIMIT = 56 * 1024 * 1024

F32 = jnp.float32
BF16 = jnp.bfloat16
NEG_INF = float("-inf")
LOG2E = 1.4426950408889634
FLASH_ROWS = 64
FLASH_GROUP = 2
NT_DIMS = (((1,), (1,)), ((), ()))


def _params(sem, vmem=V7X_VMEM_LIMIT):
    return pltpu.CompilerParams(dimension_semantics=sem, vmem_limit_bytes=vmem)


def _adaln_body(cb_ref, w_ref, b_ref, o_ref, *, tn):
    cb = cb_ref[...]
    for s in range(tn // LANES):
        sl = slice(s * LANES, (s + 1) * LANES)
        o_ref[:, sl] = jnp.sum(w_ref[:, sl] * cb, axis=0, keepdims=True) + b_ref[:, sl]


def adaln(c, w, b, tn=512):
    d, n = w.shape
    cb = jnp.broadcast_to(c.reshape(d, 1), (d, LANES))
    return pl.pallas_call(
        functools.partial(_adaln_body, tn=tn),
        grid=(n // tn,),
        in_specs=[
            pl.BlockSpec((d, LANES), lambda j: (0, 0)),
            pl.BlockSpec((d, tn), lambda j: (0, j)),
            pl.BlockSpec((1, tn), lambda j: (0, j)),
        ],
        out_specs=pl.BlockSpec((1, tn), lambda j: (0, j)),
        out_shape=jax.ShapeDtypeStruct((1, n), F32),
        compiler_params=_params(("arbitrary",)),
        name="adaln",
    )(cb, w, b.reshape(1, n))


def _mm_fullk_body(*refs, prologue, epilogue, nsub):
    refs = list(refs)
    mode_ref = refs.pop(0) if epilogue == "rope" else None
    x_ref, p1_ref = refs.pop(0), refs.pop(0)
    p2_ref = refs.pop(0) if prologue == "mod" else None
    w_ref = refs.pop(0)
    if epilogue == "rope":
        ta_ref, tb_ref = refs.pop(0), refs.pop(0)
    o_ref, h_scr = refs
    j = pl.program_id(1)

    @pl.when(j == 0)
    def _():
        x = x_ref[...]
        if prologue == "mod":
            h = x * (1.0 + p1_ref[...]) + p2_ref[...]
        else:
            h = x * lax.rsqrt(jnp.mean(x * x, axis=-1, keepdims=True) + 1e-6) * p1_ref[...]
        h_scr[...] = h.astype(BF16)

    acc = jnp.dot(h_scr[...], w_ref[...], preferred_element_type=F32)
    if epilogue == "relu2":
        r = jnp.maximum(acc, 0.0)
        o_ref[...] = (r * r).astype(o_ref.dtype)
    elif epilogue == "rope":
        for s in range(nsub):
            sl = slice(s * LANES, (s + 1) * LANES)
            md = mode_ref[j * nsub + s]
            sub = acc[:, sl]
            o_ref[:, sl] = (sub * ta_ref[md] + pltpu.roll(sub, LANES // 2, 1) * tb_ref[md]).astype(o_ref.dtype)
    else:
        o_ref[...] = acc.astype(o_ref.dtype)


def mm_fullk(x, x_col_block, k, w, *, prologue, p1, p2=None, epilogue="none", modes=None, ta=None, tb=None,
             out_dtype=BF16, tm=512, tn=1024):
    m = x.shape[0]
    n = w.shape[1]
    tm = min(tm, m)
    assert m % tm == 0 and n % tn == 0 and w.shape[0] == k
    nsub = tn // LANES
    rope = epilogue == "rope"
    npre = 1 if rope else 0

    def im(f):
        return (lambda i, j, *_: f(i, j))

    in_specs = [pl.BlockSpec((tm, k), im(lambda i, j: (i, x_col_block))),
                pl.BlockSpec((1, k), im(lambda i, j: (0, 0)))]
    args = [x, p1.reshape(1, k)]
    if prologue == "mod":
        in_specs.append(pl.BlockSpec((1, k), im(lambda i, j: (0, 0))))
        args.append(p2.reshape(1, k))
    in_specs.append(pl.BlockSpec((k, tn), im(lambda i, j: (0, j))))
    args.append(w)
    if rope:
        nmode = ta.shape[0]
        in_specs += [pl.BlockSpec((nmode, tm, LANES), im(lambda i, j: (0, i, 0)))] * 2
        args += [ta, tb]
    grid_spec = pltpu.PrefetchScalarGridSpec(
        num_scalar_prefetch=npre,
        grid=(m // tm, n // tn),
        in_specs=in_specs,
        out_specs=pl.BlockSpec((tm, tn), im(lambda i, j: (i, j))),
        scratch_shapes=[pltpu.VMEM((tm, k), BF16)],
    )
    fn = pl.pallas_call(
        functools.partial(_mm_fullk_body, prologue=prologue, epilogue=epilogue, nsub=nsub),
        grid_spec=grid_spec,
        out_shape=jax.ShapeDtypeStruct((m, n), out_dtype),
        compiler_params=_params(("parallel", "arbitrary")),
        name="mm_fullk_" + prologue + "_" + epilogue,
    )
    if rope:
        return fn(jnp.asarray(modes, jnp.int32), *args)
    return fn(*args)


def _mm_ln_body(lhs_ref, w_ref, x_ref, g_ref, lng_ref, lnb_ref, o_ref, acc_ref, *, nk):
    kk = pl.program_id(1)

    @pl.when(kk == 0)
    def _():
        acc_ref[...] = jnp.zeros_like(acc_ref)

    acc_ref[...] += jnp.dot(lhs_ref[...], w_ref[...], preferred_element_type=F32)

    @pl.when(kk == nk - 1)
    def _():
        z = ALPHA * x_ref[...] + (1.0 + g_ref[...]) * acc_ref[...]
        mu = jnp.mean(z, axis=-1, keepdims=True)
        zc = z - mu
        var = jnp.mean(zc * zc, axis=-1, keepdims=True)
        o_ref[...] = zc * lax.rsqrt(var + 1e-5) * lng_ref[...] + lnb_ref[...]


def mm_ln(lhs, w, x, gate, ln_g, ln_b, tm=512, tk=512):
    m, k = lhs.shape
    n = w.shape[1]
    tm = min(tm, m)
    nk = k // tk
    row = lambda i, kk: (0, 0)
    return pl.pallas_call(
        functools.partial(_mm_ln_body, nk=nk),
        grid=(m // tm, nk),
        in_specs=[
            pl.BlockSpec((tm, tk), lambda i, kk: (i, kk)),
            pl.BlockSpec((tk, n), lambda i, kk: (kk, 0)),
            pl.BlockSpec((tm, n), lambda i, kk: (i, 0), pipeline_mode=pl.Buffered(1)),
            pl.BlockSpec((1, n), row), pl.BlockSpec((1, n), row), pl.BlockSpec((1, n), row),
        ],
        out_specs=pl.BlockSpec((tm, n), lambda i, kk: (i, 0)),
        out_shape=jax.ShapeDtypeStruct((m, n), F32),
        scratch_shapes=[pltpu.VMEM((tm, n), F32)],
        compiler_params=_params(("parallel", "arbitrary")),
        name="mm_ln",
    )(lhs, w, x, gate.reshape(1, n), ln_g.reshape(1, n), ln_b.reshape(1, n))


def _dil_body(q_ref, kc_ref, kp_ref, vc_ref, vp_ref, o_ref, lse_ref, *, scale, max_delta):
    n = pl.program_id(1)
    qi = lax.broadcasted_iota(jnp.int32, (QBLOCK, 2 * QBLOCK), 0)
    ki = lax.broadcasted_iota(jnp.int32, (QBLOCK, 2 * QBLOCK), 1)
    delta = QBLOCK + qi - ki
    valid = (delta >= 0) & (delta <= max_delta) & ((ki >= QBLOCK) | (n > 0))
    for h in range(DSW_HEADS):
        sl = slice(h * HEAD_DIM, (h + 1) * HEAD_DIM)
        k = jnp.concatenate([kp_ref[:, sl], kc_ref[:, sl]], axis=0)
        v = jnp.concatenate([vp_ref[:, sl], vc_ref[:, sl]], axis=0)
        s = lax.dot_general(q_ref[:, sl], k, NT_DIMS, preferred_element_type=F32) * scale
        s = jnp.where(valid, s, NEG_INF)
        m = jnp.max(s, axis=-1, keepdims=True)
        p = jnp.exp(s - m)
        l = jnp.sum(p, axis=-1, keepdims=True)
        o = jnp.dot(p.astype(BF16), v, preferred_element_type=F32)
        o_ref[:, sl] = o / l
        lse_ref[:, h:h + 1] = m + jnp.log(l)


def dilated_pattern(qkv, s_len, width, q_blk, k_blk, v_blk, window, dil):
    l_len = s_len // dil
    nblk = l_len // QBLOCK
    max_delta = window // dil
    assert max_delta == QBLOCK
    per_row = width // DSW_W
    view = qkv.reshape(l_len, dil * width)
    cur = lambda blk: (lambda r, n: (n, r * per_row + blk))
    prev = lambda blk: (lambda r, n: (jnp.maximum(n - 1, 0), r * per_row + blk))
    bs = lambda f: pl.BlockSpec((QBLOCK, DSW_W), f)
    o, lse = pl.pallas_call(
        functools.partial(_dil_body, scale=HEAD_DIM ** -0.5, max_delta=max_delta),
        grid=(dil, nblk),
        in_specs=[bs(cur(q_blk)), bs(cur(k_blk)), bs(prev(k_blk)), bs(cur(v_blk)), bs(prev(v_blk))],
        out_specs=[pl.BlockSpec((QBLOCK, DSW_W), lambda r, n: (n, r)),
                   pl.BlockSpec((None, QBLOCK, DSW_HEADS), lambda r, n: (r, n, 0))],
        out_shape=[jax.ShapeDtypeStruct((l_len, dil * DSW_W), F32),
                   jax.ShapeDtypeStruct((dil, l_len, DSW_HEADS), F32)],
        compiler_params=_params(("parallel", "parallel")),
        name="dilated_d%d" % dil,
    )(view, view, view, view, view)
    o = o.reshape(s_len, DSW_W)
    lse = lse.transpose(1, 0, 2).reshape(s_len, DSW_HEADS)
    return o, lse


def _dil_combine_body(o1_ref, o2_ref, o3_ref, l1_ref, l2_ref, l3_ref, out_ref):
    ls = [l1_ref[...], l2_ref[...], l3_ref[...]]
    mx = jnp.maximum(jnp.maximum(ls[0], ls[1]), ls[2])
    ws = [jnp.exp(t - mx) for t in ls]
    den = ws[0] + ws[1] + ws[2]
    os_ = [o1_ref, o2_ref, o3_ref]
    for h in range(DSW_HEADS):
        sl = slice(h * HEAD_DIM, (h + 1) * HEAD_DIM)
        num = sum(ws[i][:, h:h + 1] * os_[i][:, sl] for i in range(3))
        out_ref[:, sl] = (num / den[:, h:h + 1]).astype(out_ref.dtype)


def dilated_combine(os_, lses, tm=512):
    s_len = os_[0].shape[0]
    tm = min(tm, s_len)
    ob = pl.BlockSpec((tm, DSW_W), lambda i: (i, 0))
    lb = pl.BlockSpec((tm, DSW_HEADS), lambda i: (i, 0))
    return pl.pallas_call(
        _dil_combine_body,
        grid=(s_len // tm,),
        in_specs=[ob, ob, ob, lb, lb, lb],
        out_specs=ob,
        out_shape=jax.ShapeDtypeStruct((s_len, DSW_W), BF16),
        compiler_params=_params(("parallel",)),
        name="dilated_combine",
    )(*os_, *lses)


def _cmp_body(x_ref, pe_ref, w1_ref, w2_ref, o_ref, *, ncp):
    x = x_ref[...]
    half = NSA_CMP_STRIDE * HEAD_DIM
    a = jnp.dot((x + pe_ref[0:1, :]).astype(BF16), w1_ref[0:half, :], preferred_element_type=F32)
    b = jnp.dot((x + pe_ref[1:2, :]).astype(BF16), w1_ref[half:2 * half, :], preferred_element_type=F32)
    hid = a + pltpu.roll(b, ncp - 1, 0)
    act = jax.nn.gelu(hid)
    o_ref[...] = jnp.dot(act.astype(BF16), w2_ref[...], preferred_element_type=F32).astype(o_ref.dtype)


def nsa_compress(xs, pe, w1, w2):
    ncp = xs.shape[2]
    half = NSA_CMP_STRIDE * HEAD_DIM
    return pl.pallas_call(
        functools.partial(_cmp_body, ncp=ncp),
        grid=(2, NSA_KV_HEADS),
        in_specs=[
            pl.BlockSpec((None, None, ncp, half), lambda a, h: (a, h, 0, 0)),
            pl.BlockSpec((None, 2, half), lambda a, h: (a, 0, 0)),
            pl.BlockSpec((None, 2 * half, NSA_CMP_HIDDEN), lambda a, h: (a, 0, 0)),
            pl.BlockSpec((None, NSA_CMP_HIDDEN, HEAD_DIM), lambda a, h: (a, 0, 0)),
        ],
        out_specs=pl.BlockSpec((None, None, ncp, HEAD_DIM), lambda a, h: (a, h, 0, 0)),
        out_shape=jax.ShapeDtypeStruct((2, NSA_KV_HEADS, ncp, HEAD_DIM), BF16),
        compiler_params=_params(("parallel", "parallel")),
        name="nsa_compress",
    )(xs, pe.reshape(2, 2, half), w1, w2)


def _stack_heads(q):
    return jnp.concatenate([q[:, g * HEAD_DIM:(g + 1) * HEAD_DIM] for g in range(NSA_GROUP)], axis=0)


def _nsa_body(q_ref, ks_ref, vs_ref, kc_ref, vc_ref, covt_ref, e0_ref, gate_ref, u_ref,
              sel_scr, q4_scr, mk_scr, s_scr, p_scr, a_scr, m_scr, l_scr, acc_scr,
              *, scale, ncp, nselp, ntop, tk):
    n = pl.program_id(1)
    rows = NSA_GROUP * QBLOCK
    q4_scr[...] = _stack_heads(q_ref[...])
    q4 = q4_scr[...]

    s = lax.dot_general(q4, kc_ref[...], NT_DIMS, preferred_element_type=F32) * scale
    qpos_r = n * QBLOCK + (lax.broadcasted_iota(jnp.int32, (rows, ncp), 0) & (QBLOCK - 1))
    cidx = lax.broadcasted_iota(jnp.int32, (rows, ncp), 1)
    s = jnp.where(cidx * NSA_CMP_STRIDE + (NSA_CMP_LEN - 1) <= qpos_r, s, NEG_INF)
    m = jnp.max(s, axis=-1, keepdims=True)
    e = jnp.exp(s - jnp.where(m == NEG_INF, 0.0, m))
    l = jnp.sum(e, axis=-1, keepdims=True)
    p = e / jnp.where(l > 0, l, 1.0)
    o_c = jnp.dot(p.astype(BF16), vc_ref[...], preferred_element_type=F32)

    psum = p[0:QBLOCK]
    for g in range(1, NSA_GROUP):
        psum = psum + p[g * QBLOCK:(g + 1) * QBLOCK]
    p_hi = psum.astype(BF16)
    p_lo = (psum - p_hi.astype(F32)).astype(BF16)
    covt = covt_ref[...]
    imp_t = (lax.dot_general(covt, p_hi, NT_DIMS, preferred_element_type=F32)
             + lax.dot_general(covt, p_lo, NT_DIMS, preferred_element_type=F32))

    jblk = lax.broadcasted_iota(jnp.int32, (nselp, QBLOCK), 0)
    qpos_c = n * QBLOCK + lax.broadcasted_iota(jnp.int32, (nselp, QBLOCK), 1)
    cur = qpos_c // NSA_SEL_BLOCK
    forced = (jblk == 0) | (jblk == cur) | (jblk == cur - 1)
    valid = jblk * NSA_SEL_BLOCK <= qpos_c
    score0 = jnp.where(valid, jnp.where(forced, jnp.inf, imp_t), NEG_INF)

    def pick(_, carry):
        score, picked = carry
        mx = jnp.max(score, axis=0, keepdims=True)
        idx = jnp.min(jnp.where(score == mx, jblk, nselp), axis=0, keepdims=True)
        hit = jblk == idx
        return jnp.where(hit, NEG_INF, score), jnp.where(hit, 1.0, picked)

    _, picked = lax.fori_loop(0, ntop, pick, (score0, jnp.zeros((nselp, QBLOCK), F32)))
    sel_scr[...] = jnp.where(valid, picked, 0.0).T

    blocks_per_tile = tk // NSA_SEL_BLOCK
    c = scale * LOG2E
    _flash_init(m_scr, l_scr, acc_scr)

    def group(items):
        tiles = []
        for g, (t, masked) in enumerate(items):
            k0 = pl.multiple_of(t * tk, tk)
            shift = (nselp - t * blocks_per_tile) % nselp
            sel_t = pltpu.roll(sel_scr[...], shift, 1)[:, 0:LANES].astype(BF16)
            mk_scr[g] = jnp.dot(sel_t, e0_ref[...], preferred_element_type=F32)

            def mask_fn(r, s, g=g, k0=k0, masked=masked):
                rq = r % QBLOCK
                ok = mk_scr[g, rq:rq + FLASH_ROWS, :] > 0.5
                if masked:
                    qpos = n * QBLOCK + rq + lax.broadcasted_iota(jnp.int32, s.shape, 0)
                    ok = ok & (k0 + lax.broadcasted_iota(jnp.int32, s.shape, 1) <= qpos)
                return jnp.where(ok, s, NEG_INF)

            tiles.append((lambda k0=k0: ks_ref[pl.ds(k0, tk), :], lambda k0=k0: vs_ref[pl.ds(k0, tk), :], mask_fn))
        _flash_group(q4_scr, tiles, s_scr, p_scr, a_scr, m_scr, l_scr, acc_scr, c)

    _flash_sweep((n * QBLOCK + QBLOCK - 1) // tk, group)
    o_s = acc_scr[...] / jnp.sum(l_scr[...], axis=1, keepdims=True)

    gates = jax.nn.sigmoid(gate_ref[...])
    for g in range(NSA_GROUP):
        rs = slice(g * QBLOCK, (g + 1) * QBLOCK)
        u_ref[:, g * HEAD_DIM:(g + 1) * HEAD_DIM] = (gates[:, 3 * g:3 * g + 1] * o_c[rs]
                                                     + gates[:, 3 * g + 1:3 * g + 2] * o_s[rs])


def nsa_cmp_sel(qkv, s_len, q_blk, ks_blk, vs_blk, kv_cmp, gates, tk=512):
    nb = s_len // QBLOCK
    ncp = s_len // NSA_CMP_STRIDE
    nsel = s_len // NSA_SEL_BLOCK
    nselp = -(-nsel // LANES) * LANES
    ntop = min(NSA_TOP_N, nsel)
    ci = np.arange(ncp)[None, :] * NSA_CMP_STRIDE
    sj = np.arange(nselp)[:, None] * NSA_SEL_BLOCK
    cov = (ci < sj + NSA_SEL_BLOCK) & (ci + NSA_CMP_LEN > sj) & (np.arange(ncp)[None, :] < ncp - 1) & (sj < s_len)
    covt = jnp.asarray(cov.astype(np.float32), BF16)
    e0 = jnp.asarray((np.arange(tk)[None, :] // NSA_SEL_BLOCK == np.arange(LANES)[:, None]).astype(np.float32), BF16)
    gw = NSA_GROUP * HEAD_DIM
    rows = NSA_GROUP * QBLOCK
    return pl.pallas_call(
        functools.partial(_nsa_body, scale=HEAD_DIM ** -0.5, ncp=ncp, nselp=nselp, ntop=ntop, tk=tk),
        grid=(NSA_KV_HEADS, nb),
        in_specs=[
            pl.BlockSpec((QBLOCK, gw), lambda h, n: (n, q_blk + h)),
            pl.BlockSpec((s_len, HEAD_DIM), lambda h, n: (0, ks_blk + h)),
            pl.BlockSpec((s_len, HEAD_DIM), lambda h, n: (0, vs_blk + h)),
            pl.BlockSpec((None, None, ncp, HEAD_DIM), lambda h, n: (0, h, 0, 0)),
            pl.BlockSpec((None, None, ncp, HEAD_DIM), lambda h, n: (1, h, 0, 0)),
            pl.BlockSpec((nselp, ncp), lambda h, n: (0, 0)),
            pl.BlockSpec((LANES, tk), lambda h, n: (0, 0)),
            pl.BlockSpec((None, QBLOCK, 3 * NSA_GROUP), lambda h, n: (h, n, 0)),
        ],
        out_specs=pl.BlockSpec((QBLOCK, gw), lambda h, n: (n, h)),
        out_shape=jax.ShapeDtypeStruct((s_len, NSA_QW), F32),
        scratch_shapes=[pltpu.VMEM((QBLOCK, nselp), F32), pltpu.VMEM((rows, HEAD_DIM), BF16),
                        pltpu.VMEM((FLASH_GROUP, QBLOCK, tk), F32), pltpu.VMEM((FLASH_GROUP, rows, tk), F32),
                        pltpu.VMEM((FLASH_GROUP, rows, tk), BF16), pltpu.VMEM((FLASH_GROUP, rows, LANES), F32),
                        pltpu.VMEM((rows, LANES), F32), pltpu.VMEM((rows, LANES), F32),
                        pltpu.VMEM((rows, HEAD_DIM), F32)],
        compiler_params=_params(("parallel", "arbitrary")),
        name="nsa_cmp_sel",
    )(qkv, qkv, qkv, kv_cmp, kv_cmp, covt, e0, gates)


def _win_body(q_ref, kw_ref, vw_ref, gate_ref, u_ref, o_ref, *, scale, span):
    n = pl.program_id(1)
    rows = NSA_GROUP * QBLOCK
    q4 = _stack_heads(q_ref[...])
    start = pl.multiple_of(jnp.maximum(n * QBLOCK - (span - QBLOCK), 0), QBLOCK)
    s = lax.dot_general(q4, kw_ref[pl.ds(start, span), :], NT_DIMS, preferred_element_type=F32) * scale
    qpos = n * QBLOCK + (lax.broadcasted_iota(jnp.int32, (rows, span), 0) & (QBLOCK - 1))
    delta = qpos - (start + lax.broadcasted_iota(jnp.int32, (rows, span), 1))
    s = jnp.where((delta >= 0) & (delta <= NSA_WINDOW - 1), s, NEG_INF)
    m = jnp.max(s, axis=-1, keepdims=True)
    p = jnp.exp(s - m)
    l = jnp.sum(p, axis=-1, keepdims=True)
    o_w = jnp.dot(p.astype(BF16), vw_ref[pl.ds(start, span), :], preferred_element_type=F32) / l
    gates = jax.nn.sigmoid(gate_ref[...])
    for g in range(NSA_GROUP):
        sl = slice(g * HEAD_DIM, (g + 1) * HEAD_DIM)
        o_ref[:, sl] = (u_ref[:, sl] + gates[:, 3 * g + 2:3 * g + 3] * o_w[g * QBLOCK:(g + 1) * QBLOCK]
                        ).astype(o_ref.dtype)


def nsa_window(qkv, s_len, q_blk, kw_blk, vw_blk, gates, u):
    nb = s_len // QBLOCK
    span = (-(-(NSA_WINDOW - 1) // QBLOCK) + 1) * QBLOCK
    gw = NSA_GROUP * HEAD_DIM
    return pl.pallas_call(
        functools.partial(_win_body, scale=HEAD_DIM ** -0.5, span=span),
        grid=(NSA_KV_HEADS, nb),
        in_specs=[
            pl.BlockSpec((QBLOCK, gw), lambda h, n: (n, q_blk + h)),
            pl.BlockSpec((s_len, HEAD_DIM), lambda h, n: (0, kw_blk + h)),
            pl.BlockSpec((s_len, HEAD_DIM), lambda h, n: (0, vw_blk + h)),
            pl.BlockSpec((None, QBLOCK, 3 * NSA_GROUP), lambda h, n: (h, n, 0)),
            pl.BlockSpec((QBLOCK, gw), lambda h, n: (n, h)),
        ],
        out_specs=pl.BlockSpec((QBLOCK, gw), lambda h, n: (n, h)),
        out_shape=jax.ShapeDtypeStruct((s_len, NSA_QW), BF16),
        compiler_params=_params(("parallel", "parallel")),
        name="nsa_window",
    )(qkv, qkv, qkv, gates, u)


def _softmax_rows(s_ref, p_ref, a_ref, m_scr, l_scr, c, mask_fn):
    rows, tk = s_ref.shape
    nrep = tk // LANES
    for r in range(0, rows, FLASH_ROWS):
        rs = slice(r, r + FLASH_ROWS)
        s = s_ref[rs, :] * c
        if mask_fn is not None:
            s = mask_fn(r, s)
        m_old = m_scr[rs, :]
        m_new = jnp.maximum(m_old, jnp.max(s, axis=1, keepdims=True))
        a = jnp.exp2(m_old - m_new)
        p = jnp.exp2(s - jnp.concatenate([m_new] * nrep, axis=1))
        psum = p[:, 0:LANES]
        for j in range(1, nrep):
            psum = psum + p[:, j * LANES:(j + 1) * LANES]
        l_scr[rs, :] = a * l_scr[rs, :] + psum
        m_scr[rs, :] = m_new
        a_ref[rs, :] = a
        p_ref[rs, :] = p.astype(BF16)


def _flash_group(q_ref, tiles, s_scr, p_scr, a_scr, m_scr, l_scr, acc_scr, c):
    for g, (k_tile, _, _) in enumerate(tiles):
        s_scr[g] = lax.dot_general(q_ref[...], k_tile(), NT_DIMS, preferred_element_type=F32)
    for g, (_, v_tile, mask_fn) in enumerate(tiles):
        _softmax_rows(s_scr.at[g], p_scr.at[g], a_scr.at[g], m_scr, l_scr, c, mask_fn)
        acc_scr[...] = a_scr[g] * acc_scr[...] + jnp.dot(p_scr[g], v_tile(), preferred_element_type=F32)


def _flash_sweep(n_full, group_fn):
    def step(u, carry):
        group_fn([(FLASH_GROUP * u + g, False) for g in range(FLASH_GROUP)])
        return carry

    lax.fori_loop(0, n_full // FLASH_GROUP, step, 0)
    rem = n_full % FLASH_GROUP
    for k in range(FLASH_GROUP):
        @pl.when(rem == k)
        def _(k=k):
            group_fn([(n_full - k + g, False) for g in range(k)] + [(n_full, True)])


def _flash_init(m_scr, l_scr, acc_scr):
    m_scr[...] = jnp.full(m_scr.shape, NEG_INF, F32)
    l_scr[...] = jnp.zeros(l_scr.shape, F32)
    acc_scr[...] = jnp.zeros(acc_scr.shape, F32)


def _mla_body(q_ref, kn_ref, kr_ref, v_ref, o_ref, kt_scr, s_scr, p_scr, a_scr, m_scr, l_scr, acc_scr,
              *, scale, tq):
    qi = pl.program_id(1)
    c = scale * LOG2E
    _flash_init(m_scr, l_scr, acc_scr)

    def diag_mask(r, s):
        row = r + lax.broadcasted_iota(jnp.int32, s.shape, 0)
        col = lax.broadcasted_iota(jnp.int32, s.shape, 1)
        return jnp.where(col <= row, s, NEG_INF)

    def group(items):
        tiles = []
        for g, (t, masked) in enumerate(items):
            k0 = pl.multiple_of(t * tq, tq)
            kt_scr[g, :, 0:MLA_NOPE_DIM] = kn_ref[pl.ds(k0, tq), :]
            kt_scr[g, :, MLA_NOPE_DIM:] = kr_ref[pl.ds(k0, tq), :]
            tiles.append((lambda g=g: kt_scr[g], lambda k0=k0: v_ref[pl.ds(k0, tq), :],
                          diag_mask if masked else None))
        _flash_group(q_ref, tiles, s_scr, p_scr, a_scr, m_scr, l_scr, acc_scr, c)

    _flash_sweep(qi, group)
    o_ref[...] = (acc_scr[...] / jnp.sum(l_scr[...], axis=1, keepdims=True)).astype(o_ref.dtype)


def mla_attention(q, kv, k_rope, s_len, tq=512):
    tq = min(tq, s_len)
    qw = 2 * LANES
    return pl.pallas_call(
        functools.partial(_mla_body, scale=(MLA_NOPE_DIM + MLA_ROPE_DIM) ** -0.5, tq=tq),
        grid=(MLA_HEADS, s_len // tq),
        in_specs=[
            pl.BlockSpec((tq, qw), lambda h, i: (i, h)),
            pl.BlockSpec((s_len, MLA_NOPE_DIM), lambda h, i: (0, h)),
            pl.BlockSpec((s_len, LANES), lambda h, i: (0, 0)),
            pl.BlockSpec((s_len, MLA_V_DIM), lambda h, i: (0, MLA_HEADS + h)),
        ],
        out_specs=pl.BlockSpec((tq, MLA_V_DIM), lambda h, i: (i, h)),
        out_shape=jax.ShapeDtypeStruct((s_len, MLA_HEADS * MLA_V_DIM), BF16),
        scratch_shapes=[pltpu.VMEM((FLASH_GROUP, tq, qw), BF16), pltpu.VMEM((FLASH_GROUP, tq, tq), F32),
                        pltpu.VMEM((FLASH_GROUP, tq, tq), BF16), pltpu.VMEM((FLASH_GROUP, tq, LANES), F32),
                        pltpu.VMEM((tq, LANES), F32), pltpu.VMEM((tq, LANES), F32),
                        pltpu.VMEM((tq, MLA_V_DIM), F32)],
        compiler_params=_params(("parallel", "parallel")),
        name="mla_attention",
    )(q, kv, k_rope, kv)


_IN_SIZES = (DSW_W, DSW_W, DSW_W, NSA_QW, NSA_KVW, NSA_KVW, NSA_KVW, NSA_KVW, NSA_KVW, NSA_KVW,
             3 * NSA_HEADS, MLA_Q_LORA, MLA_KV_LORA, MLA_ROPE_DIM)
_IN_NAMES = ("a_q", "a_k", "a_v", "n_q", "n_kc", "n_vc", "n_ks", "n_vs", "n_kw", "n_vw", "n_gate",
             "m_cq", "m_ckv", "m_kr")
_IN_OFF = dict(zip(_IN_NAMES, np.concatenate([[0], np.cumsum(_IN_SIZES)[:-1]]).tolist()))
_IN_LEN = dict(zip(_IN_NAMES, _IN_SIZES))

_B_ORDER = ("a_q", "a_k", "n_q", "n_ks", "n_kw", "n_vs", "n_vw", "a_v")
_B_ROPE = 2 * DSW_W + NSA_QW + 2 * NSA_KVW
_B_WIDTH = sum(_IN_LEN[k] for k in _B_ORDER)
_B_COL = dict(zip(_B_ORDER, np.concatenate([[0], np.cumsum([_IN_LEN[k] for k in _B_ORDER])[:-1]]).tolist()))
_F_COL = {"m_cq": 0, "m_ckv": MLA_Q_LORA, "n_kc": 2048, "n_vc": 2304, "m_kr": 2560, "n_gate": 2688}
_F_WIDTH = 3072
_HALF_ROPE = MLA_ROPE_DIM // 2


def _spread_rope_cols(w):
    z = jnp.zeros((w.shape[0], _HALF_ROPE), w.dtype)
    return jnp.concatenate([w[:, :_HALF_ROPE], z, w[:, _HALF_ROPE:], z], axis=1)


def _prep_w_in(w):
    col = lambda name: w[:, _IN_OFF[name]:_IN_OFF[name] + _IN_LEN[name]]
    wb = jnp.concatenate([col(k) for k in _B_ORDER], axis=1).astype(BF16)
    used = _F_COL["n_gate"] + _IN_LEN["n_gate"]
    wf = jnp.concatenate([col("m_cq"), col("m_ckv"), col("n_kc"), col("n_vc"), _spread_rope_cols(col("m_kr")),
                          col("n_gate"), jnp.zeros((w.shape[0], _F_WIDTH - used), w.dtype)], axis=1).astype(BF16)
    return wb, wf


_B_MODES = [1] * (_B_ROPE // LANES) + [0] * ((_B_WIDTH - _B_ROPE) // LANES)
_F_MODES = [0] * 16 + [1, 1, 0, 0, 2] + [0] * 3
_Q_MODES = [0, 2] * MLA_HEADS


def _prep_w_uq(w):
    w = w.reshape(MLA_Q_LORA, MLA_HEADS, MLA_NOPE_DIM + MLA_ROPE_DIM)
    z = jnp.zeros((MLA_Q_LORA, MLA_HEADS, _HALF_ROPE), w.dtype)
    w = jnp.concatenate([w[..., :MLA_NOPE_DIM], w[..., MLA_NOPE_DIM:MLA_NOPE_DIM + _HALF_ROPE], z,
                         w[..., MLA_NOPE_DIM + _HALF_ROPE:], z], axis=-1)
    return w.reshape(MLA_Q_LORA, MLA_HEADS * 2 * LANES).astype(BF16)


def _prep_w_ukv(w):
    w = w.reshape(MLA_KV_LORA, MLA_HEADS, 2, MLA_NOPE_DIM).transpose(0, 2, 1, 3)
    return w.reshape(MLA_KV_LORA, 2 * MLA_HEADS * MLA_NOPE_DIM).astype(BF16)


def _rope_tables(positions):
    pos = positions.astype(F32)[:, None]

    def cs(dim):
        inv = ROPE_THETA ** (-jnp.arange(0, dim, 2, dtype=F32) / dim)
        ang = pos * inv
        return jnp.cos(ang), jnp.sin(ang)

    c128, s128 = cs(HEAD_DIM)
    c64, s64 = cs(MLA_ROPE_DIM)
    z = jnp.zeros_like(c64)
    ta = jnp.stack([jnp.ones((pos.shape[0], LANES), F32), jnp.concatenate([c128, c128], 1),
                    jnp.concatenate([c64, z, c64, z], 1)])
    tb = jnp.stack([jnp.zeros((pos.shape[0], LANES), F32), jnp.concatenate([-s128, s128], 1),
                    jnp.concatenate([-s64, z, s64, z], 1)])
    return ta, tb


def _mixer(x, sc1, sh1, ta, tb, w_in, cmp_pe, cmp_w1, cmp_w2, q_norm, kv_norm, w_uq, w_ukv):
    s_len = x.shape[0]
    wb, wf = _prep_w_in(w_in)
    pb = mm_fullk(x, 0, D_MODEL, wb, prologue="mod", p1=sc1, p2=sh1, epilogue="rope", modes=_B_MODES,
                  ta=ta, tb=tb, out_dtype=BF16)
    pf = mm_fullk(x, 0, D_MODEL, wf, prologue="mod", p1=sc1, p2=sh1, epilogue="rope", modes=_F_MODES,
                  ta=ta, tb=tb, out_dtype=F32)

    blk = lambda name: _B_COL[name] // DSW_W
    outs = [dilated_pattern(pb, s_len, _B_WIDTH, blk("a_q"), blk("a_k"), blk("a_v"), window, dil)
            for window, dil in DSW_PATTERNS]
    out_a = dilated_combine([o for o, _ in outs], [t for _, t in outs])

    ncp = s_len // NSA_CMP_STRIDE

    def blocks16(name):
        t = pf[:, _F_COL[name]:_F_COL[name] + NSA_KVW]
        return t.reshape(ncp, NSA_CMP_STRIDE, NSA_KV_HEADS, HEAD_DIM).transpose(2, 0, 1, 3).reshape(
            NSA_KV_HEADS, ncp, NSA_CMP_STRIDE * HEAD_DIM)

    kv_cmp = nsa_compress(jnp.stack([blocks16("n_kc"), blocks16("n_vc")]), cmp_pe,
                          cmp_w1.astype(BF16), cmp_w2.astype(BF16))
    gates = pf[:, _F_COL["n_gate"]:_F_COL["n_gate"] + 3 * NSA_HEADS]
    gates = gates.reshape(s_len, NSA_KV_HEADS, 3 * NSA_GROUP).transpose(1, 0, 2)
    gw = NSA_GROUP * HEAD_DIM
    u = nsa_cmp_sel(pb, s_len, _B_COL["n_q"] // gw, _B_COL["n_ks"] // HEAD_DIM, _B_COL["n_vs"] // HEAD_DIM,
                    kv_cmp, gates)
    out_b = nsa_window(pb, s_len, _B_COL["n_q"] // gw, _B_COL["n_kw"] // HEAD_DIM, _B_COL["n_vw"] // HEAD_DIM,
                       gates, u)

    q = mm_fullk(pf, 0, MLA_Q_LORA, _prep_w_uq(w_uq), prologue="rms", p1=q_norm, epilogue="rope",
                 modes=_Q_MODES, ta=ta, tb=tb, out_dtype=BF16)
    kv = mm_fullk(pf, _F_COL["m_ckv"] // MLA_KV_LORA, MLA_KV_LORA, _prep_w_ukv(w_ukv), prologue="rms",
                  p1=kv_norm, out_dtype=BF16)
    k_rope = pf[:, _F_COL["m_kr"]:_F_COL["m_kr"] + LANES].astype(BF16)
    out_c = mla_attention(q, kv, k_rope, s_len)
    return jnp.concatenate([out_a, out_b, out_c], axis=1)


def kernel(x, c, positions, w_ada, b_ada, w_in, nsa_cmp_pe, nsa_cmp_w1, nsa_cmp_w2, mla_q_norm, mla_kv_norm,
           mla_w_uq, mla_w_ukv, w_out, ln1_g, ln1_b, mlp_w1, mlp_w2, ln2_g, ln2_b):
    assert x.shape[0] == 1, "kernel handles batch size 1"
    xs = x[0]
    d = xs.shape[1]
    ta, tb = _rope_tables(positions[0])
    for l in range(DEPTH):
        mod = adaln(c, w_ada[l], b_ada[l])
        sh1, sc1, g1, sh2, sc2, g2 = [mod[:, i * d:(i + 1) * d] for i in range(6)]
        mixed = _mixer(xs, sc1, sh1, ta, tb, w_in[l], nsa_cmp_pe[l], nsa_cmp_w1[l], nsa_cmp_w2[l],
                       mla_q_norm[l], mla_kv_norm[l], mla_w_uq[l], mla_w_ukv[l])
        xs = mm_ln(mixed, w_out[l].astype(BF16), xs, g1, ln1_g[l], ln1_b[l])
        act = mm_fullk(xs, 0, d, mlp_w1[l].astype(BF16), prologue="mod", p1=sc2, p2=sh2, epilogue="relu2",
                       out_dtype=BF16)
        xs = mm_ln(act, mlp_w2[l].astype(BF16), xs, g2, ln2_g[l], ln2_b[l])
    return xs[None]
```

```python
import functools

import jax
import jax.numpy as jnp
import numpy as np
from jax import lax
from jax.experimental import pallas as pl
from jax.experimental.pallas import tpu as pltpu

D_MODEL = 4096
DEPTH = 2
HEAD_DIM = 128
ROPE_THETA = 10000.0
QBLOCK = 128
DSW_HEADS = 8
DSW_PATTERNS = ((128, 1), (512, 4), (2048, 16))
NSA_HEADS = 8
NSA_KV_HEADS = 2
NSA_GROUP = NSA_HEADS // NSA_KV_HEADS
NSA_CMP_LEN = 32
NSA_CMP_STRIDE = 16
NSA_CMP_HIDDEN = 256
NSA_SEL_BLOCK = 64
NSA_TOP_N = 16
NSA_WINDOW = 512
MLA_HEADS = 16
MLA_Q_LORA = 1536
MLA_KV_LORA = 512
MLA_NOPE_DIM = 128
MLA_ROPE_DIM = 64
MLA_V_DIM = 128
D_FF = 4 * D_MODEL
ALPHA = (2 * DEPTH) ** 0.25

DSW_W = DSW_HEADS * HEAD_DIM
NSA_QW = NSA_HEADS * HEAD_DIM
NSA_KVW = NSA_KV_HEADS * HEAD_DIM

LANES = 128
V7X_VMEM_LIMIT = 56 * 1024 * 1024

F32 = jnp.float32
BF16 = jnp.bfloat16
NEG_INF = float("-inf")
LOG2E = 1.4426950408889634
FLASH_ROWS = 64
FLASH_SLOTS = 2
NT_DIMS = (((1,), (1,)), ((), ()))


def _params(sem, vmem=V7X_VMEM_LIMIT):
    return pltpu.CompilerParams(dimension_semantics=sem, vmem_limit_bytes=vmem)


def _adaln_body(cb_ref, w_ref, b_ref, o_ref, *, tn):
    cb = cb_ref[...]
    for s in range(tn // LANES):
        sl = slice(s * LANES, (s + 1) * LANES)
        o_ref[:, sl] = jnp.sum(w_ref[:, sl] * cb, axis=0, keepdims=True) + b_ref[:, sl]


def adaln(c, w, b, tn=512):
    d, n = w.shape
    cb = jnp.broadcast_to(c.reshape(d, 1), (d, LANES))
    return pl.pallas_call(
        functools.partial(_adaln_body, tn=tn),
        grid=(n // tn,),
        in_specs=[
            pl.BlockSpec((d, LANES), lambda j: (0, 0)),
            pl.BlockSpec((d, tn), lambda j: (0, j)),
            pl.BlockSpec((1, tn), lambda j: (0, j)),
        ],
        out_specs=pl.BlockSpec((1, tn), lambda j: (0, j)),
        out_shape=jax.ShapeDtypeStruct((1, n), F32),
        compiler_params=_params(("arbitrary",)),
        name="adaln",
    )(cb, w, b.reshape(1, n))


def _mm_fullk_body(*refs, prologue, epilogue, nsub):
    refs = list(refs)
    mode_ref = refs.pop(0) if epilogue == "rope" else None
    x_ref, p1_ref = refs.pop(0), refs.pop(0)
    p2_ref = refs.pop(0) if prologue == "mod" else None
    w_ref = refs.pop(0)
    if epilogue == "rope":
        ta_ref, tb_ref = refs.pop(0), refs.pop(0)
    o_ref, h_scr = refs
    j = pl.program_id(1)

    @pl.when(j == 0)
    def _():
        x = x_ref[...]
        if prologue == "mod":
            h = x * (1.0 + p1_ref[...]) + p2_ref[...]
        else:
            h = x * lax.rsqrt(jnp.mean(x * x, axis=-1, keepdims=True) + 1e-6) * p1_ref[...]
        h_scr[...] = h.astype(BF16)

    acc = jnp.dot(h_scr[...], w_ref[...], preferred_element_type=F32)
    if epilogue == "relu2":
        r = jnp.maximum(acc, 0.0)
        o_ref[...] = (r * r).astype(o_ref.dtype)
    elif epilogue == "rope":
        for s in range(nsub):
            sl = slice(s * LANES, (s + 1) * LANES)
            md = mode_ref[j * nsub + s]
            sub = acc[:, sl]
            o_ref[:, sl] = (sub * ta_ref[md] + pltpu.roll(sub, LANES // 2, 1) * tb_ref[md]).astype(o_ref.dtype)
    else:
        o_ref[...] = acc.astype(o_ref.dtype)


def mm_fullk(x, x_col_block, k, w, *, prologue, p1, p2=None, epilogue="none", modes=None, ta=None, tb=None,
             out_dtype=BF16, tm=512, tn=1024):
    m = x.shape[0]
    n = w.shape[1]
    tm = min(tm, m)
    assert m % tm == 0 and n % tn == 0 and w.shape[0] == k
    nsub = tn // LANES
    rope = epilogue == "rope"
    npre = 1 if rope else 0

    def im(f):
        return (lambda i, j, *_: f(i, j))

    in_specs = [pl.BlockSpec((tm, k), im(lambda i, j: (i, x_col_block))),
                pl.BlockSpec((1, k), im(lambda i, j: (0, 0)))]
    args = [x, p1.reshape(1, k)]
    if prologue == "mod":
        in_specs.append(pl.BlockSpec((1, k), im(lambda i, j: (0, 0))))
        args.append(p2.reshape(1, k))
    in_specs.append(pl.BlockSpec((k, tn), im(lambda i, j: (0, j))))
    args.append(w)
    if rope:
        nmode = ta.shape[0]
        in_specs += [pl.BlockSpec((nmode, tm, LANES), im(lambda i, j: (0, i, 0)))] * 2
        args += [ta, tb]
    grid_spec = pltpu.PrefetchScalarGridSpec(
        num_scalar_prefetch=npre,
        grid=(m // tm, n // tn),
        in_specs=in_specs,
        out_specs=pl.BlockSpec((tm, tn), im(lambda i, j: (i, j))),
        scratch_shapes=[pltpu.VMEM((tm, k), BF16)],
    )
    fn = pl.pallas_call(
        functools.partial(_mm_fullk_body, prologue=prologue, epilogue=epilogue, nsub=nsub),
        grid_spec=grid_spec,
        out_shape=jax.ShapeDtypeStruct((m, n), out_dtype),
        compiler_params=_params(("parallel", "arbitrary")),
        name="mm_fullk_" + prologue + "_" + epilogue,
    )
    if rope:
        return fn(jnp.asarray(modes, jnp.int32), *args)
    return fn(*args)


def _mm_ln_body(lhs_ref, w_ref, x_ref, g_ref, lng_ref, lnb_ref, o_ref, acc_ref, *, nk):
    kk = pl.program_id(1)

    @pl.when(kk == 0)
    def _():
        acc_ref[...] = jnp.zeros_like(acc_ref)

    acc_ref[...] += jnp.dot(lhs_ref[...], w_ref[...], preferred_element_type=F32)

    @pl.when(kk == nk - 1)
    def _():
        z = ALPHA * x_ref[...] + (1.0 + g_ref[...]) * acc_ref[...]
        mu = jnp.mean(z, axis=-1, keepdims=True)
        zc = z - mu
        var = jnp.mean(zc * zc, axis=-1, keepdims=True)
        o_ref[...] = zc * lax.rsqrt(var + 1e-5) * lng_ref[...] + lnb_ref[...]


def mm_ln(lhs, w, x, gate, ln_g, ln_b, tm=512, tk=512):
    m, k = lhs.shape
    n = w.shape[1]
    tm = min(tm, m)
    nk = k // tk
    row = lambda i, kk: (0, 0)
    return pl.pallas_call(
        functools.partial(_mm_ln_body, nk=nk),
        grid=(m // tm, nk),
        in_specs=[
            pl.BlockSpec((tm, tk), lambda i, kk: (i, kk)),
            pl.BlockSpec((tk, n), lambda i, kk: (kk, 0)),
            pl.BlockSpec((tm, n), lambda i, kk: (i, 0), pipeline_mode=pl.Buffered(1)),
            pl.BlockSpec((1, n), row), pl.BlockSpec((1, n), row), pl.BlockSpec((1, n), row),
        ],
        out_specs=pl.BlockSpec((tm, n), lambda i, kk: (i, 0)),
        out_shape=jax.ShapeDtypeStruct((m, n), F32),
        scratch_shapes=[pltpu.VMEM((tm, n), F32)],
        compiler_params=_params(("parallel", "arbitrary")),
        name="mm_ln",
    )(lhs, w, x, gate.reshape(1, n), ln_g.reshape(1, n), ln_b.reshape(1, n))


def _dil_body(q_ref, kc_ref, kp_ref, vc_ref, vp_ref, o_ref, lse_ref, *, max_delta):
    n = pl.program_id(1)
    qi = lax.broadcasted_iota(jnp.int32, (QBLOCK, 2 * QBLOCK), 0)
    ki = lax.broadcasted_iota(jnp.int32, (QBLOCK, 2 * QBLOCK), 1)
    delta = QBLOCK + qi - ki
    valid = (delta >= 0) & (delta <= max_delta) & ((ki >= QBLOCK) | (n > 0))
    for h in range(DSW_HEADS):
        sl = slice(h * HEAD_DIM, (h + 1) * HEAD_DIM)
        k = jnp.concatenate([kp_ref[:, sl], kc_ref[:, sl]], axis=0)
        v = jnp.concatenate([vp_ref[:, sl], vc_ref[:, sl]], axis=0)
        s = lax.dot_general(q_ref[:, sl], k, NT_DIMS, preferred_element_type=F32)
        s = jnp.where(valid, s, NEG_INF)
        m = jnp.max(s, axis=-1, keepdims=True)
        p = jnp.exp2(s - m)
        l = jnp.sum(p, axis=-1, keepdims=True)
        o = jnp.dot(p.astype(BF16), v, preferred_element_type=F32)
        o_ref[:, sl] = o / l
        lse_ref[:, h:h + 1] = m + jnp.log2(l)


def dilated_pattern(qkv, s_len, width, q_blk, k_blk, v_blk, window, dil):
    l_len = s_len // dil
    nblk = l_len // QBLOCK
    max_delta = window // dil
    assert max_delta == QBLOCK
    per_row = width // DSW_W
    view = qkv.reshape(l_len, dil * width)
    cur = lambda blk: (lambda r, n: (n, r * per_row + blk))
    prev = lambda blk: (lambda r, n: (jnp.maximum(n - 1, 0), r * per_row + blk))
    bs = lambda f: pl.BlockSpec((QBLOCK, DSW_W), f)
    o, lse = pl.pallas_call(
        functools.partial(_dil_body, max_delta=max_delta),
        grid=(dil, nblk),
        in_specs=[bs(cur(q_blk)), bs(cur(k_blk)), bs(prev(k_blk)), bs(cur(v_blk)), bs(prev(v_blk))],
        out_specs=[pl.BlockSpec((QBLOCK, DSW_W), lambda r, n: (n, r)),
                   pl.BlockSpec((None, QBLOCK, DSW_HEADS), lambda r, n: (r, n, 0))],
        out_shape=[jax.ShapeDtypeStruct((l_len, dil * DSW_W), F32),
                   jax.ShapeDtypeStruct((dil, l_len, DSW_HEADS), F32)],
        compiler_params=_params(("parallel", "parallel")),
        name="dilated_d%d" % dil,
    )(view, view, view, view, view)
    o = o.reshape(s_len, DSW_W)
    lse = lse.transpose(1, 0, 2).reshape(s_len, DSW_HEADS)
    return o, lse


def _dil_combine_body(o1_ref, o2_ref, o3_ref, l1_ref, l2_ref, l3_ref, out_ref):
    ls = [l1_ref[...], l2_ref[...], l3_ref[...]]
    mx = jnp.maximum(jnp.maximum(ls[0], ls[1]), ls[2])
    ws = [jnp.exp2(t - mx) for t in ls]
    den = ws[0] + ws[1] + ws[2]
    os_ = [o1_ref, o2_ref, o3_ref]
    for h in range(DSW_HEADS):
        sl = slice(h * HEAD_DIM, (h + 1) * HEAD_DIM)
        num = sum(ws[i][:, h:h + 1] * os_[i][:, sl] for i in range(3))
        out_ref[:, sl] = (num / den[:, h:h + 1]).astype(out_ref.dtype)


def dilated_combine(os_, lses, tm=512):
    s_len = os_[0].shape[0]
    tm = min(tm, s_len)
    ob = pl.BlockSpec((tm, DSW_W), lambda i: (i, 0))
    lb = pl.BlockSpec((tm, DSW_HEADS), lambda i: (i, 0))
    return pl.pallas_call(
        _dil_combine_body,
        grid=(s_len // tm,),
        in_specs=[ob, ob, ob, lb, lb, lb],
        out_specs=ob,
        out_shape=jax.ShapeDtypeStruct((s_len, DSW_W), BF16),
        compiler_params=_params(("parallel",)),
        name="dilated_combine",
    )(*os_, *lses)


def _cmp_body(x_ref, pe_ref, w1_ref, w2_ref, o_ref, *, ncp):
    x = x_ref[...]
    half = NSA_CMP_STRIDE * HEAD_DIM
    a = jnp.dot((x + pe_ref[0:1, :]).astype(BF16), w1_ref[0:half, :], preferred_element_type=F32)
    b = jnp.dot((x + pe_ref[1:2, :]).astype(BF16), w1_ref[half:2 * half, :], preferred_element_type=F32)
    hid = a + pltpu.roll(b, ncp - 1, 0)
    act = jax.nn.gelu(hid)
    o_ref[...] = jnp.dot(act.astype(BF16), w2_ref[...], preferred_element_type=F32).astype(o_ref.dtype)


def nsa_compress(xs, pe, w1, w2):
    ncp = xs.shape[2]
    half = NSA_CMP_STRIDE * HEAD_DIM
    return pl.pallas_call(
        functools.partial(_cmp_body, ncp=ncp),
        grid=(2, NSA_KV_HEADS),
        in_specs=[
            pl.BlockSpec((None, None, ncp, half), lambda a, h: (a, h, 0, 0)),
            pl.BlockSpec((None, 2, half), lambda a, h: (a, 0, 0)),
            pl.BlockSpec((None, 2 * half, NSA_CMP_HIDDEN), lambda a, h: (a, 0, 0)),
            pl.BlockSpec((None, NSA_CMP_HIDDEN, HEAD_DIM), lambda a, h: (a, 0, 0)),
        ],
        out_specs=pl.BlockSpec((None, None, ncp, HEAD_DIM), lambda a, h: (a, h, 0, 0)),
        out_shape=jax.ShapeDtypeStruct((2, NSA_KV_HEADS, ncp, HEAD_DIM), BF16),
        compiler_params=_params(("parallel", "parallel")),
        name="nsa_compress",
    )(xs, pe.reshape(2, 2, half), w1, w2)


def _stack_heads(q):
    return jnp.concatenate([q[:, g * HEAD_DIM:(g + 1) * HEAD_DIM] for g in range(NSA_GROUP)], axis=0)


def _nsa_body(q_ref, ks_ref, vs_ref, kc_ref, vc_ref, covt_ref, e0_ref, gate_ref, u_ref,
              sel_scr, q4_scr, mk_scr, ve_scr, s_scr, p_scr, a_scr, m_scr, acc_scr,
              *, ncp, nselp, ntop, tk):
    n = pl.program_id(1)
    rows = NSA_GROUP * QBLOCK
    q4_scr[...] = _stack_heads(q_ref[...])
    q4 = q4_scr[...]

    s = lax.dot_general(q4, kc_ref[...], NT_DIMS, preferred_element_type=F32)
    qpos_r = n * QBLOCK + (lax.broadcasted_iota(jnp.int32, (rows, ncp), 0) & (QBLOCK - 1))
    cidx = lax.broadcasted_iota(jnp.int32, (rows, ncp), 1)
    s = jnp.where(cidx * NSA_CMP_STRIDE + (NSA_CMP_LEN - 1) <= qpos_r, s, NEG_INF)
    m = jnp.max(s, axis=-1, keepdims=True)
    e = jnp.exp2(s - jnp.where(m == NEG_INF, 0.0, m))
    l = jnp.sum(e, axis=-1, keepdims=True)
    p = e / jnp.where(l > 0, l, 1.0)
    o_c = jnp.dot(p.astype(BF16), vc_ref[...], preferred_element_type=F32)

    psum = p[0:QBLOCK]
    for g in range(1, NSA_GROUP):
        psum = psum + p[g * QBLOCK:(g + 1) * QBLOCK]
    p_hi = psum.astype(BF16)
    p_lo = (psum - p_hi.astype(F32)).astype(BF16)
    covt = covt_ref[...]
    imp_t = (lax.dot_general(covt, p_hi, NT_DIMS, preferred_element_type=F32)
             + lax.dot_general(covt, p_lo, NT_DIMS, preferred_element_type=F32))

    jblk = lax.broadcasted_iota(jnp.int32, (nselp, QBLOCK), 0)
    qpos_c = n * QBLOCK + lax.broadcasted_iota(jnp.int32, (nselp, QBLOCK), 1)
    cur = qpos_c // NSA_SEL_BLOCK
    forced = (jblk == 0) | (jblk == cur) | (jblk == cur - 1)
    valid = jblk * NSA_SEL_BLOCK <= qpos_c
    score0 = jnp.where(valid, jnp.where(forced, jnp.inf, imp_t), NEG_INF)

    def pick(_, carry):
        score, picked = carry
        mx = jnp.max(score, axis=0, keepdims=True)
        idx = jnp.min(jnp.where(score == mx, jblk, nselp), axis=0, keepdims=True)
        hit = jblk == idx
        return jnp.where(hit, NEG_INF, score), jnp.where(hit, 1.0, picked)

    _, picked = lax.fori_loop(0, ntop, pick, (score0, jnp.zeros((nselp, QBLOCK), F32)))
    sel_scr[...] = jnp.where(valid, picked, 0.0).T

    blocks_per_tile = tk // NSA_SEL_BLOCK
    _flash_init(m_scr, acc_scr, ve_scr)

    def k_tile(t, slot):
        shift = (nselp - t * blocks_per_tile) % nselp
        sel_t = pltpu.roll(sel_scr[...], shift, 1)[:, 0:LANES].astype(BF16)
        mk_scr[slot] = jnp.dot(sel_t, e0_ref[...], preferred_element_type=F32)
        return ks_ref[pl.ds(pl.multiple_of(t * tk, tk), tk), :]

    def v_tile(t, slot):
        ve_scr[slot, :, 0:HEAD_DIM] = vs_ref[pl.ds(pl.multiple_of(t * tk, tk), tk), :]
        return ve_scr[slot]

    def mask_for(slot, t, masked):
        def mask_fn(r, s):
            rq = r % QBLOCK
            ok = mk_scr[slot, rq:rq + FLASH_ROWS, :] > 0.5
            if masked:
                qpos = n * QBLOCK + rq + lax.broadcasted_iota(jnp.int32, s.shape, 0)
                ok = ok & (t * tk + lax.broadcasted_iota(jnp.int32, s.shape, 1) <= qpos)
            return jnp.where(ok, s, NEG_INF)
        return mask_fn

    _flash_pipeline((n * QBLOCK + QBLOCK - 1) // tk, q4_scr, k_tile, v_tile, mask_for,
                    s_scr, p_scr, a_scr, m_scr, acc_scr)
    o_s = _flash_output(acc_scr)

    gates = jax.nn.sigmoid(gate_ref[...])
    for g in range(NSA_GROUP):
        rs = slice(g * QBLOCK, (g + 1) * QBLOCK)
        u_ref[:, g * HEAD_DIM:(g + 1) * HEAD_DIM] = (gates[:, 3 * g:3 * g + 1] * o_c[rs]
                                                     + gates[:, 3 * g + 1:3 * g + 2] * o_s[rs])


def nsa_cmp_sel(qkv, s_len, q_blk, ks_blk, vs_blk, kv_cmp, gates, tk=512):
    nb = s_len // QBLOCK
    ncp = s_len // NSA_CMP_STRIDE
    nsel = s_len // NSA_SEL_BLOCK
    nselp = -(-nsel // LANES) * LANES
    ntop = min(NSA_TOP_N, nsel)
    ci = np.arange(ncp)[None, :] * NSA_CMP_STRIDE
    sj = np.arange(nselp)[:, None] * NSA_SEL_BLOCK
    cov = (ci < sj + NSA_SEL_BLOCK) & (ci + NSA_CMP_LEN > sj) & (np.arange(ncp)[None, :] < ncp - 1) & (sj < s_len)
    covt = jnp.asarray(cov.astype(np.float32), BF16)
    e0 = jnp.asarray((np.arange(tk)[None, :] // NSA_SEL_BLOCK == np.arange(LANES)[:, None]).astype(np.float32), BF16)
    gw = NSA_GROUP * HEAD_DIM
    rows = NSA_GROUP * QBLOCK
    return pl.pallas_call(
        functools.partial(_nsa_body, ncp=ncp, nselp=nselp, ntop=ntop, tk=tk),
        grid=(NSA_KV_HEADS, nb),
        in_specs=[
            pl.BlockSpec((QBLOCK, gw), lambda h, n: (n, q_blk + h)),
            pl.BlockSpec((s_len, HEAD_DIM), lambda h, n: (0, ks_blk + h)),
            pl.BlockSpec((s_len, HEAD_DIM), lambda h, n: (0, vs_blk + h)),
            pl.BlockSpec((None, None, ncp, HEAD_DIM), lambda h, n: (0, h, 0, 0)),
            pl.BlockSpec((None, None, ncp, HEAD_DIM), lambda h, n: (1, h, 0, 0)),
            pl.BlockSpec((nselp, ncp), lambda h, n: (0, 0)),
            pl.BlockSpec((LANES, tk), lambda h, n: (0, 0)),
            pl.BlockSpec((None, QBLOCK, 3 * NSA_GROUP), lambda h, n: (h, n, 0)),
        ],
        out_specs=pl.BlockSpec((QBLOCK, gw), lambda h, n: (n, h)),
        out_shape=jax.ShapeDtypeStruct((s_len, NSA_QW), F32),
        scratch_shapes=[pltpu.VMEM((QBLOCK, nselp), F32), pltpu.VMEM((rows, HEAD_DIM), BF16),
                        pltpu.VMEM((FLASH_SLOTS, QBLOCK, tk), F32), pltpu.VMEM((FLASH_SLOTS, tk, 2 * HEAD_DIM), BF16),
                        pltpu.VMEM((FLASH_SLOTS, rows, tk), F32), pltpu.VMEM((FLASH_SLOTS, rows, tk), BF16),
                        pltpu.VMEM((FLASH_SLOTS, rows, LANES), F32), pltpu.VMEM((rows, LANES), F32),
                        pltpu.VMEM((rows, 2 * HEAD_DIM), F32)],
        compiler_params=_params(("parallel", "arbitrary")),
        name="nsa_cmp_sel",
    )(qkv, qkv, qkv, kv_cmp, kv_cmp, covt, e0, gates)


def _win_body(q_ref, kw_ref, vw_ref, gate_ref, u_ref, o_ref, *, span):
    n = pl.program_id(1)
    rows = NSA_GROUP * QBLOCK
    q4 = _stack_heads(q_ref[...])
    start = pl.multiple_of(jnp.maximum(n * QBLOCK - (span - QBLOCK), 0), QBLOCK)
    s = lax.dot_general(q4, kw_ref[pl.ds(start, span), :], NT_DIMS, preferred_element_type=F32)
    qpos = n * QBLOCK + (lax.broadcasted_iota(jnp.int32, (rows, span), 0) & (QBLOCK - 1))
    delta = qpos - (start + lax.broadcasted_iota(jnp.int32, (rows, span), 1))
    s = jnp.where((delta >= 0) & (delta <= NSA_WINDOW - 1), s, NEG_INF)
    m = jnp.max(s, axis=-1, keepdims=True)
    p = jnp.exp2(s - m)
    l = jnp.sum(p, axis=-1, keepdims=True)
    o_w = jnp.dot(p.astype(BF16), vw_ref[pl.ds(start, span), :], preferred_element_type=F32) / l
    gates = jax.nn.sigmoid(gate_ref[...])
    for g in range(NSA_GROUP):
        sl = slice(g * HEAD_DIM, (g + 1) * HEAD_DIM)
        o_ref[:, sl] = (u_ref[:, sl] + gates[:, 3 * g + 2:3 * g + 3] * o_w[g * QBLOCK:(g + 1) * QBLOCK]
                        ).astype(o_ref.dtype)


def nsa_window(qkv, s_len, q_blk, kw_blk, vw_blk, gates, u):
    nb = s_len // QBLOCK
    span = (-(-(NSA_WINDOW - 1) // QBLOCK) + 1) * QBLOCK
    gw = NSA_GROUP * HEAD_DIM
    return pl.pallas_call(
        functools.partial(_win_body, span=span),
        grid=(NSA_KV_HEADS, nb),
        in_specs=[
            pl.BlockSpec((QBLOCK, gw), lambda h, n: (n, q_blk + h)),
            pl.BlockSpec((s_len, HEAD_DIM), lambda h, n: (0, kw_blk + h)),
            pl.BlockSpec((s_len, HEAD_DIM), lambda h, n: (0, vw_blk + h)),
            pl.BlockSpec((None, QBLOCK, 3 * NSA_GROUP), lambda h, n: (h, n, 0)),
            pl.BlockSpec((QBLOCK, gw), lambda h, n: (n, h)),
        ],
        out_specs=pl.BlockSpec((QBLOCK, gw), lambda h, n: (n, h)),
        out_shape=jax.ShapeDtypeStruct((s_len, NSA_QW), BF16),
        compiler_params=_params(("parallel", "parallel")),
        name="nsa_window",
    )(qkv, qkv, qkv, gates, u)


def _softmax_rows(s_ref, p_ref, a_ref, m_scr, mask_fn):
    rows, tk = s_ref.shape
    nrep = tk // LANES
    for r in range(0, rows, FLASH_ROWS):
        rs = slice(r, r + FLASH_ROWS)
        s = s_ref[rs, :]
        if mask_fn is not None:
            s = mask_fn(r, s)
        m_old = m_scr[rs, :]
        m_new = jnp.maximum(m_old, jnp.max(s, axis=1, keepdims=True))
        p_ref[rs, :] = jnp.exp2(s - jnp.concatenate([m_new] * nrep, axis=1)).astype(BF16)
        a_ref[rs, :] = jnp.exp2(m_old - m_new)
        m_scr[rs, :] = m_new


def _flash_pipeline(n_full, q_ref, k_tile, v_tile, mask_for, s_scr, p_scr, a_scr, m_scr, acc_scr):
    def scores(t, slot):
        s_scr[slot] = lax.dot_general(q_ref[...], k_tile(t, slot), NT_DIMS, preferred_element_type=F32)

    def softmax(slot, t, masked):
        _softmax_rows(s_scr.at[slot], p_scr.at[slot], a_scr.at[slot], m_scr, mask_for(slot, t, masked))

    def values(slot, t):
        a = a_scr[slot]
        acc_scr[...] = (jnp.concatenate([a, a], axis=1) * acc_scr[...]
                        + jnp.dot(p_scr[slot], v_tile(t, slot), preferred_element_type=F32))

    p_scr[1] = jnp.zeros(p_scr.shape[1:], p_scr.dtype)
    a_scr[1] = jnp.ones(a_scr.shape[1:], a_scr.dtype)
    scores(0, 0)

    def body(u, carry):
        t = 2 * u
        scores(t + 1, 1)
        softmax(0, t, False)
        values(1, jnp.maximum(t - 1, 0))
        scores(t + 2, 0)
        softmax(1, t + 1, False)
        values(0, t)
        return carry

    lax.fori_loop(0, n_full // 2, body, 0)
    prev = jnp.maximum(2 * (n_full // 2) - 1, 0)

    @pl.when(n_full % 2 == 0)
    def _():
        softmax(0, n_full, True)
        values(1, prev)
        values(0, n_full)

    @pl.when(n_full % 2 == 1)
    def _():
        scores(n_full, 1)
        softmax(0, n_full - 1, False)
        values(1, prev)
        softmax(1, n_full, True)
        values(0, n_full - 1)
        values(1, n_full)


def _flash_init(m_scr, acc_scr, ve_scr):
    dv = ve_scr.shape[2] // 2
    m_scr[...] = jnp.full(m_scr.shape, NEG_INF, F32)
    acc_scr[...] = jnp.zeros(acc_scr.shape, F32)
    ve_scr[:, :, dv:] = jnp.ones((ve_scr.shape[0], ve_scr.shape[1], dv), ve_scr.dtype)


def _flash_output(acc_scr):
    dv = acc_scr.shape[1] // 2
    return acc_scr[:, 0:dv] / acc_scr[:, dv:]


def _mla_body(q_ref, kn_ref, kr_ref, v_ref, o_ref, kt_scr, ve_scr, s_scr, p_scr, a_scr, m_scr, acc_scr, *, tq):
    qi = pl.program_id(1)
    _flash_init(m_scr, acc_scr, ve_scr)

    def diag_mask(r, s):
        row = r + lax.broadcasted_iota(jnp.int32, s.shape, 0)
        col = lax.broadcasted_iota(jnp.int32, s.shape, 1)
        return jnp.where(col <= row, s, NEG_INF)

    def k_tile(t, slot):
        k0 = pl.multiple_of(t * tq, tq)
        kt_scr[slot, :, 0:MLA_NOPE_DIM] = kn_ref[pl.ds(k0, tq), :]
        kt_scr[slot, :, MLA_NOPE_DIM:] = kr_ref[pl.ds(k0, tq), :]
        return kt_scr[slot]

    def v_tile(t, slot):
        ve_scr[slot, :, 0:MLA_V_DIM] = v_ref[pl.ds(pl.multiple_of(t * tq, tq), tq), :]
        return ve_scr[slot]

    _flash_pipeline(qi, q_ref, k_tile, v_tile, lambda slot, t, masked: diag_mask if masked else None,
                    s_scr, p_scr, a_scr, m_scr, acc_scr)
    o_ref[...] = _flash_output(acc_scr).astype(o_ref.dtype)


def mla_attention(q, kv, k_rope, s_len, tq=512):
    tq = min(tq, s_len)
    qw = 2 * LANES
    return pl.pallas_call(
        functools.partial(_mla_body, tq=tq),
        grid=(MLA_HEADS, s_len // tq),
        in_specs=[
            pl.BlockSpec((tq, qw), lambda h, i: (i, h)),
            pl.BlockSpec((s_len, MLA_NOPE_DIM), lambda h, i: (0, h)),
            pl.BlockSpec((s_len, LANES), lambda h, i: (0, 0)),
            pl.BlockSpec((s_len, MLA_V_DIM), lambda h, i: (0, MLA_HEADS + h)),
        ],
        out_specs=pl.BlockSpec((tq, MLA_V_DIM), lambda h, i: (i, h)),
        out_shape=jax.ShapeDtypeStruct((s_len, MLA_HEADS * MLA_V_DIM), BF16),
        scratch_shapes=[pltpu.VMEM((FLASH_SLOTS, tq, qw), BF16), pltpu.VMEM((FLASH_SLOTS, tq, 2 * MLA_V_DIM), BF16),
                        pltpu.VMEM((FLASH_SLOTS, tq, tq), F32), pltpu.VMEM((FLASH_SLOTS, tq, tq), BF16),
                        pltpu.VMEM((FLASH_SLOTS, tq, LANES), F32), pltpu.VMEM((tq, LANES), F32),
                        pltpu.VMEM((tq, 2 * MLA_V_DIM), F32)],
        compiler_params=_params(("parallel", "parallel")),
        name="mla_attention",
    )(q, kv, k_rope, kv)


_IN_SIZES = (DSW_W, DSW_W, DSW_W, NSA_QW, NSA_KVW, NSA_KVW, NSA_KVW, NSA_KVW, NSA_KVW, NSA_KVW,
             3 * NSA_HEADS, MLA_Q_LORA, MLA_KV_LORA, MLA_ROPE_DIM)
_IN_NAMES = ("a_q", "a_k", "a_v", "n_q", "n_kc", "n_vc", "n_ks", "n_vs", "n_kw", "n_vw", "n_gate",
             "m_cq", "m_ckv", "m_kr")
_IN_OFF = dict(zip(_IN_NAMES, np.concatenate([[0], np.cumsum(_IN_SIZES)[:-1]]).tolist()))
_IN_LEN = dict(zip(_IN_NAMES, _IN_SIZES))

_B_ORDER = ("a_q", "a_k", "n_q", "n_ks", "n_kw", "n_vs", "n_vw", "a_v")
_B_ROPE = 2 * DSW_W + NSA_QW + 2 * NSA_KVW
_B_WIDTH = sum(_IN_LEN[k] for k in _B_ORDER)
_B_COL = dict(zip(_B_ORDER, np.concatenate([[0], np.cumsum([_IN_LEN[k] for k in _B_ORDER])[:-1]]).tolist()))
_F_COL = {"m_cq": 0, "m_ckv": MLA_Q_LORA, "n_kc": 2048, "n_vc": 2304, "m_kr": 2560, "n_gate": 2688}
_F_WIDTH = 3072
_HALF_ROPE = MLA_ROPE_DIM // 2


def _spread_rope_cols(w):
    z = jnp.zeros((w.shape[0], _HALF_ROPE), w.dtype)
    return jnp.concatenate([w[:, :_HALF_ROPE], z, w[:, _HALF_ROPE:], z], axis=1)


def _prep_w_in(w):
    col = lambda name: w[:, _IN_OFF[name]:_IN_OFF[name] + _IN_LEN[name]]
    wb = jnp.concatenate([col(k) for k in _B_ORDER], axis=1).astype(BF16)
    used = _F_COL["n_gate"] + _IN_LEN["n_gate"]
    wf = jnp.concatenate([col("m_cq"), col("m_ckv"), col("n_kc"), col("n_vc"), _spread_rope_cols(col("m_kr")),
                          col("n_gate"), jnp.zeros((w.shape[0], _F_WIDTH - used), w.dtype)], axis=1).astype(BF16)
    return wb, wf


_B_MODES = ([2] * (DSW_W // LANES) + [1] * (DSW_W // LANES) + [2] * (NSA_QW // LANES)
            + [1] * (2 * NSA_KVW // LANES) + [0] * ((_B_WIDTH - _B_ROPE) // LANES))
_F_MODES = [0] * 16 + [1, 1, 0, 0, 2] + [0] * 3
_Q_MODES = [0, 1] * MLA_HEADS
C_HEAD = HEAD_DIM ** -0.5 * LOG2E
C_MLA = (MLA_NOPE_DIM + MLA_ROPE_DIM) ** -0.5 * LOG2E


def _prep_w_uq(w):
    w = w.reshape(MLA_Q_LORA, MLA_HEADS, MLA_NOPE_DIM + MLA_ROPE_DIM)
    z = jnp.zeros((MLA_Q_LORA, MLA_HEADS, _HALF_ROPE), w.dtype)
    w = jnp.concatenate([w[..., :MLA_NOPE_DIM], w[..., MLA_NOPE_DIM:MLA_NOPE_DIM + _HALF_ROPE], z,
                         w[..., MLA_NOPE_DIM + _HALF_ROPE:], z], axis=-1)
    return w.reshape(MLA_Q_LORA, MLA_HEADS * 2 * LANES).astype(BF16)


def _prep_w_ukv(w):
    w = w.reshape(MLA_KV_LORA, MLA_HEADS, 2, MLA_NOPE_DIM).transpose(0, 2, 1, 3)
    return w.reshape(MLA_KV_LORA, 2 * MLA_HEADS * MLA_NOPE_DIM).astype(BF16)


def _rope_tables(positions):
    pos = positions.astype(F32)[:, None]

    def cs(dim):
        inv = ROPE_THETA ** (-jnp.arange(0, dim, 2, dtype=F32) / dim)
        ang = pos * inv
        return jnp.cos(ang), jnp.sin(ang)

    c128, s128 = cs(HEAD_DIM)
    c64, s64 = cs(MLA_ROPE_DIM)
    z = jnp.zeros_like(c64)
    one = jnp.ones((pos.shape[0], LANES), F32)
    zero = jnp.zeros((pos.shape[0], LANES), F32)
    a128, b128 = jnp.concatenate([c128, c128], 1), jnp.concatenate([-s128, s128], 1)
    a64, b64 = jnp.concatenate([c64, z, c64, z], 1), jnp.concatenate([-s64, z, s64, z], 1)
    return {"b": (jnp.stack([one, a128, a128 * C_HEAD]), jnp.stack([zero, b128, b128 * C_HEAD])),
            "f": (jnp.stack([one, a128, a64]), jnp.stack([zero, b128, b64])),
            "q": (jnp.stack([one * C_MLA, a64 * C_MLA]), jnp.stack([zero, b64 * C_MLA]))}


def _mixer(x, sc1, sh1, tabs, w_in, cmp_pe, cmp_w1, cmp_w2, q_norm, kv_norm, w_uq, w_ukv):
    s_len = x.shape[0]
    wb, wf = _prep_w_in(w_in)
    pb = mm_fullk(x, 0, D_MODEL, wb, prologue="mod", p1=sc1, p2=sh1, epilogue="rope", modes=_B_MODES,
                  ta=tabs["b"][0], tb=tabs["b"][1], out_dtype=BF16)
    pf = mm_fullk(x, 0, D_MODEL, wf, prologue="mod", p1=sc1, p2=sh1, epilogue="rope", modes=_F_MODES,
                  ta=tabs["f"][0], tb=tabs["f"][1], out_dtype=F32)

    blk = lambda name: _B_COL[name] // DSW_W
    outs = [dilated_pattern(pb, s_len, _B_WIDTH, blk("a_q"), blk("a_k"), blk("a_v"), window, dil)
            for window, dil in DSW_PATTERNS]
    out_a = dilated_combine([o for o, _ in outs], [t for _, t in outs])

    ncp = s_len // NSA_CMP_STRIDE

    def blocks16(name):
        t = pf[:, _F_COL[name]:_F_COL[name] + NSA_KVW]
        return t.reshape(ncp, NSA_CMP_STRIDE, NSA_KV_HEADS, HEAD_DIM).transpose(2, 0, 1, 3).reshape(
            NSA_KV_HEADS, ncp, NSA_CMP_STRIDE * HEAD_DIM)

    kv_cmp = nsa_compress(jnp.stack([blocks16("n_kc"), blocks16("n_vc")]), cmp_pe,
                          cmp_w1.astype(BF16), cmp_w2.astype(BF16))
    gates = pf[:, _F_COL["n_gate"]:_F_COL["n_gate"] + 3 * NSA_HEADS]
    gates = gates.reshape(s_len, NSA_KV_HEADS, 3 * NSA_GROUP).transpose(1, 0, 2)
    gw = NSA_GROUP * HEAD_DIM
    u = nsa_cmp_sel(pb, s_len, _B_COL["n_q"] // gw, _B_COL["n_ks"] // HEAD_DIM, _B_COL["n_vs"] // HEAD_DIM,
                    kv_cmp, gates)
    out_b = nsa_window(pb, s_len, _B_COL["n_q"] // gw, _B_COL["n_kw"] // HEAD_DIM, _B_COL["n_vw"] // HEAD_DIM,
                       gates, u)

    q = mm_fullk(pf, 0, MLA_Q_LORA, _prep_w_uq(w_uq), prologue="rms", p1=q_norm, epilogue="rope",
                 modes=_Q_MODES, ta=tabs["q"][0], tb=tabs["q"][1], out_dtype=BF16)
    kv = mm_fullk(pf, _F_COL["m_ckv"] // MLA_KV_LORA, MLA_KV_LORA, _prep_w_ukv(w_ukv), prologue="rms",
                  p1=kv_norm, out_dtype=BF16)
    k_rope = pf[:, _F_COL["m_kr"]:_F_COL["m_kr"] + LANES].astype(BF16)
    out_c = mla_attention(q, kv, k_rope, s_len)
    return jnp.concatenate([out_a, out_b, out_c], axis=1)


def kernel(x, c, positions, w_ada, b_ada, w_in, nsa_cmp_pe, nsa_cmp_w1, nsa_cmp_w2, mla_q_norm, mla_kv_norm,
           mla_w_uq, mla_w_ukv, w_out, ln1_g, ln1_b, mlp_w1, mlp_w2, ln2_g, ln2_b):
    assert x.shape[0] == 1, "kernel handles batch size 1"
    xs = x[0]
    d = xs.shape[1]
    tabs = _rope_tables(positions[0])
    for l in range(DEPTH):
        mod = adaln(c, w_ada[l], b_ada[l])
        sh1, sc1, g1, sh2, sc2, g2 = [mod[:, i * d:(i + 1) * d] for i in range(6)]
        mixed = _mixer(xs, sc1, sh1, tabs, w_in[l], nsa_cmp_pe[l], nsa_cmp_w1[l], nsa_cmp_w2[l],
                       mla_q_norm[l], mla_kv_norm[l], mla_w_uq[l], mla_w_ukv[l])
        xs = mm_ln(mixed, w_out[l].astype(BF16), xs, g1, ln1_g[l], ln1_b[l])
        act = mm_fullk(xs, 0, d, mlp_w1[l].astype(BF16), prologue="mod", p1=sc2, p2=sh2, epilogue="relu2",
                       out_dtype=BF16)
        xs = mm_ln(act, mlp_w2[l].astype(BF16), xs, g2, ln2_g[l], ln2_b[l])
    return xs[None]
```

```python
import functools

import jax
import jax.numpy as jnp
import numpy as np
from jax import lax
from jax.experimental import pallas as pl
from jax.experimental.pallas import tpu as pltpu

D_MODEL = 4096
DEPTH = 2
HEAD_DIM = 128
ROPE_THETA = 10000.0
QBLOCK = 128
DSW_HEADS = 8
DSW_PATTERNS = ((128, 1), (512, 4), (2048, 16))
NSA_HEADS = 8
NSA_KV_HEADS = 2
NSA_GROUP = NSA_HEADS // NSA_KV_HEADS
NSA_CMP_LEN = 32
NSA_CMP_STRIDE = 16
NSA_CMP_HIDDEN = 256
NSA_SEL_BLOCK = 64
NSA_TOP_N = 16
NSA_WINDOW = 512
MLA_HEADS = 16
MLA_Q_LORA = 1536
MLA_KV_LORA = 512
MLA_NOPE_DIM = 128
MLA_ROPE_DIM = 64
MLA_V_DIM = 128
D_FF = 4 * D_MODEL
ALPHA = (2 * DEPTH) ** 0.25

DSW_W = DSW_HEADS * HEAD_DIM
NSA_QW = NSA_HEADS * HEAD_DIM
NSA_KVW = NSA_KV_HEADS * HEAD_DIM

LANES = 128
V7X_VMEM_LIMIT = 56 * 1024 * 1024

F32 = jnp.float32
BF16 = jnp.bfloat16
NEG_INF = float("-inf")
LOG2E = 1.4426950408889634
FLASH_ROWS = 64
FLASH_SLOTS = 2
NT_DIMS = (((1,), (1,)), ((), ()))


def _params(sem, vmem=V7X_VMEM_LIMIT):
    return pltpu.CompilerParams(dimension_semantics=sem, vmem_limit_bytes=vmem)


def _adaln_body(cb_ref, w_ref, b_ref, o_ref, *, tn):
    cb = cb_ref[...]
    for s in range(tn // LANES):
        sl = slice(s * LANES, (s + 1) * LANES)
        o_ref[:, sl] = jnp.sum(w_ref[:, sl] * cb, axis=0, keepdims=True) + b_ref[:, sl]


def adaln(c, w, b, tn=512):
    d, n = w.shape
    cb = jnp.broadcast_to(c.reshape(d, 1), (d, LANES))
    return pl.pallas_call(
        functools.partial(_adaln_body, tn=tn),
        grid=(n // tn,),
        in_specs=[
            pl.BlockSpec((d, LANES), lambda j: (0, 0)),
            pl.BlockSpec((d, tn), lambda j: (0, j)),
            pl.BlockSpec((1, tn), lambda j: (0, j)),
        ],
        out_specs=pl.BlockSpec((1, tn), lambda j: (0, j)),
        out_shape=jax.ShapeDtypeStruct((1, n), F32),
        compiler_params=_params(("arbitrary",)),
        name="adaln",
    )(cb, w, b.reshape(1, n))


def _mm_fullk_body(*refs, prologue, epilogue, nsub):
    refs = list(refs)
    mode_ref = refs.pop(0) if epilogue == "rope" else None
    x_ref, p1_ref = refs.pop(0), refs.pop(0)
    p2_ref = refs.pop(0) if prologue == "mod" else None
    w_ref = refs.pop(0)
    if epilogue == "rope":
        ta_ref, tb_ref = refs.pop(0), refs.pop(0)
    o_ref, h_scr = refs
    j = pl.program_id(1)

    @pl.when(j == 0)
    def _():
        x = x_ref[...]
        if prologue == "mod":
            h = x * (1.0 + p1_ref[...]) + p2_ref[...]
        else:
            h = x * lax.rsqrt(jnp.mean(x * x, axis=-1, keepdims=True) + 1e-6) * p1_ref[...]
        h_scr[...] = h.astype(BF16)

    acc = jnp.dot(h_scr[...], w_ref[...], preferred_element_type=F32)
    if epilogue == "relu2":
        r = jnp.maximum(acc, 0.0)
        o_ref[...] = (r * r).astype(o_ref.dtype)
    elif epilogue == "rope":
        for s in range(nsub):
            sl = slice(s * LANES, (s + 1) * LANES)
            md = mode_ref[j * nsub + s]
            sub = acc[:, sl]
            o_ref[:, sl] = (sub * ta_ref[md] + pltpu.roll(sub, LANES // 2, 1) * tb_ref[md]).astype(o_ref.dtype)
    else:
        o_ref[...] = acc.astype(o_ref.dtype)


def mm_fullk(x, x_col_block, k, w, *, prologue, p1, p2=None, epilogue="none", modes=None, ta=None, tb=None,
             out_dtype=BF16, tm=512, tn=1024):
    m = x.shape[0]
    n = w.shape[1]
    tm = min(tm, m)
    assert m % tm == 0 and n % tn == 0 and w.shape[0] == k
    nsub = tn // LANES
    rope = epilogue == "rope"
    npre = 1 if rope else 0

    def im(f):
        return (lambda i, j, *_: f(i, j))

    in_specs = [pl.BlockSpec((tm, k), im(lambda i, j: (i, x_col_block))),
                pl.BlockSpec((1, k), im(lambda i, j: (0, 0)))]
    args = [x, p1.reshape(1, k)]
    if prologue == "mod":
        in_specs.append(pl.BlockSpec((1, k), im(lambda i, j: (0, 0))))
        args.append(p2.reshape(1, k))
    in_specs.append(pl.BlockSpec((k, tn), im(lambda i, j: (0, j))))
    args.append(w)
    if rope:
        nmode = ta.shape[0]
        in_specs += [pl.BlockSpec((nmode, tm, LANES), im(lambda i, j: (0, i, 0)))] * 2
        args += [ta, tb]
    grid_spec = pltpu.PrefetchScalarGridSpec(
        num_scalar_prefetch=npre,
        grid=(m // tm, n // tn),
        in_specs=in_specs,
        out_specs=pl.BlockSpec((tm, tn), im(lambda i, j: (i, j))),
        scratch_shapes=[pltpu.VMEM((tm, k), BF16)],
    )
    fn = pl.pallas_call(
        functools.partial(_mm_fullk_body, prologue=prologue, epilogue=epilogue, nsub=nsub),
        grid_spec=grid_spec,
        out_shape=jax.ShapeDtypeStruct((m, n), out_dtype),
        compiler_params=_params(("parallel", "arbitrary")),
        name="mm_fullk_" + prologue + "_" + epilogue,
    )
    if rope:
        return fn(jnp.asarray(modes, jnp.int32), *args)
    return fn(*args)


def _mm_ln_body(lhs_ref, w_ref, x_ref, g_ref, lng_ref, lnb_ref, o_ref, acc_ref, *, nk):
    kk = pl.program_id(1)

    @pl.when(kk == 0)
    def _():
        acc_ref[...] = jnp.zeros_like(acc_ref)

    acc_ref[...] += jnp.dot(lhs_ref[...], w_ref[...], preferred_element_type=F32)

    @pl.when(kk == nk - 1)
    def _():
        z = ALPHA * x_ref[...] + (1.0 + g_ref[...]) * acc_ref[...]
        mu = jnp.mean(z, axis=-1, keepdims=True)
        zc = z - mu
        var = jnp.mean(zc * zc, axis=-1, keepdims=True)
        o_ref[...] = zc * lax.rsqrt(var + 1e-5) * lng_ref[...] + lnb_ref[...]


def mm_ln(lhs, w, x, gate, ln_g, ln_b, tm=512, tk=512):
    m, k = lhs.shape
    n = w.shape[1]
    tm = min(tm, m)
    nk = k // tk
    row = lambda i, kk: (0, 0)
    return pl.pallas_call(
        functools.partial(_mm_ln_body, nk=nk),
        grid=(m // tm, nk),
        in_specs=[
            pl.BlockSpec((tm, tk), lambda i, kk: (i, kk)),
            pl.BlockSpec((tk, n), lambda i, kk: (kk, 0)),
            pl.BlockSpec((tm, n), lambda i, kk: (i, 0), pipeline_mode=pl.Buffered(1)),
            pl.BlockSpec((1, n), row), pl.BlockSpec((1, n), row), pl.BlockSpec((1, n), row),
        ],
        out_specs=pl.BlockSpec((tm, n), lambda i, kk: (i, 0)),
        out_shape=jax.ShapeDtypeStruct((m, n), F32),
        scratch_shapes=[pltpu.VMEM((tm, n), F32)],
        compiler_params=_params(("parallel", "arbitrary")),
        name="mm_ln",
    )(lhs, w, x, gate.reshape(1, n), ln_g.reshape(1, n), ln_b.reshape(1, n))


def _dil_body(q_ref, kc_ref, kp_ref, vc_ref, vp_ref, o_ref, lse_ref, *, max_delta):
    n = pl.program_id(1)
    qi = lax.broadcasted_iota(jnp.int32, (QBLOCK, 2 * QBLOCK), 0)
    ki = lax.broadcasted_iota(jnp.int32, (QBLOCK, 2 * QBLOCK), 1)
    delta = QBLOCK + qi - ki
    valid = (delta >= 0) & (delta <= max_delta) & ((ki >= QBLOCK) | (n > 0))
    for h in range(DSW_HEADS):
        sl = slice(h * HEAD_DIM, (h + 1) * HEAD_DIM)
        k = jnp.concatenate([kp_ref[:, sl], kc_ref[:, sl]], axis=0)
        v = jnp.concatenate([vp_ref[:, sl], vc_ref[:, sl]], axis=0)
        s = lax.dot_general(q_ref[:, sl], k, NT_DIMS, preferred_element_type=F32)
        s = jnp.where(valid, s, NEG_INF)
        m = jnp.max(s, axis=-1, keepdims=True)
        p = jnp.exp2(s - m)
        l = jnp.sum(p, axis=-1, keepdims=True)
        o = jnp.dot(p.astype(BF16), v, preferred_element_type=F32)
        o_ref[:, sl] = o / l
        lse_ref[:, h:h + 1] = m + jnp.log2(l)


def dilated_pattern(qkv, s_len, width, q_blk, k_blk, v_blk, window, dil):
    l_len = s_len // dil
    nblk = l_len // QBLOCK
    max_delta = window // dil
    assert max_delta == QBLOCK
    per_row = width // DSW_W
    view = qkv.reshape(l_len, dil * width)
    cur = lambda blk: (lambda r, n: (n, r * per_row + blk))
    prev = lambda blk: (lambda r, n: (jnp.maximum(n - 1, 0), r * per_row + blk))
    bs = lambda f: pl.BlockSpec((QBLOCK, DSW_W), f)
    o, lse = pl.pallas_call(
        functools.partial(_dil_body, max_delta=max_delta),
        grid=(dil, nblk),
        in_specs=[bs(cur(q_blk)), bs(cur(k_blk)), bs(prev(k_blk)), bs(cur(v_blk)), bs(prev(v_blk))],
        out_specs=[pl.BlockSpec((QBLOCK, DSW_W), lambda r, n: (n, r)),
                   pl.BlockSpec((None, QBLOCK, DSW_HEADS), lambda r, n: (r, n, 0))],
        out_shape=[jax.ShapeDtypeStruct((l_len, dil * DSW_W), F32),
                   jax.ShapeDtypeStruct((dil, l_len, DSW_HEADS), F32)],
        compiler_params=_params(("parallel", "parallel")),
        name="dilated_d%d" % dil,
    )(view, view, view, view, view)
    o = o.reshape(s_len, DSW_W)
    lse = lse.transpose(1, 0, 2).reshape(s_len, DSW_HEADS)
    return o, lse


def _dil_combine_body(o1_ref, o2_ref, o3_ref, l1_ref, l2_ref, l3_ref, out_ref):
    ls = [l1_ref[...], l2_ref[...], l3_ref[...]]
    mx = jnp.maximum(jnp.maximum(ls[0], ls[1]), ls[2])
    ws = [jnp.exp2(t - mx) for t in ls]
    den = ws[0] + ws[1] + ws[2]
    os_ = [o1_ref, o2_ref, o3_ref]
    for h in range(DSW_HEADS):
        sl = slice(h * HEAD_DIM, (h + 1) * HEAD_DIM)
        num = sum(ws[i][:, h:h + 1] * os_[i][:, sl] for i in range(3))
        out_ref[:, sl] = (num / den[:, h:h + 1]).astype(out_ref.dtype)


def dilated_combine(os_, lses, tm=512):
    s_len = os_[0].shape[0]
    tm = min(tm, s_len)
    ob = pl.BlockSpec((tm, DSW_W), lambda i: (i, 0))
    lb = pl.BlockSpec((tm, DSW_HEADS), lambda i: (i, 0))
    return pl.pallas_call(
        _dil_combine_body,
        grid=(s_len // tm,),
        in_specs=[ob, ob, ob, lb, lb, lb],
        out_specs=ob,
        out_shape=jax.ShapeDtypeStruct((s_len, DSW_W), BF16),
        compiler_params=_params(("parallel",)),
        name="dilated_combine",
    )(*os_, *lses)


def _cmp_body(x_ref, pe_ref, w1_ref, w2_ref, o_ref, *, ncp):
    x = x_ref[...]
    half = NSA_CMP_STRIDE * HEAD_DIM
    a = jnp.dot((x + pe_ref[0:1, :]).astype(BF16), w1_ref[0:half, :], preferred_element_type=F32)
    b = jnp.dot((x + pe_ref[1:2, :]).astype(BF16), w1_ref[half:2 * half, :], preferred_element_type=F32)
    hid = a + pltpu.roll(b, ncp - 1, 0)
    act = jax.nn.gelu(hid)
    o_ref[...] = jnp.dot(act.astype(BF16), w2_ref[...], preferred_element_type=F32).astype(o_ref.dtype)


def nsa_compress(xs, pe, w1, w2):
    ncp = xs.shape[2]
    half = NSA_CMP_STRIDE * HEAD_DIM
    return pl.pallas_call(
        functools.partial(_cmp_body, ncp=ncp),
        grid=(2, NSA_KV_HEADS),
        in_specs=[
            pl.BlockSpec((None, None, ncp, half), lambda a, h: (a, h, 0, 0)),
            pl.BlockSpec((None, 2, half), lambda a, h: (a, 0, 0)),
            pl.BlockSpec((None, 2 * half, NSA_CMP_HIDDEN), lambda a, h: (a, 0, 0)),
            pl.BlockSpec((None, NSA_CMP_HIDDEN, HEAD_DIM), lambda a, h: (a, 0, 0)),
        ],
        out_specs=pl.BlockSpec((None, None, ncp, HEAD_DIM), lambda a, h: (a, h, 0, 0)),
        out_shape=jax.ShapeDtypeStruct((2, NSA_KV_HEADS, ncp, HEAD_DIM), BF16),
        compiler_params=_params(("parallel", "parallel")),
        name="nsa_compress",
    )(xs, pe.reshape(2, 2, half), w1, w2)


def _stack_heads(q):
    return jnp.concatenate([q[:, g * HEAD_DIM:(g + 1) * HEAD_DIM] for g in range(NSA_GROUP)], axis=0)


def _nsa_body(q_ref, kst_ref, vs_ref, kc_ref, vc_ref, covt_ref, e0_ref, gate_ref, u_ref,
              sel_scr, q4_scr, mk_scr, ve_scr, s_scr, p_scr, a_scr, m_scr, acc_scr,
              *, ncp, nselp, ntop, tk):
    n = pl.program_id(1)
    rows = NSA_GROUP * QBLOCK
    q4_scr[...] = _stack_heads(q_ref[...])
    q4 = q4_scr[...]

    s = lax.dot_general(q4, kc_ref[...], NT_DIMS, preferred_element_type=F32)
    qpos_r = n * QBLOCK + (lax.broadcasted_iota(jnp.int32, (rows, ncp), 0) & (QBLOCK - 1))
    cidx = lax.broadcasted_iota(jnp.int32, (rows, ncp), 1)
    s = jnp.where(cidx * NSA_CMP_STRIDE + (NSA_CMP_LEN - 1) <= qpos_r, s, NEG_INF)
    m = jnp.max(s, axis=-1, keepdims=True)
    e = jnp.exp2(s - jnp.where(m == NEG_INF, 0.0, m))
    l = jnp.sum(e, axis=-1, keepdims=True)
    p = e / jnp.where(l > 0, l, 1.0)
    o_c = jnp.dot(p.astype(BF16), vc_ref[...], preferred_element_type=F32)

    psum = p[0:QBLOCK]
    for g in range(1, NSA_GROUP):
        psum = psum + p[g * QBLOCK:(g + 1) * QBLOCK]
    p_hi = psum.astype(BF16)
    p_lo = (psum - p_hi.astype(F32)).astype(BF16)
    covt = covt_ref[...]
    imp_t = (lax.dot_general(covt, p_hi, NT_DIMS, preferred_element_type=F32)
             + lax.dot_general(covt, p_lo, NT_DIMS, preferred_element_type=F32))

    jblk = lax.broadcasted_iota(jnp.int32, (nselp, QBLOCK), 0)
    qpos_c = n * QBLOCK + lax.broadcasted_iota(jnp.int32, (nselp, QBLOCK), 1)
    cur = qpos_c // NSA_SEL_BLOCK
    forced = (jblk == 0) | (jblk == cur) | (jblk == cur - 1)
    valid = jblk * NSA_SEL_BLOCK <= qpos_c
    score0 = jnp.where(valid, jnp.where(forced, jnp.inf, imp_t), NEG_INF)

    def pick(_, carry):
        score, picked = carry
        mx = jnp.max(score, axis=0, keepdims=True)
        idx = jnp.min(jnp.where(score == mx, jblk, nselp), axis=0, keepdims=True)
        hit = jblk == idx
        return jnp.where(hit, NEG_INF, score), jnp.where(hit, 1.0, picked)

    _, picked = lax.fori_loop(0, ntop, pick, (score0, jnp.zeros((nselp, QBLOCK), F32)))
    sel_scr[...] = jnp.where(valid, picked, 0.0).T

    blocks_per_tile = tk // NSA_SEL_BLOCK
    _flash_init(m_scr, acc_scr, ve_scr)

    def k_tile(t, slot):
        shift = (nselp - t * blocks_per_tile) % nselp
        sel_t = pltpu.roll(sel_scr[...], shift, 1)[:, 0:LANES].astype(BF16)
        mk_scr[slot] = jnp.dot(sel_t, e0_ref[...], preferred_element_type=F32)
        return kst_ref[:, pl.ds(pl.multiple_of(t * tk, tk), tk)]

    def v_tile(t, slot):
        ve_scr[slot, :, 0:HEAD_DIM] = vs_ref[pl.ds(pl.multiple_of(t * tk, tk), tk), :]
        return ve_scr[slot]

    def mask_for(slot, t, masked):
        def mask_fn(r, s):
            rq = r % QBLOCK
            ok = mk_scr[slot, rq:rq + FLASH_ROWS, :] > 0.5
            if masked:
                qpos = n * QBLOCK + rq + lax.broadcasted_iota(jnp.int32, s.shape, 0)
                ok = ok & (t * tk + lax.broadcasted_iota(jnp.int32, s.shape, 1) <= qpos)
            return jnp.where(ok, s, NEG_INF)
        return mask_fn

    _flash_pipeline((n * QBLOCK + QBLOCK - 1) // tk, q4_scr, k_tile, v_tile, mask_for,
                    s_scr, p_scr, a_scr, m_scr, acc_scr)
    o_s = _flash_output(acc_scr)

    gates = jax.nn.sigmoid(gate_ref[...])
    for g in range(NSA_GROUP):
        rs = slice(g * QBLOCK, (g + 1) * QBLOCK)
        u_ref[:, g * HEAD_DIM:(g + 1) * HEAD_DIM] = (gates[:, 3 * g:3 * g + 1] * o_c[rs]
                                                     + gates[:, 3 * g + 1:3 * g + 2] * o_s[rs])


def nsa_cmp_sel(qkv, ks_t, s_len, q_blk, vs_blk, kv_cmp, gates, tk=512):
    nb = s_len // QBLOCK
    ncp = s_len // NSA_CMP_STRIDE
    nsel = s_len // NSA_SEL_BLOCK
    nselp = -(-nsel // LANES) * LANES
    ntop = min(NSA_TOP_N, nsel)
    ci = np.arange(ncp)[None, :] * NSA_CMP_STRIDE
    sj = np.arange(nselp)[:, None] * NSA_SEL_BLOCK
    cov = (ci < sj + NSA_SEL_BLOCK) & (ci + NSA_CMP_LEN > sj) & (np.arange(ncp)[None, :] < ncp - 1) & (sj < s_len)
    covt = jnp.asarray(cov.astype(np.float32), BF16)
    e0 = jnp.asarray((np.arange(tk)[None, :] // NSA_SEL_BLOCK == np.arange(LANES)[:, None]).astype(np.float32), BF16)
    gw = NSA_GROUP * HEAD_DIM
    rows = NSA_GROUP * QBLOCK
    return pl.pallas_call(
        functools.partial(_nsa_body, ncp=ncp, nselp=nselp, ntop=ntop, tk=tk),
        grid=(NSA_KV_HEADS, nb),
        in_specs=[
            pl.BlockSpec((QBLOCK, gw), lambda h, n: (n, q_blk + h)),
            pl.BlockSpec((HEAD_DIM, s_len), lambda h, n: (h, 0)),
            pl.BlockSpec((s_len, HEAD_DIM), lambda h, n: (0, vs_blk + h)),
            pl.BlockSpec((None, None, ncp, HEAD_DIM), lambda h, n: (0, h, 0, 0)),
            pl.BlockSpec((None, None, ncp, HEAD_DIM), lambda h, n: (1, h, 0, 0)),
            pl.BlockSpec((nselp, ncp), lambda h, n: (0, 0)),
            pl.BlockSpec((LANES, tk), lambda h, n: (0, 0)),
            pl.BlockSpec((None, QBLOCK, 3 * NSA_GROUP), lambda h, n: (h, n, 0)),
        ],
        out_specs=pl.BlockSpec((QBLOCK, gw), lambda h, n: (n, h)),
        out_shape=jax.ShapeDtypeStruct((s_len, NSA_QW), F32),
        scratch_shapes=[pltpu.VMEM((QBLOCK, nselp), F32), pltpu.VMEM((rows, HEAD_DIM), BF16),
                        pltpu.VMEM((FLASH_SLOTS, QBLOCK, tk), F32), pltpu.VMEM((FLASH_SLOTS, tk, 2 * HEAD_DIM), BF16),
                        pltpu.VMEM((FLASH_SLOTS, rows, tk), F32), pltpu.VMEM((FLASH_SLOTS, rows, tk), BF16),
                        pltpu.VMEM((FLASH_SLOTS, rows, LANES), F32), pltpu.VMEM((rows, LANES), F32),
                        pltpu.VMEM((rows, 2 * HEAD_DIM), F32)],
        compiler_params=_params(("parallel", "arbitrary")),
        name="nsa_cmp_sel",
    )(qkv, ks_t, qkv, kv_cmp, kv_cmp, covt, e0, gates)


def _win_body(q_ref, kw_ref, vw_ref, gate_ref, u_ref, o_ref, *, span):
    n = pl.program_id(1)
    rows = NSA_GROUP * QBLOCK
    q4 = _stack_heads(q_ref[...])
    start = pl.multiple_of(jnp.maximum(n * QBLOCK - (span - QBLOCK), 0), QBLOCK)
    s = lax.dot_general(q4, kw_ref[pl.ds(start, span), :], NT_DIMS, preferred_element_type=F32)
    qpos = n * QBLOCK + (lax.broadcasted_iota(jnp.int32, (rows, span), 0) & (QBLOCK - 1))
    delta = qpos - (start + lax.broadcasted_iota(jnp.int32, (rows, span), 1))
    s = jnp.where((delta >= 0) & (delta <= NSA_WINDOW - 1), s, NEG_INF)
    m = jnp.max(s, axis=-1, keepdims=True)
    p = jnp.exp2(s - m)
    l = jnp.sum(p, axis=-1, keepdims=True)
    o_w = jnp.dot(p.astype(BF16), vw_ref[pl.ds(start, span), :], preferred_element_type=F32) / l
    gates = jax.nn.sigmoid(gate_ref[...])
    for g in range(NSA_GROUP):
        sl = slice(g * HEAD_DIM, (g + 1) * HEAD_DIM)
        o_ref[:, sl] = (u_ref[:, sl] + gates[:, 3 * g + 2:3 * g + 3] * o_w[g * QBLOCK:(g + 1) * QBLOCK]
                        ).astype(o_ref.dtype)


def nsa_window(qkv, s_len, q_blk, kw_blk, vw_blk, gates, u):
    nb = s_len // QBLOCK
    span = (-(-(NSA_WINDOW - 1) // QBLOCK) + 1) * QBLOCK
    gw = NSA_GROUP * HEAD_DIM
    return pl.pallas_call(
        functools.partial(_win_body, span=span),
        grid=(NSA_KV_HEADS, nb),
        in_specs=[
            pl.BlockSpec((QBLOCK, gw), lambda h, n: (n, q_blk + h)),
            pl.BlockSpec((s_len, HEAD_DIM), lambda h, n: (0, kw_blk + h)),
            pl.BlockSpec((s_len, HEAD_DIM), lambda h, n: (0, vw_blk + h)),
            pl.BlockSpec((None, QBLOCK, 3 * NSA_GROUP), lambda h, n: (h, n, 0)),
            pl.BlockSpec((QBLOCK, gw), lambda h, n: (n, h)),
        ],
        out_specs=pl.BlockSpec((QBLOCK, gw), lambda h, n: (n, h)),
        out_shape=jax.ShapeDtypeStruct((s_len, NSA_QW), BF16),
        compiler_params=_params(("parallel", "parallel")),
        name="nsa_window",
    )(qkv, qkv, qkv, gates, u)


def _softmax_rows(s_ref, p_ref, a_ref, m_scr, mask_fn):
    rows, tk = s_ref.shape
    nrep = tk // LANES
    for r in range(0, rows, FLASH_ROWS):
        rs = slice(r, r + FLASH_ROWS)
        s = s_ref[rs, :]
        if mask_fn is not None:
            s = mask_fn(r, s)
        m_old = m_scr[rs, :]
        m_new = jnp.maximum(m_old, jnp.max(s, axis=1, keepdims=True))
        p_ref[rs, :] = jnp.exp2(s - jnp.concatenate([m_new] * nrep, axis=1)).astype(BF16)
        a_ref[rs, :] = jnp.exp2(m_old - m_new)
        m_scr[rs, :] = m_new


def _flash_pipeline(n_full, q_ref, k_tile, v_tile, mask_for, s_scr, p_scr, a_scr, m_scr, acc_scr):
    def scores(t, slot):
        s_scr[slot] = jnp.dot(q_ref[...], k_tile(t, slot), preferred_element_type=F32)

    def softmax(slot, t, masked):
        _softmax_rows(s_scr.at[slot], p_scr.at[slot], a_scr.at[slot], m_scr, mask_for(slot, t, masked))

    def values(slot, t):
        a = a_scr[slot]
        acc_scr[...] = (jnp.concatenate([a, a], axis=1) * acc_scr[...]
                        + jnp.dot(p_scr[slot], v_tile(t, slot), preferred_element_type=F32))

    p_scr[1] = jnp.zeros(p_scr.shape[1:], p_scr.dtype)
    a_scr[1] = jnp.ones(a_scr.shape[1:], a_scr.dtype)
    scores(0, 0)

    def body(u, carry):
        t = 2 * u
        scores(t + 1, 1)
        softmax(0, t, False)
        values(1, jnp.maximum(t - 1, 0))
        scores(t + 2, 0)
        softmax(1, t + 1, False)
        values(0, t)
        return carry

    lax.fori_loop(0, n_full // 2, body, 0)
    prev = jnp.maximum(2 * (n_full // 2) - 1, 0)

    @pl.when(n_full % 2 == 0)
    def _():
        softmax(0, n_full, True)
        values(1, prev)
        values(0, n_full)

    @pl.when(n_full % 2 == 1)
    def _():
        scores(n_full, 1)
        softmax(0, n_full - 1, False)
        values(1, prev)
        softmax(1, n_full, True)
        values(0, n_full - 1)
        values(1, n_full)


def _flash_init(m_scr, acc_scr, ve_scr):
    dv = ve_scr.shape[2] // 2
    m_scr[...] = jnp.full(m_scr.shape, NEG_INF, F32)
    acc_scr[...] = jnp.zeros(acc_scr.shape, F32)
    ve_scr[:, :, dv:] = jnp.ones((ve_scr.shape[0], ve_scr.shape[1], dv), ve_scr.dtype)


def _flash_output(acc_scr):
    dv = acc_scr.shape[1] // 2
    return acc_scr[:, 0:dv] / acc_scr[:, dv:]


def _mla_body(q_ref, knt_ref, krt_ref, v_ref, o_ref, kt_scr, ve_scr, s_scr, p_scr, a_scr, m_scr, acc_scr, *, tq):
    qi = pl.program_id(1)
    _flash_init(m_scr, acc_scr, ve_scr)

    def diag_mask(r, s):
        row = r + lax.broadcasted_iota(jnp.int32, s.shape, 0)
        col = lax.broadcasted_iota(jnp.int32, s.shape, 1)
        return jnp.where(col <= row, s, NEG_INF)

    def k_tile(t, slot):
        k0 = pl.multiple_of(t * tq, tq)
        kt_scr[slot, 0:MLA_NOPE_DIM, :] = knt_ref[:, pl.ds(k0, tq)]
        kt_scr[slot, MLA_NOPE_DIM:, :] = krt_ref[:, pl.ds(k0, tq)]
        return kt_scr[slot]

    def v_tile(t, slot):
        ve_scr[slot, :, 0:MLA_V_DIM] = v_ref[pl.ds(pl.multiple_of(t * tq, tq), tq), :]
        return ve_scr[slot]

    _flash_pipeline(qi, q_ref, k_tile, v_tile, lambda slot, t, masked: diag_mask if masked else None,
                    s_scr, p_scr, a_scr, m_scr, acc_scr)
    o_ref[...] = _flash_output(acc_scr).astype(o_ref.dtype)


def mla_attention(q, kv, k_nope_t, k_rope_t, s_len, tq=512):
    tq = min(tq, s_len)
    qw = 2 * LANES
    return pl.pallas_call(
        functools.partial(_mla_body, tq=tq),
        grid=(MLA_HEADS, s_len // tq),
        in_specs=[
            pl.BlockSpec((tq, qw), lambda h, i: (i, h)),
            pl.BlockSpec((MLA_NOPE_DIM, s_len), lambda h, i: (h, 0)),
            pl.BlockSpec((LANES, s_len), lambda h, i: (0, 0)),
            pl.BlockSpec((s_len, MLA_V_DIM), lambda h, i: (0, MLA_HEADS + h)),
        ],
        out_specs=pl.BlockSpec((tq, MLA_V_DIM), lambda h, i: (i, h)),
        out_shape=jax.ShapeDtypeStruct((s_len, MLA_HEADS * MLA_V_DIM), BF16),
        scratch_shapes=[pltpu.VMEM((FLASH_SLOTS, qw, tq), BF16), pltpu.VMEM((FLASH_SLOTS, tq, 2 * MLA_V_DIM), BF16),
                        pltpu.VMEM((FLASH_SLOTS, tq, tq), F32), pltpu.VMEM((FLASH_SLOTS, tq, tq), BF16),
                        pltpu.VMEM((FLASH_SLOTS, tq, LANES), F32), pltpu.VMEM((tq, LANES), F32),
                        pltpu.VMEM((tq, 2 * MLA_V_DIM), F32)],
        compiler_params=_params(("parallel", "parallel")),
        name="mla_attention",
    )(q, k_nope_t, k_rope_t, kv)


_IN_SIZES = (DSW_W, DSW_W, DSW_W, NSA_QW, NSA_KVW, NSA_KVW, NSA_KVW, NSA_KVW, NSA_KVW, NSA_KVW,
             3 * NSA_HEADS, MLA_Q_LORA, MLA_KV_LORA, MLA_ROPE_DIM)
_IN_NAMES = ("a_q", "a_k", "a_v", "n_q", "n_kc", "n_vc", "n_ks", "n_vs", "n_kw", "n_vw", "n_gate",
             "m_cq", "m_ckv", "m_kr")
_IN_OFF = dict(zip(_IN_NAMES, np.concatenate([[0], np.cumsum(_IN_SIZES)[:-1]]).tolist()))
_IN_LEN = dict(zip(_IN_NAMES, _IN_SIZES))

_B_ORDER = ("a_q", "a_k", "n_q", "n_ks", "n_kw", "n_vs", "n_vw", "a_v")
_B_ROPE = 2 * DSW_W + NSA_QW + 2 * NSA_KVW
_B_WIDTH = sum(_IN_LEN[k] for k in _B_ORDER)
_B_COL = dict(zip(_B_ORDER, np.concatenate([[0], np.cumsum([_IN_LEN[k] for k in _B_ORDER])[:-1]]).tolist()))
_F_COL = {"m_cq": 0, "m_ckv": MLA_Q_LORA, "n_kc": 2048, "n_vc": 2304, "m_kr": 2560, "n_gate": 2688}
_F_WIDTH = 3072
_HALF_ROPE = MLA_ROPE_DIM // 2


def _spread_rope_cols(w):
    z = jnp.zeros((w.shape[0], _HALF_ROPE), w.dtype)
    return jnp.concatenate([w[:, :_HALF_ROPE], z, w[:, _HALF_ROPE:], z], axis=1)


def _prep_w_in(w):
    col = lambda name: w[:, _IN_OFF[name]:_IN_OFF[name] + _IN_LEN[name]]
    wb = jnp.concatenate([col(k) for k in _B_ORDER], axis=1).astype(BF16)
    used = _F_COL["n_gate"] + _IN_LEN["n_gate"]
    wf = jnp.concatenate([col("m_cq"), col("m_ckv"), col("n_kc"), col("n_vc"), _spread_rope_cols(col("m_kr")),
                          col("n_gate"), jnp.zeros((w.shape[0], _F_WIDTH - used), w.dtype)], axis=1).astype(BF16)
    return wb, wf


_B_MODES = ([2] * (DSW_W // LANES) + [1] * (DSW_W // LANES) + [2] * (NSA_QW // LANES)
            + [1] * (2 * NSA_KVW // LANES) + [0] * ((_B_WIDTH - _B_ROPE) // LANES))
_F_MODES = [0] * 16 + [1, 1, 0, 0, 2] + [0] * 3
_Q_MODES = [0, 1] * MLA_HEADS
C_HEAD = HEAD_DIM ** -0.5 * LOG2E
C_MLA = (MLA_NOPE_DIM + MLA_ROPE_DIM) ** -0.5 * LOG2E


def _prep_w_uq(w):
    w = w.reshape(MLA_Q_LORA, MLA_HEADS, MLA_NOPE_DIM + MLA_ROPE_DIM)
    z = jnp.zeros((MLA_Q_LORA, MLA_HEADS, _HALF_ROPE), w.dtype)
    w = jnp.concatenate([w[..., :MLA_NOPE_DIM], w[..., MLA_NOPE_DIM:MLA_NOPE_DIM + _HALF_ROPE], z,
                         w[..., MLA_NOPE_DIM + _HALF_ROPE:], z], axis=-1)
    return w.reshape(MLA_Q_LORA, MLA_HEADS * 2 * LANES).astype(BF16)


def _prep_w_ukv(w):
    w = w.reshape(MLA_KV_LORA, MLA_HEADS, 2, MLA_NOPE_DIM).transpose(0, 2, 1, 3)
    return w.reshape(MLA_KV_LORA, 2 * MLA_HEADS * MLA_NOPE_DIM).astype(BF16)


def _rope_tables(positions):
    pos = positions.astype(F32)[:, None]

    def cs(dim):
        inv = ROPE_THETA ** (-jnp.arange(0, dim, 2, dtype=F32) / dim)
        ang = pos * inv
        return jnp.cos(ang), jnp.sin(ang)

    c128, s128 = cs(HEAD_DIM)
    c64, s64 = cs(MLA_ROPE_DIM)
    z = jnp.zeros_like(c64)
    one = jnp.ones((pos.shape[0], LANES), F32)
    zero = jnp.zeros((pos.shape[0], LANES), F32)
    a128, b128 = jnp.concatenate([c128, c128], 1), jnp.concatenate([-s128, s128], 1)
    a64, b64 = jnp.concatenate([c64, z, c64, z], 1), jnp.concatenate([-s64, z, s64, z], 1)
    return {"b": (jnp.stack([one, a128, a128 * C_HEAD]), jnp.stack([zero, b128, b128 * C_HEAD])),
            "f": (jnp.stack([one, a128, a64]), jnp.stack([zero, b128, b64])),
            "q": (jnp.stack([one * C_MLA, a64 * C_MLA]), jnp.stack([zero, b64 * C_MLA]))}


def _mixer(x, sc1, sh1, tabs, w_in, cmp_pe, cmp_w1, cmp_w2, q_norm, kv_norm, w_uq, w_ukv):
    s_len = x.shape[0]
    wb, wf = _prep_w_in(w_in)
    pb = mm_fullk(x, 0, D_MODEL, wb, prologue="mod", p1=sc1, p2=sh1, epilogue="rope", modes=_B_MODES,
                  ta=tabs["b"][0], tb=tabs["b"][1], out_dtype=BF16)
    pf = mm_fullk(x, 0, D_MODEL, wf, prologue="mod", p1=sc1, p2=sh1, epilogue="rope", modes=_F_MODES,
                  ta=tabs["f"][0], tb=tabs["f"][1], out_dtype=F32)

    blk = lambda name: _B_COL[name] // DSW_W
    outs = [dilated_pattern(pb, s_len, _B_WIDTH, blk("a_q"), blk("a_k"), blk("a_v"), window, dil)
            for window, dil in DSW_PATTERNS]
    out_a = dilated_combine([o for o, _ in outs], [t for _, t in outs])

    ncp = s_len // NSA_CMP_STRIDE

    def blocks16(name):
        t = pf[:, _F_COL[name]:_F_COL[name] + NSA_KVW]
        return t.reshape(ncp, NSA_CMP_STRIDE, NSA_KV_HEADS, HEAD_DIM).transpose(2, 0, 1, 3).reshape(
            NSA_KV_HEADS, ncp, NSA_CMP_STRIDE * HEAD_DIM)

    kv_cmp = nsa_compress(jnp.stack([blocks16("n_kc"), blocks16("n_vc")]), cmp_pe,
                          cmp_w1.astype(BF16), cmp_w2.astype(BF16))
    gates = pf[:, _F_COL["n_gate"]:_F_COL["n_gate"] + 3 * NSA_HEADS]
    gates = gates.reshape(s_len, NSA_KV_HEADS, 3 * NSA_GROUP).transpose(1, 0, 2)
    gw = NSA_GROUP * HEAD_DIM
    ks_t = pb[:, _B_COL["n_ks"]:_B_COL["n_ks"] + NSA_KVW].T
    u = nsa_cmp_sel(pb, ks_t, s_len, _B_COL["n_q"] // gw, _B_COL["n_vs"] // HEAD_DIM, kv_cmp, gates)
    out_b = nsa_window(pb, s_len, _B_COL["n_q"] // gw, _B_COL["n_kw"] // HEAD_DIM, _B_COL["n_vw"] // HEAD_DIM,
                       gates, u)

    q = mm_fullk(pf, 0, MLA_Q_LORA, _prep_w_uq(w_uq), prologue="rms", p1=q_norm, epilogue="rope",
                 modes=_Q_MODES, ta=tabs["q"][0], tb=tabs["q"][1], out_dtype=BF16)
    kv = mm_fullk(pf, _F_COL["m_ckv"] // MLA_KV_LORA, MLA_KV_LORA, _prep_w_ukv(w_ukv), prologue="rms",
                  p1=kv_norm, out_dtype=BF16)
    k_rope_t = pf[:, _F_COL["m_kr"]:_F_COL["m_kr"] + LANES].astype(BF16).T
    k_nope_t = kv[:, :MLA_HEADS * MLA_NOPE_DIM].T
    out_c = mla_attention(q, kv, k_nope_t, k_rope_t, s_len)
    return jnp.concatenate([out_a, out_b, out_c], axis=1)


def kernel(x, c, positions, w_ada, b_ada, w_in, nsa_cmp_pe, nsa_cmp_w1, nsa_cmp_w2, mla_q_norm, mla_kv_norm,
           mla_w_uq, mla_w_ukv, w_out, ln1_g, ln1_b, mlp_w1, mlp_w2, ln2_g, ln2_b):
    assert x.shape[0] == 1, "kernel handles batch size 1"
    xs = x[0]
    d = xs.shape[1]
    tabs = _rope_tables(positions[0])
    for l in range(DEPTH):
        mod = adaln(c, w_ada[l], b_ada[l])
        sh1, sc1, g1, sh2, sc2, g2 = [mod[:, i * d:(i + 1) * d] for i in range(6)]
        mixed = _mixer(xs, sc1, sh1, tabs, w_in[l], nsa_cmp_pe[l], nsa_cmp_w1[l], nsa_cmp_w2[l],
                       mla_q_norm[l], mla_kv_norm[l], mla_w_uq[l], mla_w_ukv[l])
        xs = mm_ln(mixed, w_out[l].astype(BF16), xs, g1, ln1_g[l], ln1_b[l])
        act = mm_fullk(xs, 0, d, mlp_w1[l].astype(BF16), prologue="mod", p1=sc2, p2=sh2, epilogue="relu2",
                       out_dtype=BF16)
        xs = mm_ln(act, mlp_w2[l].astype(BF16), xs, g2, ln2_g[l], ln2_b[l])
    return xs[None]
```

```python
import functools

import jax
import jax.numpy as jnp
import numpy as np
from jax import lax
from jax.experimental import pallas as pl
from jax.experimental.pallas import tpu as pltpu

D_MODEL = 4096
DEPTH = 2
HEAD_DIM = 128
ROPE_THETA = 10000.0
QBLOCK = 128
DSW_HEADS = 8
DSW_PATTERNS = ((128, 1), (512, 4), (2048, 16))
NSA_HEADS = 8
NSA_KV_HEADS = 2
NSA_GROUP = NSA_HEADS // NSA_KV_HEADS
NSA_CMP_LEN = 32
NSA_CMP_STRIDE = 16
NSA_CMP_HIDDEN = 256
NSA_SEL_BLOCK = 64
NSA_TOP_N = 16
NSA_WINDOW = 512
MLA_HEADS = 16
MLA_Q_LORA = 1536
MLA_KV_LORA = 512
MLA_NOPE_DIM = 128
MLA_ROPE_DIM = 64
MLA_V_DIM = 128
D_FF = 4 * D_MODEL
ALPHA = (2 * DEPTH) ** 0.25

DSW_W = DSW_HEADS * HEAD_DIM
NSA_QW = NSA_HEADS * HEAD_DIM
NSA_KVW = NSA_KV_HEADS * HEAD_DIM

LANES = 128
V7X_VMEM_LIMIT = 56 * 1024 * 1024
V7X_VMEM_LIMIT_LARGE = 60 * 1024 * 1024

F32 = jnp.float32
BF16 = jnp.bfloat16
NEG_INF = float("-inf")
LOG2E = 1.4426950408889634
FLASH_ROWS = 64
FLASH_UNROLL = 4
NT_DIMS = (((1,), (1,)), ((), ()))


def _params(sem, vmem=V7X_VMEM_LIMIT):
    return pltpu.CompilerParams(dimension_semantics=sem, vmem_limit_bytes=vmem)


def _adaln_body(cb_ref, w_ref, b_ref, o_ref, *, tn):
    cb = cb_ref[...]
    for s in range(tn // LANES):
        sl = slice(s * LANES, (s + 1) * LANES)
        o_ref[:, sl] = jnp.sum(w_ref[:, sl] * cb, axis=0, keepdims=True) + b_ref[:, sl]


def adaln(c, w, b, tn=512):
    d, n = w.shape
    cb = jnp.broadcast_to(c.reshape(d, 1), (d, LANES))
    return pl.pallas_call(
        functools.partial(_adaln_body, tn=tn),
        grid=(n // tn,),
        in_specs=[
            pl.BlockSpec((d, LANES), lambda j: (0, 0)),
            pl.BlockSpec((d, tn), lambda j: (0, j)),
            pl.BlockSpec((1, tn), lambda j: (0, j)),
        ],
        out_specs=pl.BlockSpec((1, tn), lambda j: (0, j)),
        out_shape=jax.ShapeDtypeStruct((1, n), F32),
        compiler_params=_params(("arbitrary",)),
        name="adaln",
    )(cb, w, b.reshape(1, n))


def _mm_fullk_body(*refs, prologue, epilogue, nsub):
    refs = list(refs)
    mode_ref = refs.pop(0) if epilogue == "rope" else None
    x_ref, p1_ref = refs.pop(0), refs.pop(0)
    p2_ref = refs.pop(0) if prologue == "mod" else None
    w_ref = refs.pop(0)
    if epilogue == "rope":
        ta_ref, tb_ref = refs.pop(0), refs.pop(0)
    o_ref, h_scr = refs
    j = pl.program_id(1)

    @pl.when(j == 0)
    def _():
        x = x_ref[...]
        if prologue == "mod":
            h = x * (1.0 + p1_ref[...]) + p2_ref[...]
        else:
            h = x * lax.rsqrt(jnp.mean(x * x, axis=-1, keepdims=True) + 1e-6) * p1_ref[...]
        h_scr[...] = h.astype(BF16)

    acc = jnp.dot(h_scr[...], w_ref[...], preferred_element_type=F32)
    if epilogue == "relu2":
        r = jnp.maximum(acc, 0.0)
        o_ref[...] = (r * r).astype(o_ref.dtype)
    elif epilogue == "rope":
        for s in range(nsub):
            sl = slice(s * LANES, (s + 1) * LANES)
            md = mode_ref[j * nsub + s]
            sub = acc[:, sl]
            o_ref[:, sl] = (sub * ta_ref[md] + pltpu.roll(sub, LANES // 2, 1) * tb_ref[md]).astype(o_ref.dtype)
    else:
        o_ref[...] = acc.astype(o_ref.dtype)


def mm_fullk(x, x_col_block, k, w, *, prologue, p1, p2=None, epilogue="none", modes=None, ta=None, tb=None,
             out_dtype=BF16, tm=512, tn=1024):
    m = x.shape[0]
    n = w.shape[1]
    tm = min(tm, m)
    assert m % tm == 0 and n % tn == 0 and w.shape[0] == k
    nsub = tn // LANES
    rope = epilogue == "rope"
    npre = 1 if rope else 0

    def im(f):
        return (lambda i, j, *_: f(i, j))

    in_specs = [pl.BlockSpec((tm, k), im(lambda i, j: (i, x_col_block))),
                pl.BlockSpec((1, k), im(lambda i, j: (0, 0)))]
    args = [x, p1.reshape(1, k)]
    if prologue == "mod":
        in_specs.append(pl.BlockSpec((1, k), im(lambda i, j: (0, 0))))
        args.append(p2.reshape(1, k))
    in_specs.append(pl.BlockSpec((k, tn), im(lambda i, j: (0, j))))
    args.append(w)
    if rope:
        nmode = ta.shape[0]
        in_specs += [pl.BlockSpec((nmode, tm, LANES), im(lambda i, j: (0, i, 0)))] * 2
        args += [ta, tb]
    grid_spec = pltpu.PrefetchScalarGridSpec(
        num_scalar_prefetch=npre,
        grid=(m // tm, n // tn),
        in_specs=in_specs,
        out_specs=pl.BlockSpec((tm, tn), im(lambda i, j: (i, j))),
        scratch_shapes=[pltpu.VMEM((tm, k), BF16)],
    )
    fn = pl.pallas_call(
        functools.partial(_mm_fullk_body, prologue=prologue, epilogue=epilogue, nsub=nsub),
        grid_spec=grid_spec,
        out_shape=jax.ShapeDtypeStruct((m, n), out_dtype),
        compiler_params=_params(("parallel", "arbitrary")),
        name="mm_fullk_" + prologue + "_" + epilogue,
    )
    if rope:
        return fn(jnp.asarray(modes, jnp.int32), *args)
    return fn(*args)


def _mm_ln_body(lhs_ref, w_ref, x_ref, g_ref, lng_ref, lnb_ref, o_ref, acc_ref, *, nk):
    kk = pl.program_id(1)

    @pl.when(kk == 0)
    def _():
        acc_ref[...] = jnp.dot(lhs_ref[...], w_ref[...], preferred_element_type=F32)

    @pl.when(kk > 0)
    def _():
        acc_ref[...] += jnp.dot(lhs_ref[...], w_ref[...], preferred_element_type=F32)

    @pl.when(kk == nk - 1)
    def _():
        z = ALPHA * x_ref[...] + (1.0 + g_ref[...]) * acc_ref[...]
        mu = jnp.mean(z, axis=-1, keepdims=True)
        zc = z - mu
        var = jnp.mean(zc * zc, axis=-1, keepdims=True)
        o_ref[...] = zc * lax.rsqrt(var + 1e-5) * lng_ref[...] + lnb_ref[...]


def mm_ln(lhs, w, x, gate, ln_g, ln_b, tm=512, tk=512):
    m, k = lhs.shape
    n = w.shape[1]
    tm = min(tm, m)
    nk = k // tk
    row = lambda i, kk: (0, 0)
    return pl.pallas_call(
        functools.partial(_mm_ln_body, nk=nk),
        grid=(m // tm, nk),
        in_specs=[
            pl.BlockSpec((tm, tk), lambda i, kk: (i, kk)),
            pl.BlockSpec((tk, n), lambda i, kk: (kk, 0)),
            pl.BlockSpec((tm, n), lambda i, kk: (i, 0)),
            pl.BlockSpec((1, n), row), pl.BlockSpec((1, n), row), pl.BlockSpec((1, n), row),
        ],
        out_specs=pl.BlockSpec((tm, n), lambda i, kk: (i, 0)),
        out_shape=jax.ShapeDtypeStruct((m, n), F32),
        scratch_shapes=[pltpu.VMEM((tm, n), F32)],
        compiler_params=_params(("parallel", "arbitrary"), V7X_VMEM_LIMIT_LARGE),
        name="mm_ln",
    )(lhs, w, x, gate.reshape(1, n), ln_g.reshape(1, n), ln_b.reshape(1, n))


def _dil_body(q_ref, kc_ref, kp_ref, vc_ref, vp_ref, o_ref, lse_ref, *, max_delta):
    n = pl.program_id(1)
    qi = lax.broadcasted_iota(jnp.int32, (QBLOCK, 2 * QBLOCK), 0)
    ki = lax.broadcasted_iota(jnp.int32, (QBLOCK, 2 * QBLOCK), 1)
    delta = QBLOCK + qi - ki
    valid = (delta >= 0) & (delta <= max_delta) & ((ki >= QBLOCK) | (n > 0))
    for h in range(DSW_HEADS):
        sl = slice(h * HEAD_DIM, (h + 1) * HEAD_DIM)
        k = jnp.concatenate([kp_ref[:, sl], kc_ref[:, sl]], axis=0)
        v = jnp.concatenate([vp_ref[:, sl], vc_ref[:, sl]], axis=0)
        s = lax.dot_general(q_ref[:, sl], k, NT_DIMS, preferred_element_type=F32)
        s = jnp.where(valid, s, NEG_INF)
        m = jnp.max(s, axis=-1, keepdims=True)
        p = jnp.exp2(s - m)
        l = jnp.sum(p, axis=-1, keepdims=True)
        o = jnp.dot(p.astype(BF16), v, preferred_element_type=F32)
        o_ref[:, sl] = o / l
        lse_ref[:, h:h + 1] = m + jnp.log2(l)


def dilated_pattern(qkv, s_len, width, q_blk, k_blk, v_blk, window, dil):
    l_len = s_len // dil
    nblk = l_len // QBLOCK
    max_delta = window // dil
    assert max_delta == QBLOCK
    per_row = width // DSW_W
    view = qkv.reshape(l_len, dil * width)
    cur = lambda blk: (lambda r, n: (n, r * per_row + blk))
    prev = lambda blk: (lambda r, n: (jnp.maximum(n - 1, 0), r * per_row + blk))
    bs = lambda f: pl.BlockSpec((QBLOCK, DSW_W), f)
    o, lse = pl.pallas_call(
        functools.partial(_dil_body, max_delta=max_delta),
        grid=(dil, nblk),
        in_specs=[bs(cur(q_blk)), bs(cur(k_blk)), bs(prev(k_blk)), bs(cur(v_blk)), bs(prev(v_blk))],
        out_specs=[pl.BlockSpec((QBLOCK, DSW_W), lambda r, n: (n, r)),
                   pl.BlockSpec((None, QBLOCK, DSW_HEADS), lambda r, n: (r, n, 0))],
        out_shape=[jax.ShapeDtypeStruct((l_len, dil * DSW_W), F32),
                   jax.ShapeDtypeStruct((dil, l_len, DSW_HEADS), F32)],
        compiler_params=_params(("parallel", "parallel")),
        name="dilated_d%d" % dil,
    )(view, view, view, view, view)
    o = o.reshape(s_len, DSW_W)
    lse = lse.transpose(1, 0, 2).reshape(s_len, DSW_HEADS)
    return o, lse


def _dil_combine_body(o1_ref, o2_ref, o3_ref, l1_ref, l2_ref, l3_ref, out_ref):
    ls = [l1_ref[...], l2_ref[...], l3_ref[...]]
    mx = jnp.maximum(jnp.maximum(ls[0], ls[1]), ls[2])
    ws = [jnp.exp2(t - mx) for t in ls]
    den = ws[0] + ws[1] + ws[2]
    os_ = [o1_ref, o2_ref, o3_ref]
    for h in range(DSW_HEADS):
        sl = slice(h * HEAD_DIM, (h + 1) * HEAD_DIM)
        num = sum(ws[i][:, h:h + 1] * os_[i][:, sl] for i in range(3))
        out_ref[:, sl] = (num / den[:, h:h + 1]).astype(out_ref.dtype)


def dilated_combine(os_, lses, tm=512):
    s_len = os_[0].shape[0]
    tm = min(tm, s_len)
    ob = pl.BlockSpec((tm, DSW_W), lambda i: (i, 0))
    lb = pl.BlockSpec((tm, DSW_HEADS), lambda i: (i, 0))
    return pl.pallas_call(
        _dil_combine_body,
        grid=(s_len // tm,),
        in_specs=[ob, ob, ob, lb, lb, lb],
        out_specs=ob,
        out_shape=jax.ShapeDtypeStruct((s_len, DSW_W), BF16),
        compiler_params=_params(("parallel",)),
        name="dilated_combine",
    )(*os_, *lses)


def _cmp_body(x_ref, pe_ref, w1_ref, w2_ref, o_ref, *, ncp):
    x = x_ref[...]
    half = NSA_CMP_STRIDE * HEAD_DIM
    a = jnp.dot((x + pe_ref[0:1, :]).astype(BF16), w1_ref[0:half, :], preferred_element_type=F32)
    b = jnp.dot((x + pe_ref[1:2, :]).astype(BF16), w1_ref[half:2 * half, :], preferred_element_type=F32)
    hid = a + pltpu.roll(b, ncp - 1, 0)
    act = jax.nn.gelu(hid)
    o_ref[...] = jnp.dot(act.astype(BF16), w2_ref[...], preferred_element_type=F32).astype(o_ref.dtype)


def nsa_compress(xs, pe, w1, w2):
    ncp = xs.shape[2]
    half = NSA_CMP_STRIDE * HEAD_DIM
    return pl.pallas_call(
        functools.partial(_cmp_body, ncp=ncp),
        grid=(2, NSA_KV_HEADS),
        in_specs=[
            pl.BlockSpec((None, None, ncp, half), lambda a, h: (a, h, 0, 0)),
            pl.BlockSpec((None, 2, half), lambda a, h: (a, 0, 0)),
            pl.BlockSpec((None, 2 * half, NSA_CMP_HIDDEN), lambda a, h: (a, 0, 0)),
            pl.BlockSpec((None, NSA_CMP_HIDDEN, HEAD_DIM), lambda a, h: (a, 0, 0)),
        ],
        out_specs=pl.BlockSpec((None, None, ncp, HEAD_DIM), lambda a, h: (a, h, 0, 0)),
        out_shape=jax.ShapeDtypeStruct((2, NSA_KV_HEADS, ncp, HEAD_DIM), BF16),
        compiler_params=_params(("parallel", "parallel")),
        name="nsa_compress",
    )(xs, pe.reshape(2, 2, half), w1, w2)


def _stack_heads(q):
    return jnp.concatenate([q[:, g * HEAD_DIM:(g + 1) * HEAD_DIM] for g in range(NSA_GROUP)], axis=0)


def _nsa_body(q_ref, kst_ref, vs_ref, kc_ref, vc_ref, covt_ref, e0_ref, gate_ref, u_ref,
              sel_scr, q4_scr, mk_scr, ve_scr, s_scr, p_scr, a_scr, m_scr, acc_scr,
              *, ncp, nselp, ntop, tk):
    n = pl.program_id(1)
    rows = NSA_GROUP * QBLOCK
    q4_scr[...] = _stack_heads(q_ref[...])
    q4 = q4_scr[...]

    s = lax.dot_general(q4, kc_ref[...], NT_DIMS, preferred_element_type=F32)
    qpos_r = n * QBLOCK + (lax.broadcasted_iota(jnp.int32, (rows, ncp), 0) & (QBLOCK - 1))
    cidx = lax.broadcasted_iota(jnp.int32, (rows, ncp), 1)
    s = jnp.where(cidx * NSA_CMP_STRIDE + (NSA_CMP_LEN - 1) <= qpos_r, s, NEG_INF)
    m = jnp.max(s, axis=-1, keepdims=True)
    e = jnp.exp2(s - jnp.where(m == NEG_INF, 0.0, m))
    l = jnp.sum(e, axis=-1, keepdims=True)
    p = e / jnp.where(l > 0, l, 1.0)
    o_c = jnp.dot(p.astype(BF16), vc_ref[...], preferred_element_type=F32)

    psum = p[0:QBLOCK]
    for g in range(1, NSA_GROUP):
        psum = psum + p[g * QBLOCK:(g + 1) * QBLOCK]
    p_hi = psum.astype(BF16)
    p_lo = (psum - p_hi.astype(F32)).astype(BF16)
    covt = covt_ref[...]
    imp_t = (lax.dot_general(covt, p_hi, NT_DIMS, preferred_element_type=F32)
             + lax.dot_general(covt, p_lo, NT_DIMS, preferred_element_type=F32))

    jblk = lax.broadcasted_iota(jnp.int32, (nselp, QBLOCK), 0)
    qpos_c = n * QBLOCK + lax.broadcasted_iota(jnp.int32, (nselp, QBLOCK), 1)
    cur = qpos_c // NSA_SEL_BLOCK
    forced = (jblk == 0) | (jblk == cur) | (jblk == cur - 1)
    valid = jblk * NSA_SEL_BLOCK <= qpos_c
    score0 = jnp.where(valid, jnp.where(forced, jnp.inf, imp_t), NEG_INF)

    def pick(_, carry):
        score, picked = carry
        mx = jnp.max(score, axis=0, keepdims=True)
        idx = jnp.min(jnp.where(score == mx, jblk, nselp), axis=0, keepdims=True)
        hit = jblk == idx
        return jnp.where(hit, NEG_INF, score), jnp.where(hit, 1.0, picked)

    _, picked = lax.fori_loop(0, ntop, pick, (score0, jnp.zeros((nselp, QBLOCK), F32)))
    sel_scr[...] = jnp.where(valid, picked, 0.0).T

    blocks_per_tile = tk // NSA_SEL_BLOCK
    _flash_init(m_scr, acc_scr, ve_scr)

    def k_tile(t, slot, buf):
        shift = (nselp - t * blocks_per_tile) % nselp
        sel_t = pltpu.roll(sel_scr[...], shift, 1)[:, 0:LANES].astype(BF16)
        mk_scr[slot] = jnp.dot(sel_t, e0_ref[...], preferred_element_type=F32)
        return kst_ref[:, pl.ds(pl.multiple_of(t * tk, tk), tk)]

    def v_tile(t, buf):
        ve_scr[buf, :, 0:HEAD_DIM] = vs_ref[pl.ds(pl.multiple_of(t * tk, tk), tk), :]
        return ve_scr[buf]

    def mask_for(slot, t, masked):
        def mask_fn(r, s):
            rq = r % QBLOCK
            ok = mk_scr[slot, rq:rq + FLASH_ROWS, :] > 0.5
            if masked:
                qpos = n * QBLOCK + rq + lax.broadcasted_iota(jnp.int32, s.shape, 0)
                ok = ok & (t * tk + lax.broadcasted_iota(jnp.int32, s.shape, 1) <= qpos)
            return jnp.where(ok, s, NEG_INF)
        return mask_fn

    _flash_pipeline((n * QBLOCK + QBLOCK - 1) // tk, q4_scr, k_tile, v_tile, mask_for,
                    s_scr, p_scr, a_scr, m_scr, acc_scr)
    o_s = _flash_output(acc_scr)

    gates = jax.nn.sigmoid(gate_ref[...])
    for g in range(NSA_GROUP):
        rs = slice(g * QBLOCK, (g + 1) * QBLOCK)
        u_ref[:, g * HEAD_DIM:(g + 1) * HEAD_DIM] = (gates[:, 3 * g:3 * g + 1] * o_c[rs]
                                                     + gates[:, 3 * g + 1:3 * g + 2] * o_s[rs])


def nsa_cmp_sel(qkv, ks_t, s_len, q_blk, vs_blk, kv_cmp, gates, tk=512):
    nb = s_len // QBLOCK
    ncp = s_len // NSA_CMP_STRIDE
    nsel = s_len // NSA_SEL_BLOCK
    nselp = -(-nsel // LANES) * LANES
    ntop = min(NSA_TOP_N, nsel)
    ci = np.arange(ncp)[None, :] * NSA_CMP_STRIDE
    sj = np.arange(nselp)[:, None] * NSA_SEL_BLOCK
    cov = (ci < sj + NSA_SEL_BLOCK) & (ci + NSA_CMP_LEN > sj) & (np.arange(ncp)[None, :] < ncp - 1) & (sj < s_len)
    covt = jnp.asarray(cov.astype(np.float32), BF16)
    e0 = jnp.asarray((np.arange(tk)[None, :] // NSA_SEL_BLOCK == np.arange(LANES)[:, None]).astype(np.float32), BF16)
    gw = NSA_GROUP * HEAD_DIM
    rows = NSA_GROUP * QBLOCK
    return pl.pallas_call(
        functools.partial(_nsa_body, ncp=ncp, nselp=nselp, ntop=ntop, tk=tk),
        grid=(NSA_KV_HEADS, nb),
        in_specs=[
            pl.BlockSpec((QBLOCK, gw), lambda h, n: (n, q_blk + h)),
            pl.BlockSpec((HEAD_DIM, s_len), lambda h, n: (h, 0)),
            pl.BlockSpec((s_len, HEAD_DIM), lambda h, n: (0, vs_blk + h)),
            pl.BlockSpec((None, None, ncp, HEAD_DIM), lambda h, n: (0, h, 0, 0)),
            pl.BlockSpec((None, None, ncp, HEAD_DIM), lambda h, n: (1, h, 0, 0)),
            pl.BlockSpec((nselp, ncp), lambda h, n: (0, 0)),
            pl.BlockSpec((LANES, tk), lambda h, n: (0, 0)),
            pl.BlockSpec((None, QBLOCK, 3 * NSA_GROUP), lambda h, n: (h, n, 0)),
        ],
        out_specs=pl.BlockSpec((QBLOCK, gw), lambda h, n: (n, h)),
        out_shape=jax.ShapeDtypeStruct((s_len, NSA_QW), F32),
        scratch_shapes=[pltpu.VMEM((QBLOCK, nselp), F32), pltpu.VMEM((rows, HEAD_DIM), BF16),
                        pltpu.VMEM((FLASH_UNROLL, QBLOCK, tk), F32), pltpu.VMEM((FLASH_UNROLL + 1, tk, 2 * HEAD_DIM), BF16),
                        pltpu.VMEM((FLASH_UNROLL, rows, tk), F32), pltpu.VMEM((FLASH_UNROLL, rows, tk), BF16),
                        pltpu.VMEM((FLASH_UNROLL, rows, LANES), F32), pltpu.VMEM((rows, LANES), F32),
                        pltpu.VMEM((rows, 2 * HEAD_DIM), F32)],
        compiler_params=_params(("parallel", "arbitrary")),
        name="nsa_cmp_sel",
    )(qkv, ks_t, qkv, kv_cmp, kv_cmp, covt, e0, gates)


def _win_body(q_ref, kw_ref, vw_ref, gate_ref, u_ref, o_ref, *, span):
    n = pl.program_id(1)
    rows = NSA_GROUP * QBLOCK
    q4 = _stack_heads(q_ref[...])
    start = pl.multiple_of(jnp.maximum(n * QBLOCK - (span - QBLOCK), 0), QBLOCK)
    s = lax.dot_general(q4, kw_ref[pl.ds(start, span), :], NT_DIMS, preferred_element_type=F32)
    qpos = n * QBLOCK + (lax.broadcasted_iota(jnp.int32, (rows, span), 0) & (QBLOCK - 1))
    delta = qpos - (start + lax.broadcasted_iota(jnp.int32, (rows, span), 1))
    s = jnp.where((delta >= 0) & (delta <= NSA_WINDOW - 1), s, NEG_INF)
    m = jnp.max(s, axis=-1, keepdims=True)
    p = jnp.exp2(s - m)
    l = jnp.sum(p, axis=-1, keepdims=True)
    o_w = jnp.dot(p.astype(BF16), vw_ref[pl.ds(start, span), :], preferred_element_type=F32) / l
    gates = jax.nn.sigmoid(gate_ref[...])
    for g in range(NSA_GROUP):
        sl = slice(g * HEAD_DIM, (g + 1) * HEAD_DIM)
        o_ref[:, sl] = (u_ref[:, sl] + gates[:, 3 * g + 2:3 * g + 3] * o_w[g * QBLOCK:(g + 1) * QBLOCK]
                        ).astype(o_ref.dtype)


def nsa_window(qkv, s_len, q_blk, kw_blk, vw_blk, gates, u):
    nb = s_len // QBLOCK
    span = (-(-(NSA_WINDOW - 1) // QBLOCK) + 1) * QBLOCK
    gw = NSA_GROUP * HEAD_DIM
    return pl.pallas_call(
        functools.partial(_win_body, span=span),
        grid=(NSA_KV_HEADS, nb),
        in_specs=[
            pl.BlockSpec((QBLOCK, gw), lambda h, n: (n, q_blk + h)),
            pl.BlockSpec((s_len, HEAD_DIM), lambda h, n: (0, kw_blk + h)),
            pl.BlockSpec((s_len, HEAD_DIM), lambda h, n: (0, vw_blk + h)),
            pl.BlockSpec((None, QBLOCK, 3 * NSA_GROUP), lambda h, n: (h, n, 0)),
            pl.BlockSpec((QBLOCK, gw), lambda h, n: (n, h)),
        ],
        out_specs=pl.BlockSpec((QBLOCK, gw), lambda h, n: (n, h)),
        out_shape=jax.ShapeDtypeStruct((s_len, NSA_QW), BF16),
        compiler_params=_params(("parallel", "parallel")),
        name="nsa_window",
    )(qkv, qkv, qkv, gates, u)


def _softmax_rows(s_ref, p_ref, a_ref, m_scr, mask_fn):
    rows, tk = s_ref.shape
    nrep = tk // LANES
    for r in range(0, rows, FLASH_ROWS):
        rs = slice(r, r + FLASH_ROWS)
        s = s_ref[rs, :]
        if mask_fn is not None:
            s = mask_fn(r, s)
        m_old = m_scr[rs, :]
        m_new = jnp.maximum(m_old, jnp.max(s, axis=1, keepdims=True))
        p_ref[rs, :] = jnp.exp2(s - jnp.concatenate([m_new] * nrep, axis=1)).astype(BF16)
        a_ref[rs, :] = jnp.exp2(m_old - m_new)
        m_scr[rs, :] = m_new


def _flash_pipeline(n_full, q_ref, k_tile, v_tile, mask_for, s_scr, p_scr, a_scr, m_scr, acc_scr):
    def scores(t, idx, buf):
        s_scr[idx] = jnp.dot(q_ref[...], k_tile(t, idx, buf), preferred_element_type=F32)

    def softmax(idx, t, masked):
        _softmax_rows(s_scr.at[idx], p_scr.at[idx], a_scr.at[idx], m_scr, mask_for(idx, t, masked))

    def values(idx, t, buf):
        a = a_scr[idx]
        acc_scr[...] = (jnp.concatenate([a, a], axis=1) * acc_scr[...]
                        + jnp.dot(p_scr[idx], v_tile(t, buf), preferred_element_type=F32))

    last = FLASH_UNROLL - 1
    p_scr[last] = jnp.zeros(p_scr.shape[1:], p_scr.dtype)
    a_scr[last] = jnp.ones(a_scr.shape[1:], a_scr.dtype)
    scores(0, 0, 0)

    def run(t0, count, masked_last, lookahead):
        for j in range(count):
            t = t0 + j
            if lookahead or j + 1 < count:
                scores(t + 1, (j + 1) % FLASH_UNROLL, j)
            softmax(j, t, masked_last and j == count - 1)
            values((j - 1) % FLASH_UNROLL, jnp.maximum(t - 1, 0), j)
        if not lookahead:
            values(count - 1, t0 + count - 1, count)

    def body(u, carry):
        run(FLASH_UNROLL * u, FLASH_UNROLL, False, True)
        return carry

    lax.fori_loop(0, n_full // FLASH_UNROLL, body, 0)
    rem = n_full % FLASH_UNROLL
    for r in range(FLASH_UNROLL):
        @pl.when(rem == r)
        def _(r=r):
            run(n_full - r, r + 1, True, False)


def _flash_init(m_scr, acc_scr, ve_scr):
    dv = ve_scr.shape[2] // 2
    m_scr[...] = jnp.full(m_scr.shape, NEG_INF, F32)
    acc_scr[...] = jnp.zeros(acc_scr.shape, F32)
    ve_scr[:, :, dv:] = jnp.ones((ve_scr.shape[0], ve_scr.shape[1], dv), ve_scr.dtype)


def _flash_output(acc_scr):
    dv = acc_scr.shape[1] // 2
    return acc_scr[:, 0:dv] / acc_scr[:, dv:]


def _mla_body(q_ref, knt_ref, krt_ref, v_ref, o_ref, kt_scr, ve_scr, s_scr, p_scr, a_scr, m_scr, acc_scr, *, tq):
    qi = pl.program_id(1)
    _flash_init(m_scr, acc_scr, ve_scr)

    def diag_mask(r, s):
        row = r + lax.broadcasted_iota(jnp.int32, s.shape, 0)
        col = lax.broadcasted_iota(jnp.int32, s.shape, 1)
        return jnp.where(col <= row, s, NEG_INF)

    def k_tile(t, slot, buf):
        k0 = pl.multiple_of(t * tq, tq)
        kt_scr[buf, 0:MLA_NOPE_DIM, :] = knt_ref[:, pl.ds(k0, tq)]
        kt_scr[buf, MLA_NOPE_DIM:, :] = krt_ref[:, pl.ds(k0, tq)]
        return kt_scr[buf]

    def v_tile(t, buf):
        ve_scr[buf, :, 0:MLA_V_DIM] = v_ref[pl.ds(pl.multiple_of(t * tq, tq), tq), :]
        return ve_scr[buf]

    _flash_pipeline(qi, q_ref, k_tile, v_tile, lambda slot, t, masked: diag_mask if masked else None,
                    s_scr, p_scr, a_scr, m_scr, acc_scr)
    o_ref[...] = _flash_output(acc_scr).astype(o_ref.dtype)


def mla_attention(q, kv, k_nope_t, k_rope_t, s_len, tq=512):
    tq = min(tq, s_len)
    qw = 2 * LANES
    return pl.pallas_call(
        functools.partial(_mla_body, tq=tq),
        grid=(MLA_HEADS, s_len // tq),
        in_specs=[
            pl.BlockSpec((tq, qw), lambda h, i: (i, h)),
            pl.BlockSpec((MLA_NOPE_DIM, s_len), lambda h, i: (h, 0)),
            pl.BlockSpec((LANES, s_len), lambda h, i: (0, 0)),
            pl.BlockSpec((s_len, MLA_V_DIM), lambda h, i: (0, MLA_HEADS + h)),
        ],
        out_specs=pl.BlockSpec((tq, MLA_V_DIM), lambda h, i: (i, h)),
        out_shape=jax.ShapeDtypeStruct((s_len, MLA_HEADS * MLA_V_DIM), BF16),
        scratch_shapes=[pltpu.VMEM((FLASH_UNROLL, qw, tq), BF16),
                        pltpu.VMEM((FLASH_UNROLL + 1, tq, 2 * MLA_V_DIM), BF16),
                        pltpu.VMEM((FLASH_UNROLL, tq, tq), F32), pltpu.VMEM((FLASH_UNROLL, tq, tq), BF16),
                        pltpu.VMEM((FLASH_UNROLL, tq, LANES), F32), pltpu.VMEM((tq, LANES), F32),
                        pltpu.VMEM((tq, 2 * MLA_V_DIM), F32)],
        compiler_params=_params(("parallel", "parallel")),
        name="mla_attention",
    )(q, k_nope_t, k_rope_t, kv)


_IN_SIZES = (DSW_W, DSW_W, DSW_W, NSA_QW, NSA_KVW, NSA_KVW, NSA_KVW, NSA_KVW, NSA_KVW, NSA_KVW,
             3 * NSA_HEADS, MLA_Q_LORA, MLA_KV_LORA, MLA_ROPE_DIM)
_IN_NAMES = ("a_q", "a_k", "a_v", "n_q", "n_kc", "n_vc", "n_ks", "n_vs", "n_kw", "n_vw", "n_gate",
             "m_cq", "m_ckv", "m_kr")
_IN_OFF = dict(zip(_IN_NAMES, np.concatenate([[0], np.cumsum(_IN_SIZES)[:-1]]).tolist()))
_IN_LEN = dict(zip(_IN_NAMES, _IN_SIZES))

_B_ORDER = ("a_q", "a_k", "n_q", "n_ks", "n_kw", "n_vs", "n_vw", "a_v")
_B_ROPE = 2 * DSW_W + NSA_QW + 2 * NSA_KVW
_B_WIDTH = sum(_IN_LEN[k] for k in _B_ORDER)
_B_COL = dict(zip(_B_ORDER, np.concatenate([[0], np.cumsum([_IN_LEN[k] for k in _B_ORDER])[:-1]]).tolist()))
_F_COL = {"m_cq": 0, "m_ckv": MLA_Q_LORA, "n_kc": 2048, "n_vc": 2304, "m_kr": 2560, "n_gate": 2688}
_F_WIDTH = 3072
_HALF_ROPE = MLA_ROPE_DIM // 2


def _spread_rope_cols(w):
    z = jnp.zeros((w.shape[0], _HALF_ROPE), w.dtype)
    return jnp.concatenate([w[:, :_HALF_ROPE], z, w[:, _HALF_ROPE:], z], axis=1)


def _prep_w_in(w):
    col = lambda name: w[:, _IN_OFF[name]:_IN_OFF[name] + _IN_LEN[name]]
    wb = jnp.concatenate([col(k) for k in _B_ORDER], axis=1).astype(BF16)
    used = _F_COL["n_gate"] + _IN_LEN["n_gate"]
    wf = jnp.concatenate([col("m_cq"), col("m_ckv"), col("n_kc"), col("n_vc"), _spread_rope_cols(col("m_kr")),
                          col("n_gate"), jnp.zeros((w.shape[0], _F_WIDTH - used), w.dtype)], axis=1).astype(BF16)
    return wb, wf


_B_MODES = ([2] * (DSW_W // LANES) + [1] * (DSW_W // LANES) + [2] * (NSA_QW // LANES)
            + [1] * (2 * NSA_KVW // LANES) + [0] * ((_B_WIDTH - _B_ROPE) // LANES))
_F_MODES = [0] * 16 + [1, 1, 0, 0, 2] + [0] * 3
_Q_MODES = [0, 1] * MLA_HEADS
C_HEAD = HEAD_DIM ** -0.5 * LOG2E
C_MLA = (MLA_NOPE_DIM + MLA_ROPE_DIM) ** -0.5 * LOG2E


def _prep_w_uq(w):
    w = w.reshape(MLA_Q_LORA, MLA_HEADS, MLA_NOPE_DIM + MLA_ROPE_DIM)
    z = jnp.zeros((MLA_Q_LORA, MLA_HEADS, _HALF_ROPE), w.dtype)
    w = jnp.concatenate([w[..., :MLA_NOPE_DIM], w[..., MLA_NOPE_DIM:MLA_NOPE_DIM + _HALF_ROPE], z,
                         w[..., MLA_NOPE_DIM + _HALF_ROPE:], z], axis=-1)
    return w.reshape(MLA_Q_LORA, MLA_HEADS * 2 * LANES).astype(BF16)


def _prep_w_ukv(w):
    w = w.reshape(MLA_KV_LORA, MLA_HEADS, 2, MLA_NOPE_DIM).transpose(0, 2, 1, 3)
    return w.reshape(MLA_KV_LORA, 2 * MLA_HEADS * MLA_NOPE_DIM).astype(BF16)


def _rope_tables(positions):
    pos = positions.astype(F32)[:, None]

    def cs(dim):
        inv = ROPE_THETA ** (-jnp.arange(0, dim, 2, dtype=F32) / dim)
        ang = pos * inv
        return jnp.cos(ang), jnp.sin(ang)

    c128, s128 = cs(HEAD_DIM)
    c64, s64 = cs(MLA_ROPE_DIM)
    z = jnp.zeros_like(c64)
    one = jnp.ones((pos.shape[0], LANES), F32)
    zero = jnp.zeros((pos.shape[0], LANES), F32)
    a128, b128 = jnp.concatenate([c128, c128], 1), jnp.concatenate([-s128, s128], 1)
    a64, b64 = jnp.concatenate([c64, z, c64, z], 1), jnp.concatenate([-s64, z, s64, z], 1)
    return {"b": (jnp.stack([one, a128, a128 * C_HEAD]), jnp.stack([zero, b128, b128 * C_HEAD])),
            "f": (jnp.stack([one, a128, a64]), jnp.stack([zero, b128, b64])),
            "q": (jnp.stack([one * C_MLA, a64 * C_MLA]), jnp.stack([zero, b64 * C_MLA]))}


def _mixer(x, sc1, sh1, tabs, w_in, cmp_pe, cmp_w1, cmp_w2, q_norm, kv_norm, w_uq, w_ukv):
    s_len = x.shape[0]
    wb, wf = _prep_w_in(w_in)
    pb = mm_fullk(x, 0, D_MODEL, wb, prologue="mod", p1=sc1, p2=sh1, epilogue="rope", modes=_B_MODES,
                  ta=tabs["b"][0], tb=tabs["b"][1], out_dtype=BF16)
    pf = mm_fullk(x, 0, D_MODEL, wf, prologue="mod", p1=sc1, p2=sh1, epilogue="rope", modes=_F_MODES,
                  ta=tabs["f"][0], tb=tabs["f"][1], out_dtype=F32)

    blk = lambda name: _B_COL[name] // DSW_W
    outs = [dilated_pattern(pb, s_len, _B_WIDTH, blk("a_q"), blk("a_k"), blk("a_v"), window, dil)
            for window, dil in DSW_PATTERNS]
    out_a = dilated_combine([o for o, _ in outs], [t for _, t in outs])

    ncp = s_len // NSA_CMP_STRIDE

    def blocks16(name):
        t = pf[:, _F_COL[name]:_F_COL[name] + NSA_KVW]
        return t.reshape(ncp, NSA_CMP_STRIDE, NSA_KV_HEADS, HEAD_DIM).transpose(2, 0, 1, 3).reshape(
            NSA_KV_HEADS, ncp, NSA_CMP_STRIDE * HEAD_DIM)

    kv_cmp = nsa_compress(jnp.stack([blocks16("n_kc"), blocks16("n_vc")]), cmp_pe,
                          cmp_w1.astype(BF16), cmp_w2.astype(BF16))
    gates = pf[:, _F_COL["n_gate"]:_F_COL["n_gate"] + 3 * NSA_HEADS]
    gates = gates.reshape(s_len, NSA_KV_HEADS, 3 * NSA_GROUP).transpose(1, 0, 2)
    gw = NSA_GROUP * HEAD_DIM
    ks_t = pb[:, _B_COL["n_ks"]:_B_COL["n_ks"] + NSA_KVW].T
    u = nsa_cmp_sel(pb, ks_t, s_len, _B_COL["n_q"] // gw, _B_COL["n_vs"] // HEAD_DIM, kv_cmp, gates)
    out_b = nsa_window(pb, s_len, _B_COL["n_q"] // gw, _B_COL["n_kw"] // HEAD_DIM, _B_COL["n_vw"] // HEAD_DIM,
                       gates, u)

    q = mm_fullk(pf, 0, MLA_Q_LORA, _prep_w_uq(w_uq), prologue="rms", p1=q_norm, epilogue="rope",
                 modes=_Q_MODES, ta=tabs["q"][0], tb=tabs["q"][1], out_dtype=BF16)
    kv = mm_fullk(pf, _F_COL["m_ckv"] // MLA_KV_LORA, MLA_KV_LORA, _prep_w_ukv(w_ukv), prologue="rms",
                  p1=kv_norm, out_dtype=BF16)
    k_rope_t = pf[:, _F_COL["m_kr"]:_F_COL["m_kr"] + LANES].astype(BF16).T
    k_nope_t = kv[:, :MLA_HEADS * MLA_NOPE_DIM].T
    out_c = mla_attention(q, kv, k_nope_t, k_rope_t, s_len)
    return jnp.concatenate([out_a, out_b, out_c], axis=1)


def kernel(x, c, positions, w_ada, b_ada, w_in, nsa_cmp_pe, nsa_cmp_w1, nsa_cmp_w2, mla_q_norm, mla_kv_norm,
           mla_w_uq, mla_w_ukv, w_out, ln1_g, ln1_b, mlp_w1, mlp_w2, ln2_g, ln2_b):
    assert x.shape[0] == 1, "kernel handles batch size 1"
    xs = x[0]
    d = xs.shape[1]
    tabs = _rope_tables(positions[0])
    for l in range(DEPTH):
        mod = adaln(c, w_ada[l], b_ada[l])
        sh1, sc1, g1, sh2, sc2, g2 = [mod[:, i * d:(i + 1) * d] for i in range(6)]
        mixed = _mixer(xs, sc1, sh1, tabs, w_in[l], nsa_cmp_pe[l], nsa_cmp_w1[l], nsa_cmp_w2[l],
                       mla_q_norm[l], mla_kv_norm[l], mla_w_uq[l], mla_w_ukv[l])
        xs = mm_ln(mixed, w_out[l].astype(BF16), xs, g1, ln1_g[l], ln1_b[l])
        act = mm_fullk(xs, 0, d, mlp_w1[l].astype(BF16), prologue="mod", p1=sc2, p2=sh2, epilogue="relu2",
                       out_dtype=BF16)
        xs = mm_ln(act, mlp_w2[l].astype(BF16), xs, g2, ln2_g[l], ln2_b[l])
    return xs[None]
```

```python
import functools

import jax
import jax.numpy as jnp
import numpy as np
from jax import lax
from jax.experimental import pallas as pl
from jax.experimental.pallas import tpu as pltpu

D_MODEL = 4096
DEPTH = 2
HEAD_DIM = 128
ROPE_THETA = 10000.0
QBLOCK = 128
DSW_HEADS = 8
DSW_PATTERNS = ((128, 1), (512, 4), (2048, 16))
NSA_HEADS = 8
NSA_KV_HEADS = 2
NSA_GROUP = NSA_HEADS // NSA_KV_HEADS
NSA_CMP_LEN = 32
NSA_CMP_STRIDE = 16
NSA_CMP_HIDDEN = 256
NSA_SEL_BLOCK = 64
NSA_TOP_N = 16
NSA_WINDOW = 512
MLA_HEADS = 16
MLA_Q_LORA = 1536
MLA_KV_LORA = 512
MLA_NOPE_DIM = 128
MLA_ROPE_DIM = 64
MLA_V_DIM = 128
D_FF = 4 * D_MODEL
ALPHA = (2 * DEPTH) ** 0.25

DSW_W = DSW_HEADS * HEAD_DIM
NSA_QW = NSA_HEADS * HEAD_DIM
NSA_KVW = NSA_KV_HEADS * HEAD_DIM

LANES = 128
V7X_VMEM_LIMIT = 56 * 1024 * 1024
V7X_VMEM_LIMIT_LARGE = 60 * 1024 * 1024

F32 = jnp.float32
BF16 = jnp.bfloat16
NEG_INF = float("-inf")
LOG2E = 1.4426950408889634
FLASH_ROWS = 64
FLASH_UNROLL = 4
DIL_SUPER = QBLOCK * max(d for _, d in DSW_PATTERNS)
DIL_MIX_ROWS = 64
NT_DIMS = (((1,), (1,)), ((), ()))


def _params(sem, vmem=V7X_VMEM_LIMIT):
    return pltpu.CompilerParams(dimension_semantics=sem, vmem_limit_bytes=vmem)


def _adaln_body(cb_ref, w_ref, b_ref, o_ref, *, tn):
    cb = cb_ref[...]
    for s in range(tn // LANES):
        sl = slice(s * LANES, (s + 1) * LANES)
        o_ref[:, sl] = jnp.sum(w_ref[:, sl] * cb, axis=0, keepdims=True) + b_ref[:, sl]


def adaln(c, w, b, tn=512):
    d, n = w.shape
    cb = jnp.broadcast_to(c.reshape(d, 1), (d, LANES))
    return pl.pallas_call(
        functools.partial(_adaln_body, tn=tn),
        grid=(n // tn,),
        in_specs=[
            pl.BlockSpec((d, LANES), lambda j: (0, 0)),
            pl.BlockSpec((d, tn), lambda j: (0, j)),
            pl.BlockSpec((1, tn), lambda j: (0, j)),
        ],
        out_specs=pl.BlockSpec((1, tn), lambda j: (0, j)),
        out_shape=jax.ShapeDtypeStruct((1, n), F32),
        compiler_params=_params(("arbitrary",)),
        name="adaln",
    )(cb, w, b.reshape(1, n))


def _mm_fullk_body(*refs, prologue, epilogue, nsub):
    refs = list(refs)
    mode_ref = refs.pop(0) if epilogue == "rope" else None
    x_ref, p1_ref = refs.pop(0), refs.pop(0)
    p2_ref = refs.pop(0) if prologue == "mod" else None
    w_ref = refs.pop(0)
    if epilogue == "rope":
        ta_ref, tb_ref = refs.pop(0), refs.pop(0)
    o_ref, h_scr = refs
    j = pl.program_id(1)

    @pl.when(j == 0)
    def _():
        x = x_ref[...]
        if prologue == "mod":
            h = x * (1.0 + p1_ref[...]) + p2_ref[...]
        else:
            h = x * lax.rsqrt(jnp.mean(x * x, axis=-1, keepdims=True) + 1e-6) * p1_ref[...]
        h_scr[...] = h.astype(BF16)

    acc = jnp.dot(h_scr[...], w_ref[...], preferred_element_type=F32)
    if epilogue == "relu2":
        r = jnp.maximum(acc, 0.0)
        o_ref[...] = (r * r).astype(o_ref.dtype)
    elif epilogue == "rope":
        for s in range(nsub):
            sl = slice(s * LANES, (s + 1) * LANES)
            md = mode_ref[j * nsub + s]
            sub = acc[:, sl]
            o_ref[:, sl] = (sub * ta_ref[md] + pltpu.roll(sub, LANES // 2, 1) * tb_ref[md]).astype(o_ref.dtype)
    else:
        o_ref[...] = acc.astype(o_ref.dtype)


def mm_fullk(x, x_col_block, k, w, *, prologue, p1, p2=None, epilogue="none", modes=None, ta=None, tb=None,
             out_dtype=BF16, tm=512, tn=1024):
    m = x.shape[0]
    n = w.shape[1]
    tm = min(tm, m)
    assert m % tm == 0 and n % tn == 0 and w.shape[0] == k
    nsub = tn // LANES
    rope = epilogue == "rope"
    npre = 1 if rope else 0

    def im(f):
        return (lambda i, j, *_: f(i, j))

    in_specs = [pl.BlockSpec((tm, k), im(lambda i, j: (i, x_col_block))),
                pl.BlockSpec((1, k), im(lambda i, j: (0, 0)))]
    args = [x, p1.reshape(1, k)]
    if prologue == "mod":
        in_specs.append(pl.BlockSpec((1, k), im(lambda i, j: (0, 0))))
        args.append(p2.reshape(1, k))
    in_specs.append(pl.BlockSpec((k, tn), im(lambda i, j: (0, j))))
    args.append(w)
    if rope:
        nmode = ta.shape[0]
        in_specs += [pl.BlockSpec((nmode, tm, LANES), im(lambda i, j: (0, i, 0)))] * 2
        args += [ta, tb]
    grid_spec = pltpu.PrefetchScalarGridSpec(
        num_scalar_prefetch=npre,
        grid=(m // tm, n // tn),
        in_specs=in_specs,
        out_specs=pl.BlockSpec((tm, tn), im(lambda i, j: (i, j))),
        scratch_shapes=[pltpu.VMEM((tm, k), BF16)],
    )
    fn = pl.pallas_call(
        functools.partial(_mm_fullk_body, prologue=prologue, epilogue=epilogue, nsub=nsub),
        grid_spec=grid_spec,
        out_shape=jax.ShapeDtypeStruct((m, n), out_dtype),
        compiler_params=_params(("parallel", "arbitrary")),
        name="mm_fullk_" + prologue + "_" + epilogue,
    )
    if rope:
        return fn(jnp.asarray(modes, jnp.int32), *args)
    return fn(*args)


def _mm_ln_body(lhs_ref, w_ref, x_ref, g_ref, lng_ref, lnb_ref, o_ref, acc_ref, *, nk):
    kk = pl.program_id(1)

    @pl.when(kk == 0)
    def _():
        acc_ref[...] = jnp.dot(lhs_ref[...], w_ref[...], preferred_element_type=F32)

    @pl.when(kk > 0)
    def _():
        acc_ref[...] += jnp.dot(lhs_ref[...], w_ref[...], preferred_element_type=F32)

    @pl.when(kk == nk - 1)
    def _():
        z = ALPHA * x_ref[...] + (1.0 + g_ref[...]) * acc_ref[...]
        mu = jnp.mean(z, axis=-1, keepdims=True)
        zc = z - mu
        var = jnp.mean(zc * zc, axis=-1, keepdims=True)
        o_ref[...] = zc * lax.rsqrt(var + 1e-5) * lng_ref[...] + lnb_ref[...]


def mm_ln(lhs, w, x, gate, ln_g, ln_b, tm=512, tk=512):
    m, k = lhs.shape
    n = w.shape[1]
    tm = min(tm, m)
    nk = k // tk
    row = lambda i, kk: (0, 0)
    return pl.pallas_call(
        functools.partial(_mm_ln_body, nk=nk),
        grid=(m // tm, nk),
        in_specs=[
            pl.BlockSpec((tm, tk), lambda i, kk: (i, kk)),
            pl.BlockSpec((tk, n), lambda i, kk: (kk, 0)),
            pl.BlockSpec((tm, n), lambda i, kk: (i, 0)),
            pl.BlockSpec((1, n), row), pl.BlockSpec((1, n), row), pl.BlockSpec((1, n), row),
        ],
        out_specs=pl.BlockSpec((tm, n), lambda i, kk: (i, 0)),
        out_shape=jax.ShapeDtypeStruct((m, n), F32),
        scratch_shapes=[pltpu.VMEM((tm, n), F32)],
        compiler_params=_params(("parallel", "arbitrary"), V7X_VMEM_LIMIT_LARGE),
        name="mm_ln",
    )(lhs, w, x, gate.reshape(1, n), ln_g.reshape(1, n), ln_b.reshape(1, n))


def _dilated_body(q_ref, kc_ref, kp_ref, vc_ref, vp_ref, o_ref, q32, kc32, kp32, vc32, vp32, od_scr, ld_scr):
    n = pl.program_id(1)
    q32[...] = q_ref[...].astype(F32)
    kc32[...] = kc_ref[...].astype(F32)
    kp32[...] = kp_ref[...].astype(F32)
    vc32[...] = vc_ref[...].astype(F32)
    vp32[...] = vp_ref[...].astype(F32)
    qi = lax.broadcasted_iota(jnp.int32, (QBLOCK, 2 * QBLOCK), 0)
    ki = lax.broadcasted_iota(jnp.int32, (QBLOCK, 2 * QBLOCK), 1)
    delta = QBLOCK + qi - ki
    in_band = (delta >= 0) & (delta <= QBLOCK)
    in_band_first = in_band & ((ki >= QBLOCK) | (n > 0))

    for pi, (window, dil) in enumerate(DSW_PATTERNS):
        assert window // dil == QBLOCK
        nblk = DIL_SUPER // (QBLOCK * dil)
        for mb in range(nblk):
            for r in range(dil):
                def rows(b):
                    return pl.ds(b * QBLOCK * dil + r, QBLOCK, stride=dil) if dil > 1 else pl.ds(b * QBLOCK, QBLOCK)

                cur = rows(mb)
                if mb > 0:
                    k_prev, v_prev, valid = kc32[rows(mb - 1), :], vc32[rows(mb - 1), :], in_band
                else:
                    k_prev, v_prev, valid = kp32[rows(nblk - 1), :], vp32[rows(nblk - 1), :], in_band_first
                k = jnp.concatenate([k_prev, kc32[cur, :]], axis=0).astype(BF16)
                v = jnp.concatenate([v_prev, vc32[cur, :]], axis=0).astype(BF16)
                s = lax.dot_general(q32[cur, :].astype(BF16), k, NT_DIMS, preferred_element_type=F32)
                s = jnp.where(valid, s, NEG_INF)
                m = jnp.max(s, axis=-1, keepdims=True)
                p = jnp.exp2(s - m)
                l = jnp.sum(p, axis=-1, keepdims=True)
                od_scr[pi, cur, :] = jnp.dot(p.astype(BF16), v, preferred_element_type=F32) / l
                ld_scr[pi, cur, :] = jnp.broadcast_to(m + jnp.log2(l), (QBLOCK, HEAD_DIM))

    npat = len(DSW_PATTERNS)
    for c0 in range(0, DIL_SUPER, DIL_MIX_ROWS):
        rs = slice(c0, c0 + DIL_MIX_ROWS)
        ls = [ld_scr[i, rs, :] for i in range(npat)]
        mx = functools.reduce(jnp.maximum, ls)
        ws = [jnp.exp2(t - mx) for t in ls]
        num = sum(ws[i] * od_scr[i, rs, :] for i in range(npat))
        o_ref[rs, :] = (num / sum(ws)).astype(o_ref.dtype)


def dilated_attention(qkv, s_len, q_blk, k_blk, v_blk):
    assert s_len % DIL_SUPER == 0
    cur = lambda blk: (lambda h, n: (n, blk + h))
    prev = lambda blk: (lambda h, n: (jnp.maximum(n - 1, 0), blk + h))
    bs = lambda f: pl.BlockSpec((DIL_SUPER, HEAD_DIM), f)
    f32_rows = pltpu.VMEM((DIL_SUPER, HEAD_DIM), F32)
    per_pattern = pltpu.VMEM((len(DSW_PATTERNS), DIL_SUPER, HEAD_DIM), F32)
    return pl.pallas_call(
        _dilated_body,
        grid=(DSW_HEADS, s_len // DIL_SUPER),
        in_specs=[bs(cur(q_blk)), bs(cur(k_blk)), bs(prev(k_blk)), bs(cur(v_blk)), bs(prev(v_blk))],
        out_specs=pl.BlockSpec((DIL_SUPER, HEAD_DIM), lambda h, n: (n, h)),
        out_shape=jax.ShapeDtypeStruct((s_len, DSW_W), BF16),
        scratch_shapes=[f32_rows] * 5 + [per_pattern] * 2,
        compiler_params=_params(("parallel", "parallel")),
        name="dilated_attention",
    )(qkv, qkv, qkv, qkv, qkv)


def _cmp_body(x_ref, pe_ref, w1_ref, w2_ref, o_ref, *, ncp):
    x = x_ref[...]
    half = NSA_CMP_STRIDE * HEAD_DIM
    a = jnp.dot((x + pe_ref[0:1, :]).astype(BF16), w1_ref[0:half, :], preferred_element_type=F32)
    b = jnp.dot((x + pe_ref[1:2, :]).astype(BF16), w1_ref[half:2 * half, :], preferred_element_type=F32)
    hid = a + pltpu.roll(b, ncp - 1, 0)
    act = jax.nn.gelu(hid)
    o_ref[...] = jnp.dot(act.astype(BF16), w2_ref[...], preferred_element_type=F32).astype(o_ref.dtype)


def nsa_compress(xs, pe, w1, w2):
    ncp = xs.shape[2]
    half = NSA_CMP_STRIDE * HEAD_DIM
    return pl.pallas_call(
        functools.partial(_cmp_body, ncp=ncp),
        grid=(2, NSA_KV_HEADS),
        in_specs=[
            pl.BlockSpec((None, None, ncp, half), lambda a, h: (a, h, 0, 0)),
            pl.BlockSpec((None, 2, half), lambda a, h: (a, 0, 0)),
            pl.BlockSpec((None, 2 * half, NSA_CMP_HIDDEN), lambda a, h: (a, 0, 0)),
            pl.BlockSpec((None, NSA_CMP_HIDDEN, HEAD_DIM), lambda a, h: (a, 0, 0)),
        ],
        out_specs=pl.BlockSpec((None, None, ncp, HEAD_DIM), lambda a, h: (a, h, 0, 0)),
        out_shape=jax.ShapeDtypeStruct((2, NSA_KV_HEADS, ncp, HEAD_DIM), BF16),
        compiler_params=_params(("parallel", "parallel")),
        name="nsa_compress",
    )(xs, pe.reshape(2, 2, half), w1, w2)


def _stack_heads(q):
    return jnp.concatenate([q[:, g * HEAD_DIM:(g + 1) * HEAD_DIM] for g in range(NSA_GROUP)], axis=0)


def _nsa_body(q_ref, kst_ref, vs_ref, kc_ref, vc_ref, covt_ref, e0_ref, gate_ref, u_ref,
              sel_scr, q4_scr, mk_scr, ve_scr, s_scr, p_scr, a_scr, m_scr, acc_scr,
              *, ncp, nselp, ntop, tk):
    n = pl.program_id(1)
    rows = NSA_GROUP * QBLOCK
    q4_scr[...] = _stack_heads(q_ref[...])
    q4 = q4_scr[...]

    s = lax.dot_general(q4, kc_ref[...], NT_DIMS, preferred_element_type=F32)
    qpos_r = n * QBLOCK + (lax.broadcasted_iota(jnp.int32, (rows, ncp), 0) & (QBLOCK - 1))
    cidx = lax.broadcasted_iota(jnp.int32, (rows, ncp), 1)
    s = jnp.where(cidx * NSA_CMP_STRIDE + (NSA_CMP_LEN - 1) <= qpos_r, s, NEG_INF)
    m = jnp.max(s, axis=-1, keepdims=True)
    e = jnp.exp2(s - jnp.where(m == NEG_INF, 0.0, m))
    l = jnp.sum(e, axis=-1, keepdims=True)
    p = e / jnp.where(l > 0, l, 1.0)
    o_c = jnp.dot(p.astype(BF16), vc_ref[...], preferred_element_type=F32)

    psum = p[0:QBLOCK]
    for g in range(1, NSA_GROUP):
        psum = psum + p[g * QBLOCK:(g + 1) * QBLOCK]
    p_hi = psum.astype(BF16)
    p_lo = (psum - p_hi.astype(F32)).astype(BF16)
    covt = covt_ref[...]
    imp_t = (lax.dot_general(covt, p_hi, NT_DIMS, preferred_element_type=F32)
             + lax.dot_general(covt, p_lo, NT_DIMS, preferred_element_type=F32))

    jblk = lax.broadcasted_iota(jnp.int32, (nselp, QBLOCK), 0)
    qpos_c = n * QBLOCK + lax.broadcasted_iota(jnp.int32, (nselp, QBLOCK), 1)
    cur = qpos_c // NSA_SEL_BLOCK
    forced = (jblk == 0) | (jblk == cur) | (jblk == cur - 1)
    valid = jblk * NSA_SEL_BLOCK <= qpos_c
    score0 = jnp.where(valid, jnp.where(forced, jnp.inf, imp_t), NEG_INF)

    def pick(_, carry):
        score, picked = carry
        mx = jnp.max(score, axis=0, keepdims=True)
        idx = jnp.min(jnp.where(score == mx, jblk, nselp), axis=0, keepdims=True)
        hit = jblk == idx
        return jnp.where(hit, NEG_INF, score), jnp.where(hit, 1.0, picked)

    _, picked = lax.fori_loop(0, ntop, pick, (score0, jnp.zeros((nselp, QBLOCK), F32)))
    sel_scr[...] = jnp.where(valid, picked, 0.0).T

    blocks_per_tile = tk // NSA_SEL_BLOCK
    _flash_init(m_scr, acc_scr, ve_scr)

    def k_tile(t, slot, buf):
        shift = (nselp - t * blocks_per_tile) % nselp
        sel_t = pltpu.roll(sel_scr[...], shift, 1)[:, 0:LANES].astype(BF16)
        mk_scr[slot] = jnp.dot(sel_t, e0_ref[...], preferred_element_type=F32)
        return kst_ref[:, pl.ds(pl.multiple_of(t * tk, tk), tk)]

    def v_tile(t, buf):
        ve_scr[buf, :, 0:HEAD_DIM] = vs_ref[pl.ds(pl.multiple_of(t * tk, tk), tk), :]
        return ve_scr[buf]

    def mask_for(slot, t, masked):
        def mask_fn(r, s):
            rq = r % QBLOCK
            ok = mk_scr[slot, rq:rq + FLASH_ROWS, :] > 0.5
            if masked:
                qpos = n * QBLOCK + rq + lax.broadcasted_iota(jnp.int32, s.shape, 0)
                ok = ok & (t * tk + lax.broadcasted_iota(jnp.int32, s.shape, 1) <= qpos)
            return jnp.where(ok, s, NEG_INF)
        return mask_fn

    _flash_pipeline((n * QBLOCK + QBLOCK - 1) // tk, q4_scr, k_tile, v_tile, mask_for,
                    s_scr, p_scr, a_scr, m_scr, acc_scr)
    o_s = _flash_output(acc_scr)

    gates = jax.nn.sigmoid(gate_ref[...])
    for g in range(NSA_GROUP):
        rs = slice(g * QBLOCK, (g + 1) * QBLOCK)
        u_ref[:, g * HEAD_DIM:(g + 1) * HEAD_DIM] = (gates[:, 3 * g:3 * g + 1] * o_c[rs]
                                                     + gates[:, 3 * g + 1:3 * g + 2] * o_s[rs])


def nsa_cmp_sel(qkv, ks_t, s_len, q_blk, vs_blk, kv_cmp, gates, tk=512):
    nb = s_len // QBLOCK
    ncp = s_len // NSA_CMP_STRIDE
    nsel = s_len // NSA_SEL_BLOCK
    nselp = -(-nsel // LANES) * LANES
    ntop = min(NSA_TOP_N, nsel)
    ci = np.arange(ncp)[None, :] * NSA_CMP_STRIDE
    sj = np.arange(nselp)[:, None] * NSA_SEL_BLOCK
    cov = (ci < sj + NSA_SEL_BLOCK) & (ci + NSA_CMP_LEN > sj) & (np.arange(ncp)[None, :] < ncp - 1) & (sj < s_len)
    covt = jnp.asarray(cov.astype(np.float32), BF16)
    e0 = jnp.asarray((np.arange(tk)[None, :] // NSA_SEL_BLOCK == np.arange(LANES)[:, None]).astype(np.float32), BF16)
    gw = NSA_GROUP * HEAD_DIM
    rows = NSA_GROUP * QBLOCK
    return pl.pallas_call(
        functools.partial(_nsa_body, ncp=ncp, nselp=nselp, ntop=ntop, tk=tk),
        grid=(NSA_KV_HEADS, nb),
        in_specs=[
            pl.BlockSpec((QBLOCK, gw), lambda h, n: (n, q_blk + h)),
            pl.BlockSpec((HEAD_DIM, s_len), lambda h, n: (h, 0)),
            pl.BlockSpec((s_len, HEAD_DIM), lambda h, n: (0, vs_blk + h)),
            pl.BlockSpec((None, None, ncp, HEAD_DIM), lambda h, n: (0, h, 0, 0)),
            pl.BlockSpec((None, None, ncp, HEAD_DIM), lambda h, n: (1, h, 0, 0)),
            pl.BlockSpec((nselp, ncp), lambda h, n: (0, 0)),
            pl.BlockSpec((LANES, tk), lambda h, n: (0, 0)),
            pl.BlockSpec((None, QBLOCK, 3 * NSA_GROUP), lambda h, n: (h, n, 0)),
        ],
        out_specs=pl.BlockSpec((QBLOCK, gw), lambda h, n: (n, h)),
        out_shape=jax.ShapeDtypeStruct((s_len, NSA_QW), F32),
        scratch_shapes=[pltpu.VMEM((QBLOCK, nselp), F32), pltpu.VMEM((rows, HEAD_DIM), BF16),
                        pltpu.VMEM((FLASH_UNROLL, QBLOCK, tk), F32), pltpu.VMEM((FLASH_UNROLL + 1, tk, 2 * HEAD_DIM), BF16),
                        pltpu.VMEM((FLASH_UNROLL, rows, tk), F32), pltpu.VMEM((FLASH_UNROLL, rows, tk), BF16),
                        pltpu.VMEM((FLASH_UNROLL, rows, LANES), F32), pltpu.VMEM((rows, LANES), F32),
                        pltpu.VMEM((rows, 2 * HEAD_DIM), F32)],
        compiler_params=_params(("parallel", "arbitrary")),
        name="nsa_cmp_sel",
    )(qkv, ks_t, qkv, kv_cmp, kv_cmp, covt, e0, gates)


def _win_body(q_ref, kw_ref, vw_ref, gate_ref, u_ref, o_ref, *, span):
    n = pl.program_id(1)
    rows = NSA_GROUP * QBLOCK
    q4 = _stack_heads(q_ref[...])
    start = pl.multiple_of(jnp.maximum(n * QBLOCK - (span - QBLOCK), 0), QBLOCK)
    s = lax.dot_general(q4, kw_ref[pl.ds(start, span), :], NT_DIMS, preferred_element_type=F32)
    qpos = n * QBLOCK + (lax.broadcasted_iota(jnp.int32, (rows, span), 0) & (QBLOCK - 1))
    delta = qpos - (start + lax.broadcasted_iota(jnp.int32, (rows, span), 1))
    s = jnp.where((delta >= 0) & (delta <= NSA_WINDOW - 1), s, NEG_INF)
    m = jnp.max(s, axis=-1, keepdims=True)
    p = jnp.exp2(s - m)
    l = jnp.sum(p, axis=-1, keepdims=True)
    o_w = jnp.dot(p.astype(BF16), vw_ref[pl.ds(start, span), :], preferred_element_type=F32) / l
    gates = jax.nn.sigmoid(gate_ref[...])
    for g in range(NSA_GROUP):
        sl = slice(g * HEAD_DIM, (g + 1) * HEAD_DIM)
        o_ref[:, sl] = (u_ref[:, sl] + gates[:, 3 * g + 2:3 * g + 3] * o_w[g * QBLOCK:(g + 1) * QBLOCK]
                        ).astype(o_ref.dtype)


def nsa_window(qkv, s_len, q_blk, kw_blk, vw_blk, gates, u):
    nb = s_len // QBLOCK
    span = (-(-(NSA_WINDOW - 1) // QBLOCK) + 1) * QBLOCK
    gw = NSA_GROUP * HEAD_DIM
    return pl.pallas_call(
        functools.partial(_win_body, span=span),
        grid=(NSA_KV_HEADS, nb),
        in_specs=[
            pl.BlockSpec((QBLOCK, gw), lambda h, n: (n, q_blk + h)),
            pl.BlockSpec((s_len, HEAD_DIM), lambda h, n: (0, kw_blk + h)),
            pl.BlockSpec((s_len, HEAD_DIM), lambda h, n: (0, vw_blk + h)),
            pl.BlockSpec((None, QBLOCK, 3 * NSA_GROUP), lambda h, n: (h, n, 0)),
            pl.BlockSpec((QBLOCK, gw), lambda h, n: (n, h)),
        ],
        out_specs=pl.BlockSpec((QBLOCK, gw), lambda h, n: (n, h)),
        out_shape=jax.ShapeDtypeStruct((s_len, NSA_QW), BF16),
        compiler_params=_params(("parallel", "parallel")),
        name="nsa_window",
    )(qkv, qkv, qkv, gates, u)


def _softmax_rows(s_ref, p_ref, a_ref, m_scr, mask_fn):
    rows, tk = s_ref.shape
    nrep = tk // LANES
    for r in range(0, rows, FLASH_ROWS):
        rs = slice(r, r + FLASH_ROWS)
        s = s_ref[rs, :]
        if mask_fn is not None:
            s = mask_fn(r, s)
        m_old = m_scr[rs, :]
        m_new = jnp.maximum(m_old, jnp.max(s, axis=1, keepdims=True))
        p_ref[rs, :] = jnp.exp2(s - jnp.concatenate([m_new] * nrep, axis=1)).astype(BF16)
        a_ref[rs, :] = jnp.exp2(m_old - m_new)
        m_scr[rs, :] = m_new


def _flash_pipeline(n_full, q_ref, k_tile, v_tile, mask_for, s_scr, p_scr, a_scr, m_scr, acc_scr):
    def scores(t, idx, buf):
        s_scr[idx] = jnp.dot(q_ref[...], k_tile(t, idx, buf), preferred_element_type=F32)

    def softmax(idx, t, masked):
        _softmax_rows(s_scr.at[idx], p_scr.at[idx], a_scr.at[idx], m_scr, mask_for(idx, t, masked))

    def values(idx, t, buf):
        a = a_scr[idx]
        acc_scr[...] = (jnp.concatenate([a, a], axis=1) * acc_scr[...]
                        + jnp.dot(p_scr[idx], v_tile(t, buf), preferred_element_type=F32))

    last = FLASH_UNROLL - 1
    p_scr[last] = jnp.zeros(p_scr.shape[1:], p_scr.dtype)
    a_scr[last] = jnp.ones(a_scr.shape[1:], a_scr.dtype)
    scores(0, 0, 0)

    def run(t0, count, masked_last, lookahead):
        for j in range(count):
            t = t0 + j
            if lookahead or j + 1 < count:
                scores(t + 1, (j + 1) % FLASH_UNROLL, j)
            softmax(j, t, masked_last and j == count - 1)
            values((j - 1) % FLASH_UNROLL, jnp.maximum(t - 1, 0), j)
        if not lookahead:
            values(count - 1, t0 + count - 1, count)

    def body(u, carry):
        run(FLASH_UNROLL * u, FLASH_UNROLL, False, True)
        return carry

    lax.fori_loop(0, n_full // FLASH_UNROLL, body, 0)
    rem = n_full % FLASH_UNROLL
    for r in range(FLASH_UNROLL):
        @pl.when(rem == r)
        def _(r=r):
            run(n_full - r, r + 1, True, False)


def _flash_init(m_scr, acc_scr, ve_scr):
    dv = ve_scr.shape[2] // 2
    m_scr[...] = jnp.full(m_scr.shape, NEG_INF, F32)
    acc_scr[...] = jnp.zeros(acc_scr.shape, F32)
    ve_scr[:, :, dv:] = jnp.ones((ve_scr.shape[0], ve_scr.shape[1], dv), ve_scr.dtype)


def _flash_output(acc_scr):
    dv = acc_scr.shape[1] // 2
    return acc_scr[:, 0:dv] / acc_scr[:, dv:]


def _mla_body(q_ref, knt_ref, krt_ref, v_ref, o_ref, kt_scr, ve_scr, s_scr, p_scr, a_scr, m_scr, acc_scr, *, tq):
    qi = pl.program_id(1)
    _flash_init(m_scr, acc_scr, ve_scr)

    def diag_mask(r, s):
        row = r + lax.broadcasted_iota(jnp.int32, s.shape, 0)
        col = lax.broadcasted_iota(jnp.int32, s.shape, 1)
        return jnp.where(col <= row, s, NEG_INF)

    def k_tile(t, slot, buf):
        k0 = pl.multiple_of(t * tq, tq)
        kt_scr[buf, 0:MLA_NOPE_DIM, :] = knt_ref[:, pl.ds(k0, tq)]
        kt_scr[buf, MLA_NOPE_DIM:, :] = krt_ref[:, pl.ds(k0, tq)]
        return kt_scr[buf]

    def v_tile(t, buf):
        ve_scr[buf, :, 0:MLA_V_DIM] = v_ref[pl.ds(pl.multiple_of(t * tq, tq), tq), :]
        return ve_scr[buf]

    _flash_pipeline(qi, q_ref, k_tile, v_tile, lambda slot, t, masked: diag_mask if masked else None,
                    s_scr, p_scr, a_scr, m_scr, acc_scr)
    o_ref[...] = _flash_output(acc_scr).astype(o_ref.dtype)


def mla_attention(q, kv, k_nope_t, k_rope_t, s_len, tq=512):
    tq = min(tq, s_len)
    qw = 2 * LANES
    return pl.pallas_call(
        functools.partial(_mla_body, tq=tq),
        grid=(MLA_HEADS, s_len // tq),
        in_specs=[
            pl.BlockSpec((tq, qw), lambda h, i: (i, h)),
            pl.BlockSpec((MLA_NOPE_DIM, s_len), lambda h, i: (h, 0)),
            pl.BlockSpec((LANES, s_len), lambda h, i: (0, 0)),
            pl.BlockSpec((s_len, MLA_V_DIM), lambda h, i: (0, MLA_HEADS + h)),
        ],
        out_specs=pl.BlockSpec((tq, MLA_V_DIM), lambda h, i: (i, h)),
        out_shape=jax.ShapeDtypeStruct((s_len, MLA_HEADS * MLA_V_DIM), BF16),
        scratch_shapes=[pltpu.VMEM((FLASH_UNROLL, qw, tq), BF16),
                        pltpu.VMEM((FLASH_UNROLL + 1, tq, 2 * MLA_V_DIM), BF16),
                        pltpu.VMEM((FLASH_UNROLL, tq, tq), F32), pltpu.VMEM((FLASH_UNROLL, tq, tq), BF16),
                        pltpu.VMEM((FLASH_UNROLL, tq, LANES), F32), pltpu.VMEM((tq, LANES), F32),
                        pltpu.VMEM((tq, 2 * MLA_V_DIM), F32)],
        compiler_params=_params(("parallel", "parallel")),
        name="mla_attention",
    )(q, k_nope_t, k_rope_t, kv)


_IN_SIZES = (DSW_W, DSW_W, DSW_W, NSA_QW, NSA_KVW, NSA_KVW, NSA_KVW, NSA_KVW, NSA_KVW, NSA_KVW,
             3 * NSA_HEADS, MLA_Q_LORA, MLA_KV_LORA, MLA_ROPE_DIM)
_IN_NAMES = ("a_q", "a_k", "a_v", "n_q", "n_kc", "n_vc", "n_ks", "n_vs", "n_kw", "n_vw", "n_gate",
             "m_cq", "m_ckv", "m_kr")
_IN_OFF = dict(zip(_IN_NAMES, np.concatenate([[0], np.cumsum(_IN_SIZES)[:-1]]).tolist()))
_IN_LEN = dict(zip(_IN_NAMES, _IN_SIZES))

_B_ORDER = ("a_q", "a_k", "n_q", "n_ks", "n_kw", "n_vs", "n_vw", "a_v")
_B_ROPE = 2 * DSW_W + NSA_QW + 2 * NSA_KVW
_B_WIDTH = sum(_IN_LEN[k] for k in _B_ORDER)
_B_COL = dict(zip(_B_ORDER, np.concatenate([[0], np.cumsum([_IN_LEN[k] for k in _B_ORDER])[:-1]]).tolist()))
_F_COL = {"m_cq": 0, "m_ckv": MLA_Q_LORA, "n_kc": 2048, "n_vc": 2304, "m_kr": 2560, "n_gate": 2688}
_F_WIDTH = 3072
_HALF_ROPE = MLA_ROPE_DIM // 2


def _spread_rope_cols(w):
    z = jnp.zeros((w.shape[0], _HALF_ROPE), w.dtype)
    return jnp.concatenate([w[:, :_HALF_ROPE], z, w[:, _HALF_ROPE:], z], axis=1)


def _prep_w_in(w):
    col = lambda name: w[:, _IN_OFF[name]:_IN_OFF[name] + _IN_LEN[name]]
    wb = jnp.concatenate([col(k) for k in _B_ORDER], axis=1).astype(BF16)
    used = _F_COL["n_gate"] + _IN_LEN["n_gate"]
    wf = jnp.concatenate([col("m_cq"), col("m_ckv"), col("n_kc"), col("n_vc"), _spread_rope_cols(col("m_kr")),
                          col("n_gate"), jnp.zeros((w.shape[0], _F_WIDTH - used), w.dtype)], axis=1).astype(BF16)
    return wb, wf


_B_MODES = ([2] * (DSW_W // LANES) + [1] * (DSW_W // LANES) + [2] * (NSA_QW // LANES)
            + [1] * (2 * NSA_KVW // LANES) + [0] * ((_B_WIDTH - _B_ROPE) // LANES))
_F_MODES = [0] * 16 + [1, 1, 0, 0, 2] + [0] * 3
_Q_MODES = [0, 1] * MLA_HEADS
C_HEAD = HEAD_DIM ** -0.5 * LOG2E
C_MLA = (MLA_NOPE_DIM + MLA_ROPE_DIM) ** -0.5 * LOG2E


def _prep_w_uq(w):
    w = w.reshape(MLA_Q_LORA, MLA_HEADS, MLA_NOPE_DIM + MLA_ROPE_DIM)
    z = jnp.zeros((MLA_Q_LORA, MLA_HEADS, _HALF_ROPE), w.dtype)
    w = jnp.concatenate([w[..., :MLA_NOPE_DIM], w[..., MLA_NOPE_DIM:MLA_NOPE_DIM + _HALF_ROPE], z,
                         w[..., MLA_NOPE_DIM + _HALF_ROPE:], z], axis=-1)
    return w.reshape(MLA_Q_LORA, MLA_HEADS * 2 * LANES).astype(BF16)


def _prep_w_ukv(w):
    w = w.reshape(MLA_KV_LORA, MLA_HEADS, 2, MLA_NOPE_DIM).transpose(0, 2, 1, 3)
    return w.reshape(MLA_KV_LORA, 2 * MLA_HEADS * MLA_NOPE_DIM).astype(BF16)


def _rope_tables(positions):
    pos = positions.astype(F32)[:, None]

    def cs(dim):
        inv = ROPE_THETA ** (-jnp.arange(0, dim, 2, dtype=F32) / dim)
        ang = pos * inv
        return jnp.cos(ang), jnp.sin(ang)

    c128, s128 = cs(HEAD_DIM)
    c64, s64 = cs(MLA_ROPE_DIM)
    z = jnp.zeros_like(c64)
    one = jnp.ones((pos.shape[0], LANES), F32)
    zero = jnp.zeros((pos.shape[0], LANES), F32)
    a128, b128 = jnp.concatenate([c128, c128], 1), jnp.concatenate([-s128, s128], 1)
    a64, b64 = jnp.concatenate([c64, z, c64, z], 1), jnp.concatenate([-s64, z, s64, z], 1)
    return {"b": (jnp.stack([one, a128, a128 * C_HEAD]), jnp.stack([zero, b128, b128 * C_HEAD])),
            "f": (jnp.stack([one, a128, a64]), jnp.stack([zero, b128, b64])),
            "q": (jnp.stack([one * C_MLA, a64 * C_MLA]), jnp.stack([zero, b64 * C_MLA]))}


def _mixer(x, sc1, sh1, tabs, w_in, cmp_pe, cmp_w1, cmp_w2, q_norm, kv_norm, w_uq, w_ukv):
    s_len = x.shape[0]
    wb, wf = _prep_w_in(w_in)
    pb = mm_fullk(x, 0, D_MODEL, wb, prologue="mod", p1=sc1, p2=sh1, epilogue="rope", modes=_B_MODES,
                  ta=tabs["b"][0], tb=tabs["b"][1], out_dtype=BF16)
    pf = mm_fullk(x, 0, D_MODEL, wf, prologue="mod", p1=sc1, p2=sh1, epilogue="rope", modes=_F_MODES,
                  ta=tabs["f"][0], tb=tabs["f"][1], out_dtype=F32)

    blk = lambda name: _B_COL[name] // HEAD_DIM
    out_a = dilated_attention(pb, s_len, blk("a_q"), blk("a_k"), blk("a_v"))

    ncp = s_len // NSA_CMP_STRIDE

    def blocks16(name):
        t = pf[:, _F_COL[name]:_F_COL[name] + NSA_KVW]
        return t.reshape(ncp, NSA_CMP_STRIDE, NSA_KV_HEADS, HEAD_DIM).transpose(2, 0, 1, 3).reshape(
            NSA_KV_HEADS, ncp, NSA_CMP_STRIDE * HEAD_DIM)

    kv_cmp = nsa_compress(jnp.stack([blocks16("n_kc"), blocks16("n_vc")]), cmp_pe,
                          cmp_w1.astype(BF16), cmp_w2.astype(BF16))
    gates = pf[:, _F_COL["n_gate"]:_F_COL["n_gate"] + 3 * NSA_HEADS]
    gates = gates.reshape(s_len, NSA_KV_HEADS, 3 * NSA_GROUP).transpose(1, 0, 2)
    gw = NSA_GROUP * HEAD_DIM
    ks_t = pb[:, _B_COL["n_ks"]:_B_COL["n_ks"] + NSA_KVW].T
    u = nsa_cmp_sel(pb, ks_t, s_len, _B_COL["n_q"] // gw, _B_COL["n_vs"] // HEAD_DIM, kv_cmp, gates)
    out_b = nsa_window(pb, s_len, _B_COL["n_q"] // gw, _B_COL["n_kw"] // HEAD_DIM, _B_COL["n_vw"] // HEAD_DIM,
                       gates, u)

    q = mm_fullk(pf, 0, MLA_Q_LORA, _prep_w_uq(w_uq), prologue="rms", p1=q_norm, epilogue="rope",
                 modes=_Q_MODES, ta=tabs["q"][0], tb=tabs["q"][1], out_dtype=BF16)
    kv = mm_fullk(pf, _F_COL["m_ckv"] // MLA_KV_LORA, MLA_KV_LORA, _prep_w_ukv(w_ukv), prologue="rms",
                  p1=kv_norm, out_dtype=BF16)
    k_rope_t = pf[:, _F_COL["m_kr"]:_F_COL["m_kr"] + LANES].astype(BF16).T
    k_nope_t = kv[:, :MLA_HEADS * MLA_NOPE_DIM].T
    out_c = mla_attention(q, kv, k_nope_t, k_rope_t, s_len)
    return jnp.concatenate([out_a, out_b, out_c], axis=1)


def kernel(x, c, positions, w_ada, b_ada, w_in, nsa_cmp_pe, nsa_cmp_w1, nsa_cmp_w2, mla_q_norm, mla_kv_norm,
           mla_w_uq, mla_w_ukv, w_out, ln1_g, ln1_b, mlp_w1, mlp_w2, ln2_g, ln2_b):
    assert x.shape[0] == 1, "kernel handles batch size 1"
    xs = x[0]
    d = xs.shape[1]
    tabs = _rope_tables(positions[0])
    for l in range(DEPTH):
        mod = adaln(c, w_ada[l], b_ada[l])
        sh1, sc1, g1, sh2, sc2, g2 = [mod[:, i * d:(i + 1) * d] for i in range(6)]
        mixed = _mixer(xs, sc1, sh1, tabs, w_in[l], nsa_cmp_pe[l], nsa_cmp_w1[l], nsa_cmp_w2[l],
                       mla_q_norm[l], mla_kv_norm[l], mla_w_uq[l], mla_w_ukv[l])
        xs = mm_ln(mixed, w_out[l].astype(BF16), xs, g1, ln1_g[l], ln1_b[l])
        act = mm_fullk(xs, 0, d, mlp_w1[l].astype(BF16), prologue="mod", p1=sc2, p2=sh2, epilogue="relu2",
                       out_dtype=BF16)
        xs = mm_ln(act, mlp_w2[l].astype(BF16), xs, g2, ln2_g[l], ln2_b[l])
    return xs[None]
```

```python
import functools

import jax
import jax.numpy as jnp
import numpy as np
from jax import lax
from jax.experimental import pallas as pl
from jax.experimental.pallas import tpu as pltpu

D_MODEL = 4096
DEPTH = 2
HEAD_DIM = 128
ROPE_THETA = 10000.0
QBLOCK = 128
DSW_HEADS = 8
DSW_PATTERNS = ((128, 1), (512, 4), (2048, 16))
NSA_HEADS = 8
NSA_KV_HEADS = 2
NSA_GROUP = NSA_HEADS // NSA_KV_HEADS
NSA_CMP_LEN = 32
NSA_CMP_STRIDE = 16
NSA_CMP_HIDDEN = 256
NSA_SEL_BLOCK = 64
NSA_TOP_N = 16
NSA_WINDOW = 512
MLA_HEADS = 16
MLA_Q_LORA = 1536
MLA_KV_LORA = 512
MLA_NOPE_DIM = 128
MLA_ROPE_DIM = 64
MLA_V_DIM = 128
D_FF = 4 * D_MODEL
ALPHA = (2 * DEPTH) ** 0.25

DSW_W = DSW_HEADS * HEAD_DIM
NSA_QW = NSA_HEADS * HEAD_DIM
NSA_KVW = NSA_KV_HEADS * HEAD_DIM

LANES = 128
V7X_VMEM_LIMIT = 56 * 1024 * 1024
V7X_VMEM_LIMIT_LARGE = 60 * 1024 * 1024

F32 = jnp.float32
BF16 = jnp.bfloat16
NEG_INF = float("-inf")
LOG2E = 1.4426950408889634
FLASH_CHUNK = 64 * 512
FLASH_UNROLL = 4
DIL_SUPER = QBLOCK * max(d for _, d in DSW_PATTERNS)
DIL_MIX_ROWS = 64
NT_DIMS = (((1,), (1,)), ((), ()))


def _params(sem, vmem=V7X_VMEM_LIMIT):
    return pltpu.CompilerParams(dimension_semantics=sem, vmem_limit_bytes=vmem)


def _adaln_body(cb_ref, w_ref, b_ref, o_ref, *, tn):
    cb = cb_ref[...]
    for s in range(tn // LANES):
        sl = slice(s * LANES, (s + 1) * LANES)
        o_ref[:, sl] = jnp.sum(w_ref[:, sl] * cb, axis=0, keepdims=True) + b_ref[:, sl]


def adaln(c, w, b, tn=512):
    d, n = w.shape
    cb = jnp.broadcast_to(c.reshape(d, 1), (d, LANES))
    return pl.pallas_call(
        functools.partial(_adaln_body, tn=tn),
        grid=(n // tn,),
        in_specs=[
            pl.BlockSpec((d, LANES), lambda j: (0, 0)),
            pl.BlockSpec((d, tn), lambda j: (0, j)),
            pl.BlockSpec((1, tn), lambda j: (0, j)),
        ],
        out_specs=pl.BlockSpec((1, tn), lambda j: (0, j)),
        out_shape=jax.ShapeDtypeStruct((1, n), F32),
        compiler_params=_params(("arbitrary",)),
        name="adaln",
    )(cb, w, b.reshape(1, n))


def _mm_fullk_body(*refs, prologue, epilogue, nsub):
    refs = list(refs)
    mode_ref = refs.pop(0) if epilogue == "rope" else None
    x_ref, p1_ref = refs.pop(0), refs.pop(0)
    p2_ref = refs.pop(0) if prologue == "mod" else None
    w_ref = refs.pop(0)
    if epilogue == "rope":
        ta_ref, tb_ref = refs.pop(0), refs.pop(0)
    o_ref, h_scr = refs
    j = pl.program_id(1)

    @pl.when(j == 0)
    def _():
        x = x_ref[...]
        if prologue == "mod":
            h = x * (1.0 + p1_ref[...]) + p2_ref[...]
        else:
            h = x * lax.rsqrt(jnp.mean(x * x, axis=-1, keepdims=True) + 1e-6) * p1_ref[...]
        h_scr[...] = h.astype(BF16)

    acc = jnp.dot(h_scr[...], w_ref[...], preferred_element_type=F32)
    if epilogue == "relu2":
        r = jnp.maximum(acc, 0.0)
        o_ref[...] = (r * r).astype(o_ref.dtype)
    elif epilogue == "rope":
        for s in range(nsub):
            sl = slice(s * LANES, (s + 1) * LANES)
            md = mode_ref[j * nsub + s]
            sub = acc[:, sl]
            o_ref[:, sl] = (sub * ta_ref[md] + pltpu.roll(sub, LANES // 2, 1) * tb_ref[md]).astype(o_ref.dtype)
    else:
        o_ref[...] = acc.astype(o_ref.dtype)


def mm_fullk(x, x_col_block, k, w, *, prologue, p1, p2=None, epilogue="none", modes=None, ta=None, tb=None,
             out_dtype=BF16, tm=512, tn=1024):
    m = x.shape[0]
    n = w.shape[1]
    tm = min(tm, m)
    assert m % tm == 0 and n % tn == 0 and w.shape[0] == k
    nsub = tn // LANES
    rope = epilogue == "rope"
    npre = 1 if rope else 0

    def im(f):
        return (lambda i, j, *_: f(i, j))

    in_specs = [pl.BlockSpec((tm, k), im(lambda i, j: (i, x_col_block))),
                pl.BlockSpec((1, k), im(lambda i, j: (0, 0)))]
    args = [x, p1.reshape(1, k)]
    if prologue == "mod":
        in_specs.append(pl.BlockSpec((1, k), im(lambda i, j: (0, 0))))
        args.append(p2.reshape(1, k))
    in_specs.append(pl.BlockSpec((k, tn), im(lambda i, j: (0, j))))
    args.append(w)
    if rope:
        nmode = ta.shape[0]
        in_specs += [pl.BlockSpec((nmode, tm, LANES), im(lambda i, j: (0, i, 0)))] * 2
        args += [ta, tb]
    grid_spec = pltpu.PrefetchScalarGridSpec(
        num_scalar_prefetch=npre,
        grid=(m // tm, n // tn),
        in_specs=in_specs,
        out_specs=pl.BlockSpec((tm, tn), im(lambda i, j: (i, j))),
        scratch_shapes=[pltpu.VMEM((tm, k), BF16)],
    )
    fn = pl.pallas_call(
        functools.partial(_mm_fullk_body, prologue=prologue, epilogue=epilogue, nsub=nsub),
        grid_spec=grid_spec,
        out_shape=jax.ShapeDtypeStruct((m, n), out_dtype),
        compiler_params=_params(("parallel", "arbitrary")),
        name="mm_fullk_" + prologue + "_" + epilogue,
    )
    if rope:
        return fn(jnp.asarray(modes, jnp.int32), *args)
    return fn(*args)


def _mm_ln_body(lhs_ref, w_ref, x_ref, g_ref, lng_ref, lnb_ref, o_ref, acc_ref, *, nk):
    kk = pl.program_id(1)

    @pl.when(kk == 0)
    def _():
        acc_ref[...] = jnp.dot(lhs_ref[...], w_ref[...], preferred_element_type=F32)

    @pl.when(kk > 0)
    def _():
        acc_ref[...] += jnp.dot(lhs_ref[...], w_ref[...], preferred_element_type=F32)

    @pl.when(kk == nk - 1)
    def _():
        z = ALPHA * x_ref[...] + (1.0 + g_ref[...]) * acc_ref[...]
        mu = jnp.mean(z, axis=-1, keepdims=True)
        zc = z - mu
        var = jnp.mean(zc * zc, axis=-1, keepdims=True)
        o_ref[...] = zc * lax.rsqrt(var + 1e-5) * lng_ref[...] + lnb_ref[...]


def mm_ln(lhs, w, x, gate, ln_g, ln_b, tm=512, tk=512):
    m, k = lhs.shape
    n = w.shape[1]
    tm = min(tm, m)
    nk = k // tk
    row = lambda i, kk: (0, 0)
    return pl.pallas_call(
        functools.partial(_mm_ln_body, nk=nk),
        grid=(m // tm, nk),
        in_specs=[
            pl.BlockSpec((tm, tk), lambda i, kk: (i, kk)),
            pl.BlockSpec((tk, n), lambda i, kk: (kk, 0)),
            pl.BlockSpec((tm, n), lambda i, kk: (i, 0)),
            pl.BlockSpec((1, n), row), pl.BlockSpec((1, n), row), pl.BlockSpec((1, n), row),
        ],
        out_specs=pl.BlockSpec((tm, n), lambda i, kk: (i, 0)),
        out_shape=jax.ShapeDtypeStruct((m, n), F32),
        scratch_shapes=[pltpu.VMEM((tm, n), F32)],
        compiler_params=_params(("parallel", "arbitrary"), V7X_VMEM_LIMIT_LARGE),
        name="mm_ln",
    )(lhs, w, x, gate.reshape(1, n), ln_g.reshape(1, n), ln_b.reshape(1, n))


def _dilated_body(q_ref, kc_ref, kp_ref, vc_ref, vp_ref, o_ref, q32, kc32, kp32, vc32, vp32, od_scr, ld_scr):
    n = pl.program_id(1)
    q32[...] = q_ref[...].astype(F32)
    kc32[...] = kc_ref[...].astype(F32)
    kp32[...] = kp_ref[...].astype(F32)
    vc32[...] = vc_ref[...].astype(F32)
    vp32[...] = vp_ref[...].astype(F32)
    qi = lax.broadcasted_iota(jnp.int32, (QBLOCK, 2 * QBLOCK), 0)
    ki = lax.broadcasted_iota(jnp.int32, (QBLOCK, 2 * QBLOCK), 1)
    delta = QBLOCK + qi - ki
    in_band = (delta >= 0) & (delta <= QBLOCK)
    in_band_first = in_band & ((ki >= QBLOCK) | (n > 0))

    for pi, (window, dil) in enumerate(DSW_PATTERNS):
        assert window // dil == QBLOCK
        nblk = DIL_SUPER // (QBLOCK * dil)
        for mb in range(nblk):
            for r in range(dil):
                def rows(b):
                    return pl.ds(b * QBLOCK * dil + r, QBLOCK, stride=dil) if dil > 1 else pl.ds(b * QBLOCK, QBLOCK)

                cur = rows(mb)
                if mb > 0:
                    k_prev, v_prev, valid = kc32[rows(mb - 1), :], vc32[rows(mb - 1), :], in_band
                else:
                    k_prev, v_prev, valid = kp32[rows(nblk - 1), :], vp32[rows(nblk - 1), :], in_band_first
                k = jnp.concatenate([k_prev, kc32[cur, :]], axis=0).astype(BF16)
                v = jnp.concatenate([v_prev, vc32[cur, :]], axis=0).astype(BF16)
                s = lax.dot_general(q32[cur, :].astype(BF16), k, NT_DIMS, preferred_element_type=F32)
                s = jnp.where(valid, s, NEG_INF)
                m = jnp.max(s, axis=-1, keepdims=True)
                p = jnp.exp2(s - m)
                l = jnp.sum(p, axis=-1, keepdims=True)
                od_scr[pi, cur, :] = jnp.dot(p.astype(BF16), v, preferred_element_type=F32) / l
                ld_scr[pi, cur, :] = jnp.broadcast_to(m + jnp.log2(l), (QBLOCK, HEAD_DIM))

    npat = len(DSW_PATTERNS)
    for c0 in range(0, DIL_SUPER, DIL_MIX_ROWS):
        rs = slice(c0, c0 + DIL_MIX_ROWS)
        ls = [ld_scr[i, rs, :] for i in range(npat)]
        mx = functools.reduce(jnp.maximum, ls)
        ws = [jnp.exp2(t - mx) for t in ls]
        num = sum(ws[i] * od_scr[i, rs, :] for i in range(npat))
        o_ref[rs, :] = (num / sum(ws)).astype(o_ref.dtype)


def dilated_attention(qkv, s_len, q_blk, k_blk, v_blk):
    assert s_len % DIL_SUPER == 0
    cur = lambda blk: (lambda h, n: (n, blk + h))
    prev = lambda blk: (lambda h, n: (jnp.maximum(n - 1, 0), blk + h))
    bs = lambda f: pl.BlockSpec((DIL_SUPER, HEAD_DIM), f)
    f32_rows = pltpu.VMEM((DIL_SUPER, HEAD_DIM), F32)
    per_pattern = pltpu.VMEM((len(DSW_PATTERNS), DIL_SUPER, HEAD_DIM), F32)
    return pl.pallas_call(
        _dilated_body,
        grid=(DSW_HEADS, s_len // DIL_SUPER),
        in_specs=[bs(cur(q_blk)), bs(cur(k_blk)), bs(prev(k_blk)), bs(cur(v_blk)), bs(prev(v_blk))],
        out_specs=pl.BlockSpec((DIL_SUPER, HEAD_DIM), lambda h, n: (n, h)),
        out_shape=jax.ShapeDtypeStruct((s_len, DSW_W), BF16),
        scratch_shapes=[f32_rows] * 5 + [per_pattern] * 2,
        compiler_params=_params(("parallel", "parallel")),
        name="dilated_attention",
    )(qkv, qkv, qkv, qkv, qkv)


def _cmp_body(x_ref, pe_ref, w1_ref, w2_ref, o_ref, *, ncp):
    x = x_ref[...]
    half = NSA_CMP_STRIDE * HEAD_DIM
    a = jnp.dot((x + pe_ref[0:1, :]).astype(BF16), w1_ref[0:half, :], preferred_element_type=F32)
    b = jnp.dot((x + pe_ref[1:2, :]).astype(BF16), w1_ref[half:2 * half, :], preferred_element_type=F32)
    hid = a + pltpu.roll(b, ncp - 1, 0)
    act = jax.nn.gelu(hid)
    o_ref[...] = jnp.dot(act.astype(BF16), w2_ref[...], preferred_element_type=F32).astype(o_ref.dtype)


def nsa_compress(xs, pe, w1, w2):
    ncp = xs.shape[2]
    half = NSA_CMP_STRIDE * HEAD_DIM
    return pl.pallas_call(
        functools.partial(_cmp_body, ncp=ncp),
        grid=(2, NSA_KV_HEADS),
        in_specs=[
            pl.BlockSpec((None, None, ncp, half), lambda a, h: (a, h, 0, 0)),
            pl.BlockSpec((None, 2, half), lambda a, h: (a, 0, 0)),
            pl.BlockSpec((None, 2 * half, NSA_CMP_HIDDEN), lambda a, h: (a, 0, 0)),
            pl.BlockSpec((None, NSA_CMP_HIDDEN, HEAD_DIM), lambda a, h: (a, 0, 0)),
        ],
        out_specs=pl.BlockSpec((None, None, ncp, HEAD_DIM), lambda a, h: (a, h, 0, 0)),
        out_shape=jax.ShapeDtypeStruct((2, NSA_KV_HEADS, ncp, HEAD_DIM), BF16),
        compiler_params=_params(("parallel", "parallel")),
        name="nsa_compress",
    )(xs, pe.reshape(2, 2, half), w1, w2)


def _stack_heads(q):
    return jnp.concatenate([q[:, g * HEAD_DIM:(g + 1) * HEAD_DIM] for g in range(NSA_GROUP)], axis=0)


def _nsa_body(q_ref, kst_ref, vs_ref, kc_ref, vc_ref, covt_ref, e0_ref, gate_ref, u_ref,
              sel_scr, q4_scr, sc_scr, pf_scr, pc_scr, mk_scr, ve_scr, s_scr, p_scr, a_scr, m_scr, acc_scr,
              *, ncp, nselp, ntop, tk):
    n = pl.program_id(1)
    rows = NSA_GROUP * QBLOCK
    q4_scr[...] = _stack_heads(q_ref[...])
    q4 = q4_scr[...]

    sc_scr[...] = lax.dot_general(q4, kc_ref[...], NT_DIMS, preferred_element_type=F32)
    chunk = min(FLASH_CHUNK // ncp, rows)
    cend = lax.broadcasted_iota(jnp.int32, (chunk, ncp), 1) * NSA_CMP_STRIDE + (NSA_CMP_LEN - 1)
    for r in range(0, rows, chunk):
        rs = slice(r, r + chunk)
        qpos_r = n * QBLOCK + ((r + lax.broadcasted_iota(jnp.int32, (chunk, ncp), 0)) & (QBLOCK - 1))
        s = jnp.where(cend <= qpos_r, sc_scr[rs, :], NEG_INF)
        m = jnp.max(s, axis=-1, keepdims=True)
        e = jnp.exp2(s - jnp.where(m == NEG_INF, 0.0, m))
        l = jnp.sum(e, axis=-1, keepdims=True)
        p = e / jnp.where(l > 0, l, 1.0)
        pf_scr[rs, :] = p
        pc_scr[rs, :] = p.astype(BF16)
    o_c = jnp.dot(pc_scr[...], vc_ref[...], preferred_element_type=F32)

    psum = pf_scr[0:QBLOCK, :]
    for g in range(1, NSA_GROUP):
        psum = psum + pf_scr[g * QBLOCK:(g + 1) * QBLOCK, :]
    p_hi = psum.astype(BF16)
    p_lo = (psum - p_hi.astype(F32)).astype(BF16)
    covt = covt_ref[...]
    imp_t = (lax.dot_general(covt, p_hi, NT_DIMS, preferred_element_type=F32)
             + lax.dot_general(covt, p_lo, NT_DIMS, preferred_element_type=F32))

    jblk = lax.broadcasted_iota(jnp.int32, (nselp, QBLOCK), 0)
    qpos_c = n * QBLOCK + lax.broadcasted_iota(jnp.int32, (nselp, QBLOCK), 1)
    cur = qpos_c // NSA_SEL_BLOCK
    forced = (jblk == 0) | (jblk == cur) | (jblk == cur - 1)
    valid = jblk * NSA_SEL_BLOCK <= qpos_c
    score0 = jnp.where(valid, jnp.where(forced, jnp.inf, imp_t), NEG_INF)

    def pick(_, carry):
        score, picked = carry
        mx = jnp.max(score, axis=0, keepdims=True)
        idx = jnp.min(jnp.where(score == mx, jblk, nselp), axis=0, keepdims=True)
        hit = jblk == idx
        return jnp.where(hit, NEG_INF, score), jnp.where(hit, 1.0, picked)

    _, picked = lax.fori_loop(0, ntop, pick, (score0, jnp.zeros((nselp, QBLOCK), F32)))
    sel_scr[...] = jnp.where(valid, picked, 0.0).T

    blocks_per_tile = tk // NSA_SEL_BLOCK
    _flash_init(m_scr, acc_scr, ve_scr)

    def k_tile(t, slot, buf):
        shift = (nselp - t * blocks_per_tile) % nselp
        sel_t = pltpu.roll(sel_scr[...], shift, 1)[:, 0:LANES].astype(BF16)
        mk_scr[slot] = jnp.dot(sel_t, e0_ref[...], preferred_element_type=F32)
        return kst_ref[:, pl.ds(pl.multiple_of(t * tk, tk), tk)]

    def v_tile(t, buf):
        ve_scr[buf, :, 0:HEAD_DIM] = vs_ref[pl.ds(pl.multiple_of(t * tk, tk), tk), :]
        return ve_scr[buf]

    def mask_for(slot, t, masked):
        def mask_fn(r, s):
            rq = r % QBLOCK
            ok = mk_scr[slot, rq:rq + s.shape[0], :] > 0.5
            if masked:
                qpos = n * QBLOCK + rq + lax.broadcasted_iota(jnp.int32, s.shape, 0)
                ok = ok & (t * tk + lax.broadcasted_iota(jnp.int32, s.shape, 1) <= qpos)
            return jnp.where(ok, s, NEG_INF)
        return mask_fn

    _flash_pipeline((n * QBLOCK + QBLOCK - 1) // tk, q4_scr, k_tile, v_tile, mask_for,
                    s_scr, p_scr, a_scr, m_scr, acc_scr)
    o_s = _flash_output(acc_scr)

    gates = jax.nn.sigmoid(gate_ref[...])
    for g in range(NSA_GROUP):
        rs = slice(g * QBLOCK, (g + 1) * QBLOCK)
        u_ref[:, g * HEAD_DIM:(g + 1) * HEAD_DIM] = (gates[:, 3 * g:3 * g + 1] * o_c[rs]
                                                     + gates[:, 3 * g + 1:3 * g + 2] * o_s[rs])


def nsa_cmp_sel(qkv, ks_t, s_len, q_blk, vs_blk, kv_cmp, gates, tk=512):
    nb = s_len // QBLOCK
    ncp = s_len // NSA_CMP_STRIDE
    nsel = s_len // NSA_SEL_BLOCK
    nselp = -(-nsel // LANES) * LANES
    ntop = min(NSA_TOP_N, nsel)
    ci = np.arange(ncp)[None, :] * NSA_CMP_STRIDE
    sj = np.arange(nselp)[:, None] * NSA_SEL_BLOCK
    cov = (ci < sj + NSA_SEL_BLOCK) & (ci + NSA_CMP_LEN > sj) & (np.arange(ncp)[None, :] < ncp - 1) & (sj < s_len)
    covt = jnp.asarray(cov.astype(np.float32), BF16)
    e0 = jnp.asarray((np.arange(tk)[None, :] // NSA_SEL_BLOCK == np.arange(LANES)[:, None]).astype(np.float32), BF16)
    gw = NSA_GROUP * HEAD_DIM
    rows = NSA_GROUP * QBLOCK
    return pl.pallas_call(
        functools.partial(_nsa_body, ncp=ncp, nselp=nselp, ntop=ntop, tk=tk),
        grid=(NSA_KV_HEADS, nb),
        in_specs=[
            pl.BlockSpec((QBLOCK, gw), lambda h, n: (n, q_blk + h)),
            pl.BlockSpec((HEAD_DIM, s_len), lambda h, n: (h, 0)),
            pl.BlockSpec((s_len, HEAD_DIM), lambda h, n: (0, vs_blk + h)),
            pl.BlockSpec((None, None, ncp, HEAD_DIM), lambda h, n: (0, h, 0, 0)),
            pl.BlockSpec((None, None, ncp, HEAD_DIM), lambda h, n: (1, h, 0, 0)),
            pl.BlockSpec((nselp, ncp), lambda h, n: (0, 0)),
            pl.BlockSpec((LANES, tk), lambda h, n: (0, 0)),
            pl.BlockSpec((None, QBLOCK, 3 * NSA_GROUP), lambda h, n: (h, n, 0)),
        ],
        out_specs=pl.BlockSpec((QBLOCK, gw), lambda h, n: (n, h)),
        out_shape=jax.ShapeDtypeStruct((s_len, NSA_QW), F32),
        scratch_shapes=[pltpu.VMEM((QBLOCK, nselp), F32), pltpu.VMEM((rows, HEAD_DIM), BF16),
                        pltpu.VMEM((rows, ncp), F32), pltpu.VMEM((rows, ncp), F32), pltpu.VMEM((rows, ncp), BF16),
                        pltpu.VMEM((FLASH_UNROLL, QBLOCK, tk), F32), pltpu.VMEM((FLASH_UNROLL + 1, tk, 2 * HEAD_DIM), BF16),
                        pltpu.VMEM((FLASH_UNROLL, rows, tk), F32), pltpu.VMEM((FLASH_UNROLL, rows, tk), BF16),
                        pltpu.VMEM((FLASH_UNROLL, rows, LANES), F32), pltpu.VMEM((rows, LANES), F32),
                        pltpu.VMEM((rows, 2 * HEAD_DIM), F32)],
        compiler_params=_params(("parallel", "arbitrary")),
        name="nsa_cmp_sel",
    )(qkv, ks_t, qkv, kv_cmp, kv_cmp, covt, e0, gates)


def _win_body(q_ref, kw_ref, vw_ref, gate_ref, u_ref, o_ref, *, span):
    n = pl.program_id(1)
    rows = NSA_GROUP * QBLOCK
    q4 = _stack_heads(q_ref[...])
    start = pl.multiple_of(jnp.maximum(n * QBLOCK - (span - QBLOCK), 0), QBLOCK)
    s = lax.dot_general(q4, kw_ref[pl.ds(start, span), :], NT_DIMS, preferred_element_type=F32)
    qpos = n * QBLOCK + (lax.broadcasted_iota(jnp.int32, (rows, span), 0) & (QBLOCK - 1))
    delta = qpos - (start + lax.broadcasted_iota(jnp.int32, (rows, span), 1))
    s = jnp.where((delta >= 0) & (delta <= NSA_WINDOW - 1), s, NEG_INF)
    m = jnp.max(s, axis=-1, keepdims=True)
    p = jnp.exp2(s - m)
    l = jnp.sum(p, axis=-1, keepdims=True)
    o_w = jnp.dot(p.astype(BF16), vw_ref[pl.ds(start, span), :], preferred_element_type=F32) / l
    gates = jax.nn.sigmoid(gate_ref[...])
    for g in range(NSA_GROUP):
        sl = slice(g * HEAD_DIM, (g + 1) * HEAD_DIM)
        o_ref[:, sl] = (u_ref[:, sl] + gates[:, 3 * g + 2:3 * g + 3] * o_w[g * QBLOCK:(g + 1) * QBLOCK]
                        ).astype(o_ref.dtype)


def nsa_window(qkv, s_len, q_blk, kw_blk, vw_blk, gates, u):
    nb = s_len // QBLOCK
    span = (-(-(NSA_WINDOW - 1) // QBLOCK) + 1) * QBLOCK
    gw = NSA_GROUP * HEAD_DIM
    return pl.pallas_call(
        functools.partial(_win_body, span=span),
        grid=(NSA_KV_HEADS, nb),
        in_specs=[
            pl.BlockSpec((QBLOCK, gw), lambda h, n: (n, q_blk + h)),
            pl.BlockSpec((s_len, HEAD_DIM), lambda h, n: (0, kw_blk + h)),
            pl.BlockSpec((s_len, HEAD_DIM), lambda h, n: (0, vw_blk + h)),
            pl.BlockSpec((None, QBLOCK, 3 * NSA_GROUP), lambda h, n: (h, n, 0)),
            pl.BlockSpec((QBLOCK, gw), lambda h, n: (n, h)),
        ],
        out_specs=pl.BlockSpec((QBLOCK, gw), lambda h, n: (n, h)),
        out_shape=jax.ShapeDtypeStruct((s_len, NSA_QW), BF16),
        compiler_params=_params(("parallel", "parallel")),
        name="nsa_window",
    )(qkv, qkv, qkv, gates, u)


def _softmax_rows(s_ref, p_ref, a_ref, m_scr, mask_fn):
    rows, tk = s_ref.shape
    nrep = tk // LANES
    chunk = FLASH_CHUNK // tk
    for r in range(0, rows, chunk):
        rs = slice(r, r + chunk)
        s = s_ref[rs, :]
        if mask_fn is not None:
            s = mask_fn(r, s)
        m_old = m_scr[rs, :]
        m_new = jnp.maximum(m_old, jnp.max(s, axis=1, keepdims=True))
        p_ref[rs, :] = jnp.exp2(s - jnp.concatenate([m_new] * nrep, axis=1)).astype(BF16)
        a_ref[rs, :] = jnp.exp2(m_old - m_new)
        m_scr[rs, :] = m_new


def _flash_pipeline(n_full, q_ref, k_tile, v_tile, mask_for, s_scr, p_scr, a_scr, m_scr, acc_scr):
    def scores(t, idx, buf):
        s_scr[idx] = jnp.dot(q_ref[...], k_tile(t, idx, buf), preferred_element_type=F32)

    def softmax(idx, t, masked):
        _softmax_rows(s_scr.at[idx], p_scr.at[idx], a_scr.at[idx], m_scr, mask_for(idx, t, masked))

    def values(idx, t, buf):
        a = a_scr[idx]
        acc_scr[...] = (jnp.concatenate([a, a], axis=1) * acc_scr[...]
                        + jnp.dot(p_scr[idx], v_tile(t, buf), preferred_element_type=F32))

    last = FLASH_UNROLL - 1
    p_scr[last] = jnp.zeros(p_scr.shape[1:], p_scr.dtype)
    a_scr[last] = jnp.ones(a_scr.shape[1:], a_scr.dtype)
    scores(0, 0, 0)

    def run(t0, count, masked_last, lookahead):
        for j in range(count):
            t = t0 + j
            if lookahead or j + 1 < count:
                scores(t + 1, (j + 1) % FLASH_UNROLL, j)
            softmax(j, t, masked_last and j == count - 1)
            values((j - 1) % FLASH_UNROLL, jnp.maximum(t - 1, 0), j)
        if not lookahead:
            values(count - 1, t0 + count - 1, count)

    def body(u, carry):
        run(FLASH_UNROLL * u, FLASH_UNROLL, False, True)
        return carry

    lax.fori_loop(0, n_full // FLASH_UNROLL, body, 0)
    rem = n_full % FLASH_UNROLL
    for r in range(FLASH_UNROLL):
        @pl.when(rem == r)
        def _(r=r):
            run(n_full - r, r + 1, True, False)


def _flash_init(m_scr, acc_scr, ve_scr):
    dv = ve_scr.shape[2] // 2
    m_scr[...] = jnp.full(m_scr.shape, NEG_INF, F32)
    acc_scr[...] = jnp.zeros(acc_scr.shape, F32)
    ve_scr[:, :, dv:] = jnp.ones((ve_scr.shape[0], ve_scr.shape[1], dv), ve_scr.dtype)


def _flash_output(acc_scr):
    dv = acc_scr.shape[1] // 2
    return acc_scr[:, 0:dv] / acc_scr[:, dv:]


def _mla_body(q_ref, knt_ref, krt_ref, v_ref, o_ref, kt_scr, ve_scr, s_scr, p_scr, a_scr, m_scr, acc_scr,
              *, tq, tk):
    qi = pl.program_id(1)
    _flash_init(m_scr, acc_scr, ve_scr)

    def mask_for(slot, t, masked):
        if not masked:
            return None

        def causal(r, s):
            qpos = qi * tq + r + lax.broadcasted_iota(jnp.int32, s.shape, 0)
            kpos = t * tk + lax.broadcasted_iota(jnp.int32, s.shape, 1)
            return jnp.where(kpos <= qpos, s, NEG_INF)
        return causal

    def k_tile(t, slot, buf):
        k0 = pl.multiple_of(t * tk, tk)
        kt_scr[buf, 0:MLA_NOPE_DIM, :] = knt_ref[:, pl.ds(k0, tk)]
        kt_scr[buf, MLA_NOPE_DIM:, :] = krt_ref[:, pl.ds(k0, tk)]
        return kt_scr[buf]

    def v_tile(t, buf):
        ve_scr[buf, :, 0:MLA_V_DIM] = v_ref[pl.ds(pl.multiple_of(t * tk, tk), tk), :]
        return ve_scr[buf]

    _flash_pipeline((qi * tq) // tk, q_ref, k_tile, v_tile, mask_for, s_scr, p_scr, a_scr, m_scr, acc_scr)
    o_ref[...] = _flash_output(acc_scr).astype(o_ref.dtype)


def mla_attention(q, kv, k_nope_t, k_rope_t, s_len, tq=512, tk=512):
    tq, tk = min(tq, s_len), min(tk, s_len)
    qw = 2 * LANES
    return pl.pallas_call(
        functools.partial(_mla_body, tq=tq, tk=tk),
        grid=(MLA_HEADS, s_len // tq),
        in_specs=[
            pl.BlockSpec((tq, qw), lambda h, i: (i, h)),
            pl.BlockSpec((MLA_NOPE_DIM, s_len), lambda h, i: (h, 0)),
            pl.BlockSpec((LANES, s_len), lambda h, i: (0, 0)),
            pl.BlockSpec((s_len, MLA_V_DIM), lambda h, i: (0, MLA_HEADS + h)),
        ],
        out_specs=pl.BlockSpec((tq, MLA_V_DIM), lambda h, i: (i, h)),
        out_shape=jax.ShapeDtypeStruct((s_len, MLA_HEADS * MLA_V_DIM), BF16),
        scratch_shapes=[pltpu.VMEM((FLASH_UNROLL, qw, tk), BF16),
                        pltpu.VMEM((FLASH_UNROLL + 1, tk, 2 * MLA_V_DIM), BF16),
                        pltpu.VMEM((FLASH_UNROLL, tq, tk), F32), pltpu.VMEM((FLASH_UNROLL, tq, tk), BF16),
                        pltpu.VMEM((FLASH_UNROLL, tq, LANES), F32), pltpu.VMEM((tq, LANES), F32),
                        pltpu.VMEM((tq, 2 * MLA_V_DIM), F32)],
        compiler_params=_params(("parallel", "parallel")),
        name="mla_attention",
    )(q, k_nope_t, k_rope_t, kv)


_IN_SIZES = (DSW_W, DSW_W, DSW_W, NSA_QW, NSA_KVW, NSA_KVW, NSA_KVW, NSA_KVW, NSA_KVW, NSA_KVW,
             3 * NSA_HEADS, MLA_Q_LORA, MLA_KV_LORA, MLA_ROPE_DIM)
_IN_NAMES = ("a_q", "a_k", "a_v", "n_q", "n_kc", "n_vc", "n_ks", "n_vs", "n_kw", "n_vw", "n_gate",
             "m_cq", "m_ckv", "m_kr")
_IN_OFF = dict(zip(_IN_NAMES, np.concatenate([[0], np.cumsum(_IN_SIZES)[:-1]]).tolist()))
_IN_LEN = dict(zip(_IN_NAMES, _IN_SIZES))

_B_ORDER = ("a_q", "a_k", "n_q", "n_ks", "n_kw", "n_vs", "n_vw", "a_v")
_B_ROPE = 2 * DSW_W + NSA_QW + 2 * NSA_KVW
_B_WIDTH = sum(_IN_LEN[k] for k in _B_ORDER)
_B_COL = dict(zip(_B_ORDER, np.concatenate([[0], np.cumsum([_IN_LEN[k] for k in _B_ORDER])[:-1]]).tolist()))
_F_COL = {"m_cq": 0, "m_ckv": MLA_Q_LORA, "n_kc": 2048, "n_vc": 2304, "m_kr": 2560, "n_gate": 2688}
_F_WIDTH = 3072
_HALF_ROPE = MLA_ROPE_DIM // 2


def _prep_w_in_body(w_ref, wb_ref, wf_ref):
    for name in _B_ORDER:
        wb_ref[:, _B_COL[name]:_B_COL[name] + _IN_LEN[name]] = (
            w_ref[:, _IN_OFF[name]:_IN_OFF[name] + _IN_LEN[name]].astype(BF16))
    wf_ref[...] = jnp.zeros(wf_ref.shape, BF16)
    for name in ("m_cq", "m_ckv", "n_kc", "n_vc", "n_gate"):
        wf_ref[:, _F_COL[name]:_F_COL[name] + _IN_LEN[name]] = (
            w_ref[:, _IN_OFF[name]:_IN_OFF[name] + _IN_LEN[name]].astype(BF16))
    kr_src, kr_dst = _IN_OFF["m_kr"], _F_COL["m_kr"]
    for half in range(2):
        wf_ref[:, kr_dst + 2 * half * _HALF_ROPE:kr_dst + (2 * half + 1) * _HALF_ROPE] = (
            w_ref[:, kr_src + half * _HALF_ROPE:kr_src + (half + 1) * _HALF_ROPE].astype(BF16))


def _prep_w_in(w, tm=256):
    k, n = w.shape
    return pl.pallas_call(
        _prep_w_in_body,
        grid=(k // tm,),
        in_specs=[pl.BlockSpec((tm, n), lambda i: (i, 0))],
        out_specs=[pl.BlockSpec((tm, _B_WIDTH), lambda i: (i, 0)), pl.BlockSpec((tm, _F_WIDTH), lambda i: (i, 0))],
        out_shape=[jax.ShapeDtypeStruct((k, _B_WIDTH), BF16), jax.ShapeDtypeStruct((k, _F_WIDTH), BF16)],
        compiler_params=_params(("parallel",)),
        name="prep_w_in",
    )(w)


_B_MODES = ([2] * (DSW_W // LANES) + [1] * (DSW_W // LANES) + [2] * (NSA_QW // LANES)
            + [1] * (2 * NSA_KVW // LANES) + [0] * ((_B_WIDTH - _B_ROPE) // LANES))
_F_MODES = [0] * 16 + [1, 1, 0, 0, 2] + [0] * 3
_Q_MODES = [0, 1] * MLA_HEADS
C_HEAD = HEAD_DIM ** -0.5 * LOG2E
C_MLA = (MLA_NOPE_DIM + MLA_ROPE_DIM) ** -0.5 * LOG2E


def _prep_w_uq(w):
    w = w.reshape(MLA_Q_LORA, MLA_HEADS, MLA_NOPE_DIM + MLA_ROPE_DIM)
    z = jnp.zeros((MLA_Q_LORA, MLA_HEADS, _HALF_ROPE), w.dtype)
    w = jnp.concatenate([w[..., :MLA_NOPE_DIM], w[..., MLA_NOPE_DIM:MLA_NOPE_DIM + _HALF_ROPE], z,
                         w[..., MLA_NOPE_DIM + _HALF_ROPE:], z], axis=-1)
    return w.reshape(MLA_Q_LORA, MLA_HEADS * 2 * LANES).astype(BF16)


def _prep_w_ukv(w):
    w = w.reshape(MLA_KV_LORA, MLA_HEADS, 2, MLA_NOPE_DIM).transpose(0, 2, 1, 3)
    return w.reshape(MLA_KV_LORA, 2 * MLA_HEADS * MLA_NOPE_DIM).astype(BF16)


def _rope_tables(positions):
    pos = positions.astype(F32)[:, None]

    def cs(dim):
        inv = ROPE_THETA ** (-jnp.arange(0, dim, 2, dtype=F32) / dim)
        ang = pos * inv
        return jnp.cos(ang), jnp.sin(ang)

    c128, s128 = cs(HEAD_DIM)
    c64, s64 = cs(MLA_ROPE_DIM)
    z = jnp.zeros_like(c64)
    one = jnp.ones((pos.shape[0], LANES), F32)
    zero = jnp.zeros((pos.shape[0], LANES), F32)
    a128, b128 = jnp.concatenate([c128, c128], 1), jnp.concatenate([-s128, s128], 1)
    a64, b64 = jnp.concatenate([c64, z, c64, z], 1), jnp.concatenate([-s64, z, s64, z], 1)
    return {"b": (jnp.stack([one, a128, a128 * C_HEAD]), jnp.stack([zero, b128, b128 * C_HEAD])),
            "f": (jnp.stack([one, a128, a64]), jnp.stack([zero, b128, b64])),
            "q": (jnp.stack([one * C_MLA, a64 * C_MLA]), jnp.stack([zero, b64 * C_MLA]))}


def _mixer(x, sc1, sh1, tabs, w_in, cmp_pe, cmp_w1, cmp_w2, q_norm, kv_norm, w_uq, w_ukv):
    s_len = x.shape[0]
    wb, wf = _prep_w_in(w_in)
    pb = mm_fullk(x, 0, D_MODEL, wb, prologue="mod", p1=sc1, p2=sh1, epilogue="rope", modes=_B_MODES,
                  ta=tabs["b"][0], tb=tabs["b"][1], out_dtype=BF16)
    pf = mm_fullk(x, 0, D_MODEL, wf, prologue="mod", p1=sc1, p2=sh1, epilogue="rope", modes=_F_MODES,
                  ta=tabs["f"][0], tb=tabs["f"][1], out_dtype=F32)

    blk = lambda name: _B_COL[name] // HEAD_DIM
    out_a = dilated_attention(pb, s_len, blk("a_q"), blk("a_k"), blk("a_v"))

    ncp = s_len // NSA_CMP_STRIDE

    def blocks16(name):
        t = pf[:, _F_COL[name]:_F_COL[name] + NSA_KVW]
        return t.reshape(ncp, NSA_CMP_STRIDE, NSA_KV_HEADS, HEAD_DIM).transpose(2, 0, 1, 3).reshape(
            NSA_KV_HEADS, ncp, NSA_CMP_STRIDE * HEAD_DIM)

    kv_cmp = nsa_compress(jnp.stack([blocks16("n_kc"), blocks16("n_vc")]), cmp_pe,
                          cmp_w1.astype(BF16), cmp_w2.astype(BF16))
    gates = pf[:, _F_COL["n_gate"]:_F_COL["n_gate"] + 3 * NSA_HEADS]
    gates = gates.reshape(s_len, NSA_KV_HEADS, 3 * NSA_GROUP).transpose(1, 0, 2)
    gw = NSA_GROUP * HEAD_DIM
    ks_t = pb[:, _B_COL["n_ks"]:_B_COL["n_ks"] + NSA_KVW].T
    u = nsa_cmp_sel(pb, ks_t, s_len, _B_COL["n_q"] // gw, _B_COL["n_vs"] // HEAD_DIM, kv_cmp, gates)
    out_b = nsa_window(pb, s_len, _B_COL["n_q"] // gw, _B_COL["n_kw"] // HEAD_DIM, _B_COL["n_vw"] // HEAD_DIM,
                       gates, u)

    q = mm_fullk(pf, 0, MLA_Q_LORA, _prep_w_uq(w_uq), prologue="rms", p1=q_norm, epilogue="rope",
                 modes=_Q_MODES, ta=tabs["q"][0], tb=tabs["q"][1], out_dtype=BF16)
    kv = mm_fullk(pf, _F_COL["m_ckv"] // MLA_KV_LORA, MLA_KV_LORA, _prep_w_ukv(w_ukv), prologue="rms",
                  p1=kv_norm, out_dtype=BF16)
    k_rope_t = pf[:, _F_COL["m_kr"]:_F_COL["m_kr"] + LANES].astype(BF16).T
    k_nope_t = kv[:, :MLA_HEADS * MLA_NOPE_DIM].T
    out_c = mla_attention(q, kv, k_nope_t, k_rope_t, s_len)
    return jnp.concatenate([out_a, out_b, out_c], axis=1)


def kernel(x, c, positions, w_ada, b_ada, w_in, nsa_cmp_pe, nsa_cmp_w1, nsa_cmp_w2, mla_q_norm, mla_kv_norm,
           mla_w_uq, mla_w_ukv, w_out, ln1_g, ln1_b, mlp_w1, mlp_w2, ln2_g, ln2_b):
    assert x.shape[0] == 1, "kernel handles batch size 1"
    xs = x[0]
    d = xs.shape[1]
    tabs = _rope_tables(positions[0])
    for l in range(DEPTH):
        mod = adaln(c, w_ada[l], b_ada[l])
        sh1, sc1, g1, sh2, sc2, g2 = [mod[:, i * d:(i + 1) * d] for i in range(6)]
        mixed = _mixer(xs, sc1, sh1, tabs, w_in[l], nsa_cmp_pe[l], nsa_cmp_w1[l], nsa_cmp_w2[l],
                       mla_q_norm[l], mla_kv_norm[l], mla_w_uq[l], mla_w_ukv[l])
        xs = mm_ln(mixed, w_out[l].astype(BF16), xs, g1, ln1_g[l], ln1_b[l])
        act = mm_fullk(xs, 0, d, mlp_w1[l].astype(BF16), prologue="mod", p1=sc2, p2=sh2, epilogue="relu2",
                       out_dtype=BF16)
        xs = mm_ln(act, mlp_w2[l].astype(BF16), xs, g2, ln2_g[l], ln2_b[l])
    return xs[None]
```

```python
import functools

import jax
import jax.numpy as jnp
import numpy as np
from jax import lax
from jax.experimental import pallas as pl
from jax.experimental.pallas import tpu as pltpu

D_MODEL = 4096
DEPTH = 2
HEAD_DIM = 128
ROPE_THETA = 10000.0
QBLOCK = 128
DSW_HEADS = 8
DSW_PATTERNS = ((128, 1), (512, 4), (2048, 16))
NSA_HEADS = 8
NSA_KV_HEADS = 2
NSA_GROUP = NSA_HEADS // NSA_KV_HEADS
NSA_CMP_LEN = 32
NSA_CMP_STRIDE = 16
NSA_CMP_HIDDEN = 256
NSA_SEL_BLOCK = 64
NSA_TOP_N = 16
NSA_WINDOW = 512
MLA_HEADS = 16
MLA_Q_LORA = 1536
MLA_KV_LORA = 512
MLA_NOPE_DIM = 128
MLA_ROPE_DIM = 64
MLA_V_DIM = 128
D_FF = 4 * D_MODEL
ALPHA = (2 * DEPTH) ** 0.25

DSW_W = DSW_HEADS * HEAD_DIM
NSA_QW = NSA_HEADS * HEAD_DIM
NSA_KVW = NSA_KV_HEADS * HEAD_DIM

LANES = 128
V7X_VMEM_LIMIT = 56 * 1024 * 1024
V7X_VMEM_LIMIT_LARGE = 60 * 1024 * 1024

F32 = jnp.float32
BF16 = jnp.bfloat16
NEG_INF = float("-inf")
LOG2E = 1.4426950408889634
FLASH_CHUNK = 64 * 512
FLASH_UNROLL = 4
DIL_SUPER = QBLOCK * max(d for _, d in DSW_PATTERNS)
DIL_MIX_ROWS = 64
NT_DIMS = (((1,), (1,)), ((), ()))


def _params(sem, vmem=V7X_VMEM_LIMIT):
    return pltpu.CompilerParams(dimension_semantics=sem, vmem_limit_bytes=vmem)


def _adaln_body(cb_ref, w_ref, b_ref, o_ref, *, tn):
    cb = cb_ref[...]
    for s in range(tn // LANES):
        sl = slice(s * LANES, (s + 1) * LANES)
        o_ref[:, sl] = jnp.sum(w_ref[:, sl] * cb, axis=0, keepdims=True) + b_ref[:, sl]


def adaln(c, w, b, layer, tn=512):
    nl, d, n = w.shape
    cb = jnp.broadcast_to(c.reshape(d, 1), (d, LANES))
    return pl.pallas_call(
        functools.partial(_adaln_body, tn=tn),
        grid=(n // tn,),
        in_specs=[
            pl.BlockSpec((d, LANES), lambda j: (0, 0)),
            pl.BlockSpec((None, d, tn), lambda j: (layer, 0, j)),
            pl.BlockSpec((None, 1, tn), lambda j: (layer, 0, j)),
        ],
        out_specs=pl.BlockSpec((1, tn), lambda j: (0, j)),
        out_shape=jax.ShapeDtypeStruct((1, n), F32),
        compiler_params=_params(("arbitrary",)),
        name="adaln",
    )(cb, w, b.reshape(nl, 1, n))


def _mm_fullk_body(*refs, prologue, epilogue, nsub):
    refs = list(refs)
    mode_ref = refs.pop(0) if epilogue == "rope" else None
    x_ref, p1_ref = refs.pop(0), refs.pop(0)
    p2_ref = refs.pop(0) if prologue == "mod" else None
    w_ref = refs.pop(0)
    if epilogue == "rope":
        ta_ref, tb_ref = refs.pop(0), refs.pop(0)
    o_ref, h_scr = refs
    j = pl.program_id(1)

    @pl.when(j == 0)
    def _():
        x = x_ref[...]
        if prologue == "mod":
            h = x * (1.0 + p1_ref[...]) + p2_ref[...]
        else:
            h = x * lax.rsqrt(jnp.mean(x * x, axis=-1, keepdims=True) + 1e-6) * p1_ref[...]
        h_scr[...] = h.astype(BF16)

    acc = jnp.dot(h_scr[...], w_ref[...], preferred_element_type=F32)
    if epilogue == "relu2":
        r = jnp.maximum(acc, 0.0)
        o_ref[...] = (r * r).astype(o_ref.dtype)
    elif epilogue == "rope":
        for s in range(nsub):
            sl = slice(s * LANES, (s + 1) * LANES)
            md = mode_ref[j * nsub + s]
            sub = acc[:, sl]
            o_ref[:, sl] = (sub * ta_ref[md] + pltpu.roll(sub, LANES // 2, 1) * tb_ref[md]).astype(o_ref.dtype)
    else:
        o_ref[...] = acc.astype(o_ref.dtype)


def mm_fullk(x, x_col_block, k, w, *, prologue, p1, p2=None, epilogue="none", modes=None, ta=None, tb=None,
             out_dtype=BF16, tm=512, tn=1024, w_layer=None):
    m = x.shape[0]
    n = w.shape[-1]
    tm = min(tm, m)
    assert m % tm == 0 and n % tn == 0 and w.shape[-2] == k
    nsub = tn // LANES
    rope = epilogue == "rope"
    npre = 1 if rope else 0

    def im(f):
        return (lambda i, j, *_: f(i, j))

    in_specs = [pl.BlockSpec((tm, k), im(lambda i, j: (i, x_col_block))),
                pl.BlockSpec((1, k), im(lambda i, j: (0, 0)))]
    args = [x, p1.reshape(1, k)]
    if prologue == "mod":
        in_specs.append(pl.BlockSpec((1, k), im(lambda i, j: (0, 0))))
        args.append(p2.reshape(1, k))
    if w_layer is None:
        in_specs.append(pl.BlockSpec((k, tn), im(lambda i, j: (0, j))))
    else:
        in_specs.append(pl.BlockSpec((None, k, tn), im(lambda i, j: (w_layer, 0, j))))
    args.append(w)
    if rope:
        nmode = ta.shape[0]
        in_specs += [pl.BlockSpec((nmode, tm, LANES), im(lambda i, j: (0, i, 0)))] * 2
        args += [ta, tb]
    grid_spec = pltpu.PrefetchScalarGridSpec(
        num_scalar_prefetch=npre,
        grid=(m // tm, n // tn),
        in_specs=in_specs,
        out_specs=pl.BlockSpec((tm, tn), im(lambda i, j: (i, j))),
        scratch_shapes=[pltpu.VMEM((tm, k), BF16)],
    )
    fn = pl.pallas_call(
        functools.partial(_mm_fullk_body, prologue=prologue, epilogue=epilogue, nsub=nsub),
        grid_spec=grid_spec,
        out_shape=jax.ShapeDtypeStruct((m, n), out_dtype),
        compiler_params=_params(("parallel", "arbitrary")),
        name="mm_fullk_" + prologue + "_" + epilogue,
    )
    if rope:
        return fn(jnp.asarray(modes, jnp.int32), *args)
    return fn(*args)


def _mm_ln_body(lhs_ref, w_ref, x_ref, g_ref, lng_ref, lnb_ref, o_ref, acc_ref, *, nk):
    kk = pl.program_id(1)

    @pl.when(kk == 0)
    def _():
        acc_ref[...] = jnp.dot(lhs_ref[...], w_ref[...], preferred_element_type=F32)

    @pl.when(kk > 0)
    def _():
        acc_ref[...] += jnp.dot(lhs_ref[...], w_ref[...], preferred_element_type=F32)

    @pl.when(kk == nk - 1)
    def _():
        z = ALPHA * x_ref[...] + (1.0 + g_ref[...]) * acc_ref[...]
        mu = jnp.mean(z, axis=-1, keepdims=True)
        zc = z - mu
        var = jnp.mean(zc * zc, axis=-1, keepdims=True)
        o_ref[...] = zc * lax.rsqrt(var + 1e-5) * lng_ref[...] + lnb_ref[...]


def mm_ln(lhs, w, w_layer, x, gate, ln_g, ln_b, tm=512, tk=512):
    m, k = lhs.shape
    n = w.shape[2]
    tm = min(tm, m)
    nk = k // tk
    row = lambda i, kk: (0, 0)
    return pl.pallas_call(
        functools.partial(_mm_ln_body, nk=nk),
        grid=(m // tm, nk),
        in_specs=[
            pl.BlockSpec((tm, tk), lambda i, kk: (i, kk)),
            pl.BlockSpec((None, tk, n), lambda i, kk: (w_layer, kk, 0)),
            pl.BlockSpec((tm, n), lambda i, kk: (i, 0)),
            pl.BlockSpec((1, n), row), pl.BlockSpec((1, n), row), pl.BlockSpec((1, n), row),
        ],
        out_specs=pl.BlockSpec((tm, n), lambda i, kk: (i, 0)),
        out_shape=jax.ShapeDtypeStruct((m, n), F32),
        scratch_shapes=[pltpu.VMEM((tm, n), F32)],
        compiler_params=_params(("parallel", "arbitrary"), V7X_VMEM_LIMIT_LARGE),
        name="mm_ln",
    )(lhs, w, x, gate.reshape(1, n), ln_g.reshape(1, n), ln_b.reshape(1, n))


def _dilated_body(q_ref, kc_ref, kp_ref, vc_ref, vp_ref, o_ref, q32, kc32, kp32, vc32, vp32, od_scr, ld_scr):
    n = pl.program_id(1)
    q32[...] = q_ref[...].astype(F32)
    kc32[...] = kc_ref[...].astype(F32)
    kp32[...] = kp_ref[...].astype(F32)
    vc32[...] = vc_ref[...].astype(F32)
    vp32[...] = vp_ref[...].astype(F32)
    qi = lax.broadcasted_iota(jnp.int32, (QBLOCK, 2 * QBLOCK), 0)
    ki = lax.broadcasted_iota(jnp.int32, (QBLOCK, 2 * QBLOCK), 1)
    delta = QBLOCK + qi - ki
    in_band = (delta >= 0) & (delta <= QBLOCK)
    in_band_first = in_band & ((ki >= QBLOCK) | (n > 0))

    for pi, (window, dil) in enumerate(DSW_PATTERNS):
        assert window // dil == QBLOCK
        nblk = DIL_SUPER // (QBLOCK * dil)
        for mb in range(nblk):
            for r in range(dil):
                def rows(b):
                    return pl.ds(b * QBLOCK * dil + r, QBLOCK, stride=dil) if dil > 1 else pl.ds(b * QBLOCK, QBLOCK)

                cur = rows(mb)
                if mb > 0:
                    k_prev, v_prev, valid = kc32[rows(mb - 1), :], vc32[rows(mb - 1), :], in_band
                else:
                    k_prev, v_prev, valid = kp32[rows(nblk - 1), :], vp32[rows(nblk - 1), :], in_band_first
                k = jnp.concatenate([k_prev, kc32[cur, :]], axis=0).astype(BF16)
                v = jnp.concatenate([v_prev, vc32[cur, :]], axis=0).astype(BF16)
                s = lax.dot_general(q32[cur, :].astype(BF16), k, NT_DIMS, preferred_element_type=F32)
                s = jnp.where(valid, s, NEG_INF)
                m = jnp.max(s, axis=-1, keepdims=True)
                p = jnp.exp2(s - m)
                l = jnp.sum(p, axis=-1, keepdims=True)
                od_scr[pi, cur, :] = jnp.dot(p.astype(BF16), v, preferred_element_type=F32) / l
                ld_scr[pi, cur, :] = jnp.broadcast_to(m + jnp.log2(l), (QBLOCK, HEAD_DIM))

    npat = len(DSW_PATTERNS)
    for c0 in range(0, DIL_SUPER, DIL_MIX_ROWS):
        rs = slice(c0, c0 + DIL_MIX_ROWS)
        ls = [ld_scr[i, rs, :] for i in range(npat)]
        mx = functools.reduce(jnp.maximum, ls)
        ws = [jnp.exp2(t - mx) for t in ls]
        num = sum(ws[i] * od_scr[i, rs, :] for i in range(npat))
        o_ref[rs, :] = (num / sum(ws)).astype(o_ref.dtype)


def dilated_attention(qkv, s_len, q_blk, k_blk, v_blk):
    assert s_len % DIL_SUPER == 0
    cur = lambda blk: (lambda h, n: (n, blk + h))
    prev = lambda blk: (lambda h, n: (jnp.maximum(n - 1, 0), blk + h))
    bs = lambda f: pl.BlockSpec((DIL_SUPER, HEAD_DIM), f)
    f32_rows = pltpu.VMEM((DIL_SUPER, HEAD_DIM), F32)
    per_pattern = pltpu.VMEM((len(DSW_PATTERNS), DIL_SUPER, HEAD_DIM), F32)
    return pl.pallas_call(
        _dilated_body,
        grid=(DSW_HEADS, s_len // DIL_SUPER),
        in_specs=[bs(cur(q_blk)), bs(cur(k_blk)), bs(prev(k_blk)), bs(cur(v_blk)), bs(prev(v_blk))],
        out_specs=pl.BlockSpec((DIL_SUPER, HEAD_DIM), lambda h, n: (n, h)),
        out_shape=jax.ShapeDtypeStruct((s_len, DSW_W), BF16),
        scratch_shapes=[f32_rows] * 5 + [per_pattern] * 2,
        compiler_params=_params(("parallel", "parallel")),
        name="dilated_attention",
    )(qkv, qkv, qkv, qkv, qkv)


def _cmp_body(x_ref, pe_ref, w1_ref, w2_ref, o_ref, *, ncp):
    x = x_ref[...]
    half = NSA_CMP_STRIDE * HEAD_DIM
    a = jnp.dot((x + pe_ref[0:1, :]).astype(BF16), w1_ref[0:half, :], preferred_element_type=F32)
    b = jnp.dot((x + pe_ref[1:2, :]).astype(BF16), w1_ref[half:2 * half, :], preferred_element_type=F32)
    hid = a + pltpu.roll(b, ncp - 1, 0)
    act = jax.nn.gelu(hid)
    o_ref[...] = jnp.dot(act.astype(BF16), w2_ref[...], preferred_element_type=F32).astype(o_ref.dtype)


def nsa_compress(xs, pe, w1, w2):
    ncp = xs.shape[2]
    half = NSA_CMP_STRIDE * HEAD_DIM
    return pl.pallas_call(
        functools.partial(_cmp_body, ncp=ncp),
        grid=(2, NSA_KV_HEADS),
        in_specs=[
            pl.BlockSpec((None, None, ncp, half), lambda a, h: (a, h, 0, 0)),
            pl.BlockSpec((None, 2, half), lambda a, h: (a, 0, 0)),
            pl.BlockSpec((None, 2 * half, NSA_CMP_HIDDEN), lambda a, h: (a, 0, 0)),
            pl.BlockSpec((None, NSA_CMP_HIDDEN, HEAD_DIM), lambda a, h: (a, 0, 0)),
        ],
        out_specs=pl.BlockSpec((None, None, ncp, HEAD_DIM), lambda a, h: (a, h, 0, 0)),
        out_shape=jax.ShapeDtypeStruct((2, NSA_KV_HEADS, ncp, HEAD_DIM), BF16),
        compiler_params=_params(("parallel", "parallel")),
        name="nsa_compress",
    )(xs, pe.reshape(2, 2, half), w1, w2)


def _stack_heads(q):
    return jnp.concatenate([q[:, g * HEAD_DIM:(g + 1) * HEAD_DIM] for g in range(NSA_GROUP)], axis=0)


def _nsa_body(q_ref, kst_ref, vs_ref, kc_ref, vc_ref, covt_ref, e0_ref, gate_ref, u_ref,
              sel_scr, q4_scr, sc_scr, pf_scr, pc_scr, mk_scr, ve_scr, s_scr, p_scr, a_scr, m_scr, acc_scr,
              *, ncp, nselp, ntop, tk):
    n = pl.program_id(1)
    rows = NSA_GROUP * QBLOCK
    q4_scr[...] = _stack_heads(q_ref[...])
    q4 = q4_scr[...]

    sc_scr[...] = lax.dot_general(q4, kc_ref[...], NT_DIMS, preferred_element_type=F32)
    chunk = min(FLASH_CHUNK // ncp, rows)
    cend = lax.broadcasted_iota(jnp.int32, (chunk, ncp), 1) * NSA_CMP_STRIDE + (NSA_CMP_LEN - 1)
    for r in range(0, rows, chunk):
        rs = slice(r, r + chunk)
        qpos_r = n * QBLOCK + ((r + lax.broadcasted_iota(jnp.int32, (chunk, ncp), 0)) & (QBLOCK - 1))
        s = jnp.where(cend <= qpos_r, sc_scr[rs, :], NEG_INF)
        m = jnp.max(s, axis=-1, keepdims=True)
        e = jnp.exp2(s - jnp.where(m == NEG_INF, 0.0, m))
        l = jnp.sum(e, axis=-1, keepdims=True)
        p = e / jnp.where(l > 0, l, 1.0)
        pf_scr[rs, :] = p
        pc_scr[rs, :] = p.astype(BF16)
    o_c = jnp.dot(pc_scr[...], vc_ref[...], preferred_element_type=F32)

    psum = pf_scr[0:QBLOCK, :]
    for g in range(1, NSA_GROUP):
        psum = psum + pf_scr[g * QBLOCK:(g + 1) * QBLOCK, :]
    p_hi = psum.astype(BF16)
    p_lo = (psum - p_hi.astype(F32)).astype(BF16)
    covt = covt_ref[...]
    imp_t = (lax.dot_general(covt, p_hi, NT_DIMS, preferred_element_type=F32)
             + lax.dot_general(covt, p_lo, NT_DIMS, preferred_element_type=F32))

    jblk = lax.broadcasted_iota(jnp.int32, (nselp, QBLOCK), 0)
    qpos_c = n * QBLOCK + lax.broadcasted_iota(jnp.int32, (nselp, QBLOCK), 1)
    cur = qpos_c // NSA_SEL_BLOCK
    forced = (jblk == 0) | (jblk == cur) | (jblk == cur - 1)
    valid = jblk * NSA_SEL_BLOCK <= qpos_c
    score0 = jnp.where(valid, jnp.where(forced, jnp.inf, imp_t), NEG_INF)

    def pick(_, carry):
        score, picked = carry
        mx = jnp.max(score, axis=0, keepdims=True)
        idx = jnp.min(jnp.where(score == mx, jblk, nselp), axis=0, keepdims=True)
        hit = jblk == idx
        return jnp.where(hit, NEG_INF, score), jnp.where(hit, 1.0, picked)

    _, picked = lax.fori_loop(0, ntop, pick, (score0, jnp.zeros((nselp, QBLOCK), F32)))
    sel_scr[...] = jnp.where(valid, picked, 0.0).T

    blocks_per_tile = tk // NSA_SEL_BLOCK
    _flash_init(m_scr, acc_scr, ve_scr)

    def k_tile(t, slot, buf):
        shift = (nselp - t * blocks_per_tile) % nselp
        sel_t = pltpu.roll(sel_scr[...], shift, 1)[:, 0:LANES].astype(BF16)
        mk_scr[slot] = jnp.dot(sel_t, e0_ref[...], preferred_element_type=F32)
        return kst_ref[:, pl.ds(pl.multiple_of(t * tk, tk), tk)]

    def v_tile(t, buf):
        ve_scr[buf, :, 0:HEAD_DIM] = vs_ref[pl.ds(pl.multiple_of(t * tk, tk), tk), :]
        return ve_scr[buf]

    def mask_for(slot, t, masked):
        def mask_fn(r, s):
            rq = r % QBLOCK
            ok = mk_scr[slot, rq:rq + s.shape[0], :] > 0.5
            if masked:
                qpos = n * QBLOCK + rq + lax.broadcasted_iota(jnp.int32, s.shape, 0)
                ok = ok & (t * tk + lax.broadcasted_iota(jnp.int32, s.shape, 1) <= qpos)
            return jnp.where(ok, s, NEG_INF)
        return mask_fn

    _flash_pipeline((n * QBLOCK + QBLOCK - 1) // tk, q4_scr, k_tile, v_tile, mask_for,
                    s_scr, p_scr, a_scr, m_scr, acc_scr)
    o_s = _flash_output(acc_scr)

    gates = jax.nn.sigmoid(gate_ref[...])
    for g in range(NSA_GROUP):
        rs = slice(g * QBLOCK, (g + 1) * QBLOCK)
        u_ref[:, g * HEAD_DIM:(g + 1) * HEAD_DIM] = (gates[:, 3 * g:3 * g + 1] * o_c[rs]
                                                     + gates[:, 3 * g + 1:3 * g + 2] * o_s[rs])


def nsa_cmp_sel(qkv, ks_t, s_len, q_blk, vs_blk, kv_cmp, gates, tk=512):
    nb = s_len // QBLOCK
    ncp = s_len // NSA_CMP_STRIDE
    nsel = s_len // NSA_SEL_BLOCK
    nselp = -(-nsel // LANES) * LANES
    ntop = min(NSA_TOP_N, nsel)
    ci = np.arange(ncp)[None, :] * NSA_CMP_STRIDE
    sj = np.arange(nselp)[:, None] * NSA_SEL_BLOCK
    cov = (ci < sj + NSA_SEL_BLOCK) & (ci + NSA_CMP_LEN > sj) & (np.arange(ncp)[None, :] < ncp - 1) & (sj < s_len)
    covt = jnp.asarray(cov.astype(np.float32), BF16)
    e0 = jnp.asarray((np.arange(tk)[None, :] // NSA_SEL_BLOCK == np.arange(LANES)[:, None]).astype(np.float32), BF16)
    gw = NSA_GROUP * HEAD_DIM
    rows = NSA_GROUP * QBLOCK
    return pl.pallas_call(
        functools.partial(_nsa_body, ncp=ncp, nselp=nselp, ntop=ntop, tk=tk),
        grid=(NSA_KV_HEADS, nb),
        in_specs=[
            pl.BlockSpec((QBLOCK, gw), lambda h, n: (n, q_blk + h)),
            pl.BlockSpec((HEAD_DIM, s_len), lambda h, n: (h, 0)),
            pl.BlockSpec((s_len, HEAD_DIM), lambda h, n: (0, vs_blk + h)),
            pl.BlockSpec((None, None, ncp, HEAD_DIM), lambda h, n: (0, h, 0, 0)),
            pl.BlockSpec((None, None, ncp, HEAD_DIM), lambda h, n: (1, h, 0, 0)),
            pl.BlockSpec((nselp, ncp), lambda h, n: (0, 0)),
            pl.BlockSpec((LANES, tk), lambda h, n: (0, 0)),
            pl.BlockSpec((None, QBLOCK, 3 * NSA_GROUP), lambda h, n: (h, n, 0)),
        ],
        out_specs=pl.BlockSpec((QBLOCK, gw), lambda h, n: (n, h)),
        out_shape=jax.ShapeDtypeStruct((s_len, NSA_QW), F32),
        scratch_shapes=[pltpu.VMEM((QBLOCK, nselp), F32), pltpu.VMEM((rows, HEAD_DIM), BF16),
                        pltpu.VMEM((rows, ncp), F32), pltpu.VMEM((rows, ncp), F32), pltpu.VMEM((rows, ncp), BF16),
                        pltpu.VMEM((FLASH_UNROLL, QBLOCK, tk), F32), pltpu.VMEM((FLASH_UNROLL + 1, tk, 2 * HEAD_DIM), BF16),
                        pltpu.VMEM((FLASH_UNROLL, rows, tk), F32), pltpu.VMEM((FLASH_UNROLL, rows, tk), BF16),
                        pltpu.VMEM((FLASH_UNROLL, rows, LANES), F32), pltpu.VMEM((rows, LANES), F32),
                        pltpu.VMEM((rows, 2 * HEAD_DIM), F32)],
        compiler_params=_params(("parallel", "arbitrary")),
        name="nsa_cmp_sel",
    )(qkv, ks_t, qkv, kv_cmp, kv_cmp, covt, e0, gates)


def _win_body(q_ref, kw_ref, vw_ref, gate_ref, u_ref, o_ref, *, span):
    n = pl.program_id(1)
    rows = NSA_GROUP * QBLOCK
    q4 = _stack_heads(q_ref[...])
    start = pl.multiple_of(jnp.maximum(n * QBLOCK - (span - QBLOCK), 0), QBLOCK)
    s = lax.dot_general(q4, kw_ref[pl.ds(start, span), :], NT_DIMS, preferred_element_type=F32)
    qpos = n * QBLOCK + (lax.broadcasted_iota(jnp.int32, (rows, span), 0) & (QBLOCK - 1))
    delta = qpos - (start + lax.broadcasted_iota(jnp.int32, (rows, span), 1))
    s = jnp.where((delta >= 0) & (delta <= NSA_WINDOW - 1), s, NEG_INF)
    m = jnp.max(s, axis=-1, keepdims=True)
    p = jnp.exp2(s - m)
    l = jnp.sum(p, axis=-1, keepdims=True)
    o_w = jnp.dot(p.astype(BF16), vw_ref[pl.ds(start, span), :], preferred_element_type=F32) / l
    gates = jax.nn.sigmoid(gate_ref[...])
    for g in range(NSA_GROUP):
        sl = slice(g * HEAD_DIM, (g + 1) * HEAD_DIM)
        o_ref[:, sl] = (u_ref[:, sl] + gates[:, 3 * g + 2:3 * g + 3] * o_w[g * QBLOCK:(g + 1) * QBLOCK]
                        ).astype(o_ref.dtype)


def nsa_window(qkv, s_len, q_blk, kw_blk, vw_blk, gates, u):
    nb = s_len // QBLOCK
    span = (-(-(NSA_WINDOW - 1) // QBLOCK) + 1) * QBLOCK
    gw = NSA_GROUP * HEAD_DIM
    return pl.pallas_call(
        functools.partial(_win_body, span=span),
        grid=(NSA_KV_HEADS, nb),
        in_specs=[
            pl.BlockSpec((QBLOCK, gw), lambda h, n: (n, q_blk + h)),
            pl.BlockSpec((s_len, HEAD_DIM), lambda h, n: (0, kw_blk + h)),
            pl.BlockSpec((s_len, HEAD_DIM), lambda h, n: (0, vw_blk + h)),
            pl.BlockSpec((None, QBLOCK, 3 * NSA_GROUP), lambda h, n: (h, n, 0)),
            pl.BlockSpec((QBLOCK, gw), lambda h, n: (n, h)),
        ],
        out_specs=pl.BlockSpec((QBLOCK, gw), lambda h, n: (n, h)),
        out_shape=jax.ShapeDtypeStruct((s_len, NSA_QW), BF16),
        compiler_params=_params(("parallel", "parallel")),
        name="nsa_window",
    )(qkv, qkv, qkv, gates, u)


def _softmax_rows(s_ref, p_ref, a_ref, m_scr, mask_fn):
    rows, tk = s_ref.shape
    nrep = tk // LANES
    chunk = FLASH_CHUNK // tk
    for r in range(0, rows, chunk):
        rs = slice(r, r + chunk)
        s = s_ref[rs, :]
        if mask_fn is not None:
            s = mask_fn(r, s)
        m_old = m_scr[rs, :]
        m_new = jnp.maximum(m_old, jnp.max(s, axis=1, keepdims=True))
        p_ref[rs, :] = jnp.exp2(s - jnp.concatenate([m_new] * nrep, axis=1)).astype(BF16)
        a_ref[rs, :] = jnp.exp2(m_old - m_new)
        m_scr[rs, :] = m_new


def _flash_pipeline(n_full, q_ref, k_tile, v_tile, mask_for, s_scr, p_scr, a_scr, m_scr, acc_scr):
    def scores(t, idx, buf):
        s_scr[idx] = jnp.dot(q_ref[...], k_tile(t, idx, buf), preferred_element_type=F32)

    def softmax(idx, t, masked):
        _softmax_rows(s_scr.at[idx], p_scr.at[idx], a_scr.at[idx], m_scr, mask_for(idx, t, masked))

    def values(idx, t, buf):
        a = a_scr[idx]
        acc_scr[...] = (jnp.concatenate([a, a], axis=1) * acc_scr[...]
                        + jnp.dot(p_scr[idx], v_tile(t, buf), preferred_element_type=F32))

    last = FLASH_UNROLL - 1
    p_scr[last] = jnp.zeros(p_scr.shape[1:], p_scr.dtype)
    a_scr[last] = jnp.ones(a_scr.shape[1:], a_scr.dtype)
    scores(0, 0, 0)

    def run(t0, count, masked_last, lookahead):
        for j in range(count):
            t = t0 + j
            if lookahead or j + 1 < count:
                scores(t + 1, (j + 1) % FLASH_UNROLL, j)
            softmax(j, t, masked_last and j == count - 1)
            values((j - 1) % FLASH_UNROLL, jnp.maximum(t - 1, 0), j)
        if not lookahead:
            values(count - 1, t0 + count - 1, count)

    def body(u, carry):
        run(FLASH_UNROLL * u, FLASH_UNROLL, False, True)
        return carry

    lax.fori_loop(0, n_full // FLASH_UNROLL, body, 0)
    rem = n_full % FLASH_UNROLL
    for r in range(FLASH_UNROLL):
        @pl.when(rem == r)
        def _(r=r):
            run(n_full - r, r + 1, True, False)


def _flash_init(m_scr, acc_scr, ve_scr):
    dv = ve_scr.shape[2] // 2
    m_scr[...] = jnp.full(m_scr.shape, NEG_INF, F32)
    acc_scr[...] = jnp.zeros(acc_scr.shape, F32)
    ve_scr[:, :, dv:] = jnp.ones((ve_scr.shape[0], ve_scr.shape[1], dv), ve_scr.dtype)


def _flash_output(acc_scr):
    dv = acc_scr.shape[1] // 2
    return acc_scr[:, 0:dv] / acc_scr[:, dv:]


def _mla_body(q_ref, knt_ref, krt_ref, v_ref, o_ref, kt_scr, ve_scr, s_scr, p_scr, a_scr, m_scr, acc_scr,
              *, tq, tk):
    qi = pl.program_id(1)
    _flash_init(m_scr, acc_scr, ve_scr)

    def mask_for(slot, t, masked):
        if not masked:
            return None

        def causal(r, s):
            qpos = qi * tq + r + lax.broadcasted_iota(jnp.int32, s.shape, 0)
            kpos = t * tk + lax.broadcasted_iota(jnp.int32, s.shape, 1)
            return jnp.where(kpos <= qpos, s, NEG_INF)
        return causal

    def k_tile(t, slot, buf):
        k0 = pl.multiple_of(t * tk, tk)
        kt_scr[buf, 0:MLA_NOPE_DIM, :] = knt_ref[:, pl.ds(k0, tk)]
        kt_scr[buf, MLA_NOPE_DIM:, :] = krt_ref[:, pl.ds(k0, tk)]
        return kt_scr[buf]

    def v_tile(t, buf):
        ve_scr[buf, :, 0:MLA_V_DIM] = v_ref[pl.ds(pl.multiple_of(t * tk, tk), tk), :]
        return ve_scr[buf]

    _flash_pipeline((qi * tq) // tk, q_ref, k_tile, v_tile, mask_for, s_scr, p_scr, a_scr, m_scr, acc_scr)
    o_ref[...] = _flash_output(acc_scr).astype(o_ref.dtype)


def mla_attention(q, kv, k_nope_t, k_rope_t, s_len, tq=512, tk=512):
    tq, tk = min(tq, s_len), min(tk, s_len)
    qw = 2 * LANES
    return pl.pallas_call(
        functools.partial(_mla_body, tq=tq, tk=tk),
        grid=(MLA_HEADS, s_len // tq),
        in_specs=[
            pl.BlockSpec((tq, qw), lambda h, i: (i, h)),
            pl.BlockSpec((MLA_NOPE_DIM, s_len), lambda h, i: (h, 0)),
            pl.BlockSpec((LANES, s_len), lambda h, i: (0, 0)),
            pl.BlockSpec((s_len, MLA_V_DIM), lambda h, i: (0, MLA_HEADS + h)),
        ],
        out_specs=pl.BlockSpec((tq, MLA_V_DIM), lambda h, i: (i, h)),
        out_shape=jax.ShapeDtypeStruct((s_len, MLA_HEADS * MLA_V_DIM), BF16),
        scratch_shapes=[pltpu.VMEM((FLASH_UNROLL, qw, tk), BF16),
                        pltpu.VMEM((FLASH_UNROLL + 1, tk, 2 * MLA_V_DIM), BF16),
                        pltpu.VMEM((FLASH_UNROLL, tq, tk), F32), pltpu.VMEM((FLASH_UNROLL, tq, tk), BF16),
                        pltpu.VMEM((FLASH_UNROLL, tq, LANES), F32), pltpu.VMEM((tq, LANES), F32),
                        pltpu.VMEM((tq, 2 * MLA_V_DIM), F32)],
        compiler_params=_params(("parallel", "parallel")),
        name="mla_attention",
    )(q, k_nope_t, k_rope_t, kv)


_IN_SIZES = (DSW_W, DSW_W, DSW_W, NSA_QW, NSA_KVW, NSA_KVW, NSA_KVW, NSA_KVW, NSA_KVW, NSA_KVW,
             3 * NSA_HEADS, MLA_Q_LORA, MLA_KV_LORA, MLA_ROPE_DIM)
_IN_NAMES = ("a_q", "a_k", "a_v", "n_q", "n_kc", "n_vc", "n_ks", "n_vs", "n_kw", "n_vw", "n_gate",
             "m_cq", "m_ckv", "m_kr")
_IN_OFF = dict(zip(_IN_NAMES, np.concatenate([[0], np.cumsum(_IN_SIZES)[:-1]]).tolist()))
_IN_LEN = dict(zip(_IN_NAMES, _IN_SIZES))

_B_ORDER = ("a_q", "a_k", "n_q", "n_ks", "n_kw", "n_vs", "n_vw", "a_v")
_B_ROPE = 2 * DSW_W + NSA_QW + 2 * NSA_KVW
_B_WIDTH = sum(_IN_LEN[k] for k in _B_ORDER)
_B_COL = dict(zip(_B_ORDER, np.concatenate([[0], np.cumsum([_IN_LEN[k] for k in _B_ORDER])[:-1]]).tolist()))
_F_COL = {"m_cq": 0, "m_ckv": MLA_Q_LORA, "n_kc": 2048, "n_vc": 2304, "m_kr": 2560, "n_gate": 2688}
_F_WIDTH = 3072
_HALF_ROPE = MLA_ROPE_DIM // 2


def _prep_w_in_body(w_ref, wb_ref, wf_ref):
    for name in _B_ORDER:
        wb_ref[:, _B_COL[name]:_B_COL[name] + _IN_LEN[name]] = (
            w_ref[:, _IN_OFF[name]:_IN_OFF[name] + _IN_LEN[name]].astype(BF16))
    wf_ref[...] = jnp.zeros(wf_ref.shape, BF16)
    for name in ("m_cq", "m_ckv", "n_kc", "n_vc", "n_gate"):
        wf_ref[:, _F_COL[name]:_F_COL[name] + _IN_LEN[name]] = (
            w_ref[:, _IN_OFF[name]:_IN_OFF[name] + _IN_LEN[name]].astype(BF16))
    kr_src, kr_dst = _IN_OFF["m_kr"], _F_COL["m_kr"]
    for half in range(2):
        wf_ref[:, kr_dst + 2 * half * _HALF_ROPE:kr_dst + (2 * half + 1) * _HALF_ROPE] = (
            w_ref[:, kr_src + half * _HALF_ROPE:kr_src + (half + 1) * _HALF_ROPE].astype(BF16))


def _prep_w_in(w, layer, tm=256):
    nl, k, n = w.shape
    return pl.pallas_call(
        _prep_w_in_body,
        grid=(k // tm,),
        in_specs=[pl.BlockSpec((None, tm, n), lambda i: (layer, i, 0))],
        out_specs=[pl.BlockSpec((tm, _B_WIDTH), lambda i: (i, 0)), pl.BlockSpec((tm, _F_WIDTH), lambda i: (i, 0))],
        out_shape=[jax.ShapeDtypeStruct((k, _B_WIDTH), BF16), jax.ShapeDtypeStruct((k, _F_WIDTH), BF16)],
        compiler_params=_params(("parallel",)),
        name="prep_w_in",
    )(w)


_B_MODES = ([2] * (DSW_W // LANES) + [1] * (DSW_W // LANES) + [2] * (NSA_QW // LANES)
            + [1] * (2 * NSA_KVW // LANES) + [0] * ((_B_WIDTH - _B_ROPE) // LANES))
_F_MODES = [0] * 16 + [1, 1, 0, 0, 2] + [0] * 3
_Q_MODES = [0, 1] * MLA_HEADS
C_HEAD = HEAD_DIM ** -0.5 * LOG2E
C_MLA = (MLA_NOPE_DIM + MLA_ROPE_DIM) ** -0.5 * LOG2E


def _prep_w_uq(w):
    w = w.reshape(MLA_Q_LORA, MLA_HEADS, MLA_NOPE_DIM + MLA_ROPE_DIM)
    z = jnp.zeros((MLA_Q_LORA, MLA_HEADS, _HALF_ROPE), w.dtype)
    w = jnp.concatenate([w[..., :MLA_NOPE_DIM], w[..., MLA_NOPE_DIM:MLA_NOPE_DIM + _HALF_ROPE], z,
                         w[..., MLA_NOPE_DIM + _HALF_ROPE:], z], axis=-1)
    return w.reshape(MLA_Q_LORA, MLA_HEADS * 2 * LANES).astype(BF16)


def _prep_w_ukv(w):
    w = w.reshape(MLA_KV_LORA, MLA_HEADS, 2, MLA_NOPE_DIM).transpose(0, 2, 1, 3)
    return w.reshape(MLA_KV_LORA, 2 * MLA_HEADS * MLA_NOPE_DIM).astype(BF16)


def _rope_tables(positions):
    pos = positions.astype(F32)[:, None]

    def cs(dim):
        inv = ROPE_THETA ** (-jnp.arange(0, dim, 2, dtype=F32) / dim)
        ang = pos * inv
        return jnp.cos(ang), jnp.sin(ang)

    c128, s128 = cs(HEAD_DIM)
    c64, s64 = cs(MLA_ROPE_DIM)
    z = jnp.zeros_like(c64)
    one = jnp.ones((pos.shape[0], LANES), F32)
    zero = jnp.zeros((pos.shape[0], LANES), F32)
    a128, b128 = jnp.concatenate([c128, c128], 1), jnp.concatenate([-s128, s128], 1)
    a64, b64 = jnp.concatenate([c64, z, c64, z], 1), jnp.concatenate([-s64, z, s64, z], 1)
    return {"b": (jnp.stack([one, a128, a128 * C_HEAD]), jnp.stack([zero, b128, b128 * C_HEAD])),
            "f": (jnp.stack([one, a128, a64]), jnp.stack([zero, b128, b64])),
            "q": (jnp.stack([one * C_MLA, a64 * C_MLA]), jnp.stack([zero, b64 * C_MLA]))}


def _mixer(x, sc1, sh1, tabs, layer, w_in, cmp_pe, cmp_w1, cmp_w2, q_norm, kv_norm, w_uq, w_ukv):
    s_len = x.shape[0]
    wb, wf = _prep_w_in(w_in, layer)
    pb = mm_fullk(x, 0, D_MODEL, wb, prologue="mod", p1=sc1, p2=sh1, epilogue="rope", modes=_B_MODES,
                  ta=tabs["b"][0], tb=tabs["b"][1], out_dtype=BF16)
    pf = mm_fullk(x, 0, D_MODEL, wf, prologue="mod", p1=sc1, p2=sh1, epilogue="rope", modes=_F_MODES,
                  ta=tabs["f"][0], tb=tabs["f"][1], out_dtype=F32)

    blk = lambda name: _B_COL[name] // HEAD_DIM
    out_a = dilated_attention(pb, s_len, blk("a_q"), blk("a_k"), blk("a_v"))

    ncp = s_len // NSA_CMP_STRIDE

    def blocks16(name):
        t = pf[:, _F_COL[name]:_F_COL[name] + NSA_KVW]
        return t.reshape(ncp, NSA_CMP_STRIDE, NSA_KV_HEADS, HEAD_DIM).transpose(2, 0, 1, 3).reshape(
            NSA_KV_HEADS, ncp, NSA_CMP_STRIDE * HEAD_DIM)

    kv_cmp = nsa_compress(jnp.stack([blocks16("n_kc"), blocks16("n_vc")]), cmp_pe,
                          cmp_w1.astype(BF16), cmp_w2.astype(BF16))
    gates = pf[:, _F_COL["n_gate"]:_F_COL["n_gate"] + 3 * NSA_HEADS]
    gates = gates.reshape(s_len, NSA_KV_HEADS, 3 * NSA_GROUP).transpose(1, 0, 2)
    gw = NSA_GROUP * HEAD_DIM
    ks_t = pb[:, _B_COL["n_ks"]:_B_COL["n_ks"] + NSA_KVW].T
    u = nsa_cmp_sel(pb, ks_t, s_len, _B_COL["n_q"] // gw, _B_COL["n_vs"] // HEAD_DIM, kv_cmp, gates)
    out_b = nsa_window(pb, s_len, _B_COL["n_q"] // gw, _B_COL["n_kw"] // HEAD_DIM, _B_COL["n_vw"] // HEAD_DIM,
                       gates, u)

    q = mm_fullk(pf, 0, MLA_Q_LORA, _prep_w_uq(w_uq), prologue="rms", p1=q_norm, epilogue="rope",
                 modes=_Q_MODES, ta=tabs["q"][0], tb=tabs["q"][1], out_dtype=BF16)
    kv = mm_fullk(pf, _F_COL["m_ckv"] // MLA_KV_LORA, MLA_KV_LORA, _prep_w_ukv(w_ukv), prologue="rms",
                  p1=kv_norm, out_dtype=BF16)
    k_rope_t = pf[:, _F_COL["m_kr"]:_F_COL["m_kr"] + LANES].astype(BF16).T
    k_nope_t = kv[:, :MLA_HEADS * MLA_NOPE_DIM].T
    out_c = mla_attention(q, kv, k_nope_t, k_rope_t, s_len)
    return jnp.concatenate([out_a, out_b, out_c], axis=1)


def kernel(x, c, positions, w_ada, b_ada, w_in, nsa_cmp_pe, nsa_cmp_w1, nsa_cmp_w2, mla_q_norm, mla_kv_norm,
           mla_w_uq, mla_w_ukv, w_out, ln1_g, ln1_b, mlp_w1, mlp_w2, ln2_g, ln2_b):
    assert x.shape[0] == 1, "kernel handles batch size 1"
    xs = x[0]
    d = xs.shape[1]
    tabs = _rope_tables(positions[0])
    w_out16, mlp_w1_16, mlp_w2_16 = w_out.astype(BF16), mlp_w1.astype(BF16), mlp_w2.astype(BF16)
    for l in range(DEPTH):
        mod = adaln(c, w_ada, b_ada, l)
        sh1, sc1, g1, sh2, sc2, g2 = [mod[:, i * d:(i + 1) * d] for i in range(6)]
        mixed = _mixer(xs, sc1, sh1, tabs, l, w_in, nsa_cmp_pe[l], nsa_cmp_w1[l], nsa_cmp_w2[l],
                       mla_q_norm[l], mla_kv_norm[l], mla_w_uq[l], mla_w_ukv[l])
        xs = mm_ln(mixed, w_out16, l, xs, g1, ln1_g[l], ln1_b[l])
        act = mm_fullk(xs, 0, d, mlp_w1_16, prologue="mod", p1=sc2, p2=sh2, epilogue="relu2", out_dtype=BF16,
                       w_layer=l)
        xs = mm_ln(act, mlp_w2_16, l, xs, g2, ln2_g[l], ln2_b[l])
    return xs[None]
```

```python
import functools

import jax
import jax.numpy as jnp
import numpy as np
from jax import lax
from jax.experimental import pallas as pl
from jax.experimental.pallas import tpu as pltpu

D_MODEL = 4096
DEPTH = 2
HEAD_DIM = 128
ROPE_THETA = 10000.0
QBLOCK = 128
DSW_HEADS = 8
DSW_PATTERNS = ((128, 1), (512, 4), (2048, 16))
NSA_HEADS = 8
NSA_KV_HEADS = 2
NSA_GROUP = NSA_HEADS // NSA_KV_HEADS
NSA_CMP_LEN = 32
NSA_CMP_STRIDE = 16
NSA_CMP_HIDDEN = 256
NSA_SEL_BLOCK = 64
NSA_TOP_N = 16
NSA_WINDOW = 512
MLA_HEADS = 16
MLA_Q_LORA = 1536
MLA_KV_LORA = 512
MLA_NOPE_DIM = 128
MLA_ROPE_DIM = 64
MLA_V_DIM = 128
D_FF = 4 * D_MODEL
ALPHA = (2 * DEPTH) ** 0.25

DSW_W = DSW_HEADS * HEAD_DIM
NSA_QW = NSA_HEADS * HEAD_DIM
NSA_KVW = NSA_KV_HEADS * HEAD_DIM

LANES = 128
V7X_VMEM_LIMIT = 56 * 1024 * 1024
V7X_VMEM_LIMIT_LARGE = 60 * 1024 * 1024

F32 = jnp.float32
BF16 = jnp.bfloat16
NEG_INF = float("-inf")
LOG2E = 1.4426950408889634
FLASH_CHUNK = 64 * 512
FLASH_UNROLL = 4
DIL_SUPER = QBLOCK * max(d for _, d in DSW_PATTERNS)
DIL_MIX_ROWS = 64
NT_DIMS = (((1,), (1,)), ((), ()))


def _params(sem, vmem=V7X_VMEM_LIMIT):
    return pltpu.CompilerParams(dimension_semantics=sem, vmem_limit_bytes=vmem)


def _adaln_body(cb_ref, w_ref, b_ref, o_ref, *, tn):
    cb = cb_ref[...]
    for s in range(tn // LANES):
        sl = slice(s * LANES, (s + 1) * LANES)
        o_ref[:, sl] = jnp.sum(w_ref[:, sl] * cb, axis=0, keepdims=True) + b_ref[:, sl]


def adaln(c, w, b, layer, tn=512):
    nl, d, n = w.shape
    cb = jnp.broadcast_to(c.reshape(d, 1), (d, LANES))
    return pl.pallas_call(
        functools.partial(_adaln_body, tn=tn),
        grid=(n // tn,),
        in_specs=[
            pl.BlockSpec((d, LANES), lambda j: (0, 0)),
            pl.BlockSpec((None, d, tn), lambda j: (layer, 0, j)),
            pl.BlockSpec((None, 1, tn), lambda j: (layer, 0, j)),
        ],
        out_specs=pl.BlockSpec((1, tn), lambda j: (0, j)),
        out_shape=jax.ShapeDtypeStruct((1, n), F32),
        compiler_params=_params(("arbitrary",)),
        name="adaln",
    )(cb, w, b.reshape(nl, 1, n))


def _mm_fullk_body(*refs, prologue, epilogue, nsub, ksplit):
    refs = list(refs)
    mode_ref = refs.pop(0) if epilogue == "rope" else None
    x_ref, p1_ref = refs.pop(0), refs.pop(0)
    p2_ref = refs.pop(0) if prologue == "mod" else None
    w_ref = refs.pop(0)
    if epilogue == "rope":
        ta_ref, tb_ref = refs.pop(0), refs.pop(0)
    if epilogue == "ksplit":
        o_ref, kt_ref, h_scr = refs
    else:
        o_ref, h_scr = refs
    j = pl.program_id(1)

    @pl.when(j == 0)
    def _():
        x = x_ref[...]
        if prologue == "mod":
            h = x * (1.0 + p1_ref[...]) + p2_ref[...]
        else:
            h = x * lax.rsqrt(jnp.mean(x * x, axis=-1, keepdims=True) + 1e-6) * p1_ref[...]
        h_scr[...] = h.astype(BF16)

    acc = jnp.dot(h_scr[...], w_ref[...], preferred_element_type=F32)
    if epilogue == "relu2":
        r = jnp.maximum(acc, 0.0)
        o_ref[...] = (r * r).astype(o_ref.dtype)
    elif epilogue == "rope":
        for s in range(nsub):
            sl = slice(s * LANES, (s + 1) * LANES)
            md = mode_ref[j * nsub + s]
            sub = acc[:, sl]
            o_ref[:, sl] = (sub * ta_ref[md] + pltpu.roll(sub, LANES // 2, 1) * tb_ref[md]).astype(o_ref.dtype)
    elif epilogue == "ksplit":
        @pl.when(j < ksplit)
        def _():
            kt_ref[...] = acc.T.astype(kt_ref.dtype)

        @pl.when(j >= ksplit)
        def _():
            o_ref[...] = acc.astype(o_ref.dtype)
    else:
        o_ref[...] = acc.astype(o_ref.dtype)


def mm_fullk(x, x_col_block, k, w, *, prologue, p1, p2=None, epilogue="none", modes=None, ta=None, tb=None,
             out_dtype=BF16, tm=512, tn=1024, w_layer=None, k_cols=0):
    m = x.shape[0]
    n = w.shape[-1]
    tm = min(tm, m)
    assert m % tm == 0 and n % tn == 0 and w.shape[-2] == k
    nsub = tn // LANES
    rope = epilogue == "rope"
    npre = 1 if rope else 0

    def im(f):
        return (lambda i, j, *_: f(i, j))

    in_specs = [pl.BlockSpec((tm, k), im(lambda i, j: (i, x_col_block))),
                pl.BlockSpec((1, k), im(lambda i, j: (0, 0)))]
    args = [x, p1.reshape(1, k)]
    if prologue == "mod":
        in_specs.append(pl.BlockSpec((1, k), im(lambda i, j: (0, 0))))
        args.append(p2.reshape(1, k))
    if w_layer is None:
        in_specs.append(pl.BlockSpec((k, tn), im(lambda i, j: (0, j))))
    else:
        in_specs.append(pl.BlockSpec((None, k, tn), im(lambda i, j: (w_layer, 0, j))))
    args.append(w)
    if rope:
        nmode = ta.shape[0]
        in_specs += [pl.BlockSpec((nmode, tm, LANES), im(lambda i, j: (0, i, 0)))] * 2
        args += [ta, tb]
    ksplit = k_cols // tn
    if epilogue == "ksplit":
        assert k_cols % tn == 0 and 0 < k_cols < n
        out_specs = [pl.BlockSpec((tm, tn), im(lambda i, j: (i, jnp.maximum(j - ksplit, 0)))),
                     pl.BlockSpec((tn, tm), im(lambda i, j: (jnp.minimum(j, ksplit - 1), i)))]
        out_shape = [jax.ShapeDtypeStruct((m, n - k_cols), out_dtype), jax.ShapeDtypeStruct((k_cols, m), out_dtype)]
    else:
        out_specs = pl.BlockSpec((tm, tn), im(lambda i, j: (i, j)))
        out_shape = jax.ShapeDtypeStruct((m, n), out_dtype)
    grid_spec = pltpu.PrefetchScalarGridSpec(
        num_scalar_prefetch=npre,
        grid=(m // tm, n // tn),
        in_specs=in_specs,
        out_specs=out_specs,
        scratch_shapes=[pltpu.VMEM((tm, k), BF16)],
    )
    fn = pl.pallas_call(
        functools.partial(_mm_fullk_body, prologue=prologue, epilogue=epilogue, nsub=nsub, ksplit=ksplit),
        grid_spec=grid_spec,
        out_shape=out_shape,
        compiler_params=_params(("parallel", "arbitrary")),
        name="mm_fullk_" + prologue + "_" + epilogue,
    )
    if rope:
        return fn(jnp.asarray(modes, jnp.int32), *args)
    return fn(*args)


def _mm_ln_body(*refs, nk, bounds):
    lhs_refs = refs[:len(bounds)]
    w_ref, x_ref, g_ref, lng_ref, lnb_ref, o_ref, acc_ref = refs[len(bounds):]
    kk = pl.program_id(1)

    def lhs():
        val = lhs_refs[-1][...]
        for ref, hi in zip(reversed(lhs_refs[:-1]), reversed(bounds[:-1])):
            val = jnp.where(kk < hi, ref[...], val)
        return val

    @pl.when(kk == 0)
    def _():
        acc_ref[...] = jnp.dot(lhs(), w_ref[...], preferred_element_type=F32)

    @pl.when(kk > 0)
    def _():
        acc_ref[...] += jnp.dot(lhs(), w_ref[...], preferred_element_type=F32)

    @pl.when(kk == nk - 1)
    def _():
        z = ALPHA * x_ref[...] + (1.0 + g_ref[...]) * acc_ref[...]
        mu = jnp.mean(z, axis=-1, keepdims=True)
        zc = z - mu
        var = jnp.mean(zc * zc, axis=-1, keepdims=True)
        o_ref[...] = zc * lax.rsqrt(var + 1e-5) * lng_ref[...] + lnb_ref[...]


def mm_ln(lhs_parts, w, w_layer, x, gate, ln_g, ln_b, tm=512, tk=512):
    m = lhs_parts[0].shape[0]
    k = sum(part.shape[1] for part in lhs_parts)
    n = w.shape[2]
    bounds, lhs_specs = [], []
    for part in lhs_parts:
        lo = bounds[-1] if bounds else 0
        hi = lo + part.shape[1] // tk
        bounds.append(hi)
        lhs_specs.append(pl.BlockSpec((tm, tk), lambda i, kk, lo=lo, hi=hi: (i, jnp.clip(kk - lo, 0, hi - lo - 1))))
    tm = min(tm, m)
    nk = k // tk
    row = lambda i, kk: (0, 0)
    return pl.pallas_call(
        functools.partial(_mm_ln_body, nk=nk, bounds=tuple(bounds)),
        grid=(m // tm, nk),
        in_specs=lhs_specs + [
            pl.BlockSpec((None, tk, n), lambda i, kk: (w_layer, kk, 0)),
            pl.BlockSpec((tm, n), lambda i, kk: (i, 0)),
            pl.BlockSpec((1, n), row), pl.BlockSpec((1, n), row), pl.BlockSpec((1, n), row),
        ],
        out_specs=pl.BlockSpec((tm, n), lambda i, kk: (i, 0)),
        out_shape=jax.ShapeDtypeStruct((m, n), F32),
        scratch_shapes=[pltpu.VMEM((tm, n), F32)],
        compiler_params=_params(("parallel", "arbitrary"), V7X_VMEM_LIMIT_LARGE),
        name="mm_ln",
    )(*lhs_parts, w, x, gate.reshape(1, n), ln_g.reshape(1, n), ln_b.reshape(1, n))


def _dilated_body(q_ref, kc_ref, kp_ref, vc_ref, vp_ref, o_ref, q32, kc32, kp32, vc32, vp32, od_scr, ld_scr):
    n = pl.program_id(1)
    q32[...] = q_ref[...].astype(F32)
    kc32[...] = kc_ref[...].astype(F32)
    kp32[...] = kp_ref[...].astype(F32)
    vc32[...] = vc_ref[...].astype(F32)
    vp32[...] = vp_ref[...].astype(F32)
    qi = lax.broadcasted_iota(jnp.int32, (QBLOCK, 2 * QBLOCK), 0)
    ki = lax.broadcasted_iota(jnp.int32, (QBLOCK, 2 * QBLOCK), 1)
    delta = QBLOCK + qi - ki
    in_band = (delta >= 0) & (delta <= QBLOCK)
    in_band_first = in_band & ((ki >= QBLOCK) | (n > 0))

    for pi, (window, dil) in enumerate(DSW_PATTERNS):
        assert window // dil == QBLOCK
        nblk = DIL_SUPER // (QBLOCK * dil)
        for mb in range(nblk):
            for r in range(dil):
                def rows(b):
                    return pl.ds(b * QBLOCK * dil + r, QBLOCK, stride=dil) if dil > 1 else pl.ds(b * QBLOCK, QBLOCK)

                cur = rows(mb)
                if mb > 0:
                    k_prev, v_prev, valid = kc32[rows(mb - 1), :], vc32[rows(mb - 1), :], in_band
                else:
                    k_prev, v_prev, valid = kp32[rows(nblk - 1), :], vp32[rows(nblk - 1), :], in_band_first
                k = jnp.concatenate([k_prev, kc32[cur, :]], axis=0).astype(BF16)
                v = jnp.concatenate([v_prev, vc32[cur, :]], axis=0).astype(BF16)
                s = lax.dot_general(q32[cur, :].astype(BF16), k, NT_DIMS, preferred_element_type=F32)
                s = jnp.where(valid, s, NEG_INF)
                m = jnp.max(s, axis=-1, keepdims=True)
                p = jnp.exp2(s - m)
                l = jnp.sum(p, axis=-1, keepdims=True)
                od_scr[pi, cur, :] = jnp.dot(p.astype(BF16), v, preferred_element_type=F32) / l
                ld_scr[pi, cur, :] = jnp.broadcast_to(m + jnp.log2(l), (QBLOCK, HEAD_DIM))

    npat = len(DSW_PATTERNS)
    for c0 in range(0, DIL_SUPER, DIL_MIX_ROWS):
        rs = slice(c0, c0 + DIL_MIX_ROWS)
        ls = [ld_scr[i, rs, :] for i in range(npat)]
        mx = functools.reduce(jnp.maximum, ls)
        ws = [jnp.exp2(t - mx) for t in ls]
        num = sum(ws[i] * od_scr[i, rs, :] for i in range(npat))
        o_ref[rs, :] = (num / sum(ws)).astype(o_ref.dtype)


def dilated_attention(qkv, s_len, q_blk, k_blk, v_blk):
    assert s_len % DIL_SUPER == 0
    cur = lambda blk: (lambda h, n: (n, blk + h))
    prev = lambda blk: (lambda h, n: (jnp.maximum(n - 1, 0), blk + h))
    bs = lambda f: pl.BlockSpec((DIL_SUPER, HEAD_DIM), f)
    f32_rows = pltpu.VMEM((DIL_SUPER, HEAD_DIM), F32)
    per_pattern = pltpu.VMEM((len(DSW_PATTERNS), DIL_SUPER, HEAD_DIM), F32)
    return pl.pallas_call(
        _dilated_body,
        grid=(DSW_HEADS, s_len // DIL_SUPER),
        in_specs=[bs(cur(q_blk)), bs(cur(k_blk)), bs(prev(k_blk)), bs(cur(v_blk)), bs(prev(v_blk))],
        out_specs=pl.BlockSpec((DIL_SUPER, HEAD_DIM), lambda h, n: (n, h)),
        out_shape=jax.ShapeDtypeStruct((s_len, DSW_W), BF16),
        scratch_shapes=[f32_rows] * 5 + [per_pattern] * 2,
        compiler_params=_params(("parallel", "parallel")),
        name="dilated_attention",
    )(qkv, qkv, qkv, qkv, qkv)


def _cmp_body(x_ref, pe_ref, w1_ref, w2_ref, o_ref, *, ncp):
    x = x_ref[...]
    half = NSA_CMP_STRIDE * HEAD_DIM
    a = jnp.dot((x + pe_ref[0:1, :]).astype(BF16), w1_ref[0:half, :], preferred_element_type=F32)
    b = jnp.dot((x + pe_ref[1:2, :]).astype(BF16), w1_ref[half:2 * half, :], preferred_element_type=F32)
    hid = a + pltpu.roll(b, ncp - 1, 0)
    act = jax.nn.gelu(hid)
    o_ref[...] = jnp.dot(act.astype(BF16), w2_ref[...], preferred_element_type=F32).astype(o_ref.dtype)


def nsa_compress(xs, pe, w1, w2):
    ncp = xs.shape[2]
    half = NSA_CMP_STRIDE * HEAD_DIM
    return pl.pallas_call(
        functools.partial(_cmp_body, ncp=ncp),
        grid=(2, NSA_KV_HEADS),
        in_specs=[
            pl.BlockSpec((None, None, ncp, half), lambda a, h: (a, h, 0, 0)),
            pl.BlockSpec((None, 2, half), lambda a, h: (a, 0, 0)),
            pl.BlockSpec((None, 2 * half, NSA_CMP_HIDDEN), lambda a, h: (a, 0, 0)),
            pl.BlockSpec((None, NSA_CMP_HIDDEN, HEAD_DIM), lambda a, h: (a, 0, 0)),
        ],
        out_specs=pl.BlockSpec((None, None, ncp, HEAD_DIM), lambda a, h: (a, h, 0, 0)),
        out_shape=jax.ShapeDtypeStruct((2, NSA_KV_HEADS, ncp, HEAD_DIM), BF16),
        compiler_params=_params(("parallel", "parallel")),
        name="nsa_compress",
    )(xs, pe.reshape(2, 2, half), w1, w2)


def _stack_heads(q):
    return jnp.concatenate([q[:, g * HEAD_DIM:(g + 1) * HEAD_DIM] for g in range(NSA_GROUP)], axis=0)


def _nsa_body(q_ref, kst_ref, vs_ref, kc_ref, vc_ref, covt_ref, e0_ref, gate_ref, u_ref,
              sel_scr, q4_scr, sc_scr, pf_scr, pc_scr, mk_scr, ve_scr, s_scr, p_scr, a_scr, m_scr, acc_scr,
              *, ncp, nselp, ntop, tk):
    n = pl.program_id(1)
    rows = NSA_GROUP * QBLOCK
    q4_scr[...] = _stack_heads(q_ref[...])
    q4 = q4_scr[...]

    sc_scr[...] = lax.dot_general(q4, kc_ref[...], NT_DIMS, preferred_element_type=F32)
    chunk = min(FLASH_CHUNK // ncp, rows)
    cend = lax.broadcasted_iota(jnp.int32, (chunk, ncp), 1) * NSA_CMP_STRIDE + (NSA_CMP_LEN - 1)
    for r in range(0, rows, chunk):
        rs = slice(r, r + chunk)
        qpos_r = n * QBLOCK + ((r + lax.broadcasted_iota(jnp.int32, (chunk, ncp), 0)) & (QBLOCK - 1))
        s = jnp.where(cend <= qpos_r, sc_scr[rs, :], NEG_INF)
        m = jnp.max(s, axis=-1, keepdims=True)
        e = jnp.exp2(s - jnp.where(m == NEG_INF, 0.0, m))
        l = jnp.sum(e, axis=-1, keepdims=True)
        p = e / jnp.where(l > 0, l, 1.0)
        pf_scr[rs, :] = p
        pc_scr[rs, :] = p.astype(BF16)
    o_c = jnp.dot(pc_scr[...], vc_ref[...], preferred_element_type=F32)

    psum = pf_scr[0:QBLOCK, :]
    for g in range(1, NSA_GROUP):
        psum = psum + pf_scr[g * QBLOCK:(g + 1) * QBLOCK, :]
    p_hi = psum.astype(BF16)
    p_lo = (psum - p_hi.astype(F32)).astype(BF16)
    covt = covt_ref[...]
    imp_t = (lax.dot_general(covt, p_hi, NT_DIMS, preferred_element_type=F32)
             + lax.dot_general(covt, p_lo, NT_DIMS, preferred_element_type=F32))

    jblk = lax.broadcasted_iota(jnp.int32, (nselp, QBLOCK), 0)
    qpos_c = n * QBLOCK + lax.broadcasted_iota(jnp.int32, (nselp, QBLOCK), 1)
    cur = qpos_c // NSA_SEL_BLOCK
    forced = (jblk == 0) | (jblk == cur) | (jblk == cur - 1)
    valid = jblk * NSA_SEL_BLOCK <= qpos_c
    score0 = jnp.where(valid, jnp.where(forced, jnp.inf, imp_t), NEG_INF)

    def pick(_, carry):
        score, picked = carry
        mx = jnp.max(score, axis=0, keepdims=True)
        idx = jnp.min(jnp.where(score == mx, jblk, nselp), axis=0, keepdims=True)
        hit = jblk == idx
        return jnp.where(hit, NEG_INF, score), jnp.where(hit, 1.0, picked)

    _, picked = lax.fori_loop(0, ntop, pick, (score0, jnp.zeros((nselp, QBLOCK), F32)))
    sel_scr[...] = jnp.where(valid, picked, 0.0).T

    blocks_per_tile = tk // NSA_SEL_BLOCK
    _flash_init(m_scr, acc_scr, ve_scr)

    def k_tile(t, slot, buf):
        shift = (nselp - t * blocks_per_tile) % nselp
        sel_t = pltpu.roll(sel_scr[...], shift, 1)[:, 0:LANES].astype(BF16)
        mk_scr[slot] = jnp.dot(sel_t, e0_ref[...], preferred_element_type=F32)
        return kst_ref[:, pl.ds(pl.multiple_of(t * tk, tk), tk)]

    def v_tile(t, buf):
        ve_scr[buf, :, 0:HEAD_DIM] = vs_ref[pl.ds(pl.multiple_of(t * tk, tk), tk), :]
        return ve_scr[buf]

    def mask_for(slot, t, masked):
        def mask_fn(r, s):
            rq = r % QBLOCK
            ok = mk_scr[slot, rq:rq + s.shape[0], :] > 0.5
            if masked:
                qpos = n * QBLOCK + rq + lax.broadcasted_iota(jnp.int32, s.shape, 0)
                ok = ok & (t * tk + lax.broadcasted_iota(jnp.int32, s.shape, 1) <= qpos)
            return jnp.where(ok, s, NEG_INF)
        return mask_fn

    _flash_pipeline((n * QBLOCK + QBLOCK - 1) // tk, q4_scr, k_tile, v_tile, mask_for,
                    s_scr, p_scr, a_scr, m_scr, acc_scr)
    o_s = _flash_output(acc_scr)

    gates = jax.nn.sigmoid(gate_ref[...])
    for g in range(NSA_GROUP):
        rs = slice(g * QBLOCK, (g + 1) * QBLOCK)
        u_ref[:, g * HEAD_DIM:(g + 1) * HEAD_DIM] = (gates[:, 3 * g:3 * g + 1] * o_c[rs]
                                                     + gates[:, 3 * g + 1:3 * g + 2] * o_s[rs])


def nsa_cmp_sel(qkv, ks_t, s_len, q_blk, vs_blk, kv_cmp, gates, tk=512):
    nb = s_len // QBLOCK
    ncp = s_len // NSA_CMP_STRIDE
    nsel = s_len // NSA_SEL_BLOCK
    nselp = -(-nsel // LANES) * LANES
    ntop = min(NSA_TOP_N, nsel)
    ci = np.arange(ncp)[None, :] * NSA_CMP_STRIDE
    sj = np.arange(nselp)[:, None] * NSA_SEL_BLOCK
    cov = (ci < sj + NSA_SEL_BLOCK) & (ci + NSA_CMP_LEN > sj) & (np.arange(ncp)[None, :] < ncp - 1) & (sj < s_len)
    covt = jnp.asarray(cov.astype(np.float32), BF16)
    e0 = jnp.asarray((np.arange(tk)[None, :] // NSA_SEL_BLOCK == np.arange(LANES)[:, None]).astype(np.float32), BF16)
    gw = NSA_GROUP * HEAD_DIM
    rows = NSA_GROUP * QBLOCK
    return pl.pallas_call(
        functools.partial(_nsa_body, ncp=ncp, nselp=nselp, ntop=ntop, tk=tk),
        grid=(NSA_KV_HEADS, nb),
        in_specs=[
            pl.BlockSpec((QBLOCK, gw), lambda h, n: (n, q_blk + h)),
            pl.BlockSpec((HEAD_DIM, s_len), lambda h, n: (h, 0)),
            pl.BlockSpec((s_len, HEAD_DIM), lambda h, n: (0, vs_blk + h)),
            pl.BlockSpec((None, None, ncp, HEAD_DIM), lambda h, n: (0, h, 0, 0)),
            pl.BlockSpec((None, None, ncp, HEAD_DIM), lambda h, n: (1, h, 0, 0)),
            pl.BlockSpec((nselp, ncp), lambda h, n: (0, 0)),
            pl.BlockSpec((LANES, tk), lambda h, n: (0, 0)),
            pl.BlockSpec((None, QBLOCK, 3 * NSA_GROUP), lambda h, n: (h, n, 0)),
        ],
        out_specs=pl.BlockSpec((QBLOCK, gw), lambda h, n: (n, h)),
        out_shape=jax.ShapeDtypeStruct((s_len, NSA_QW), F32),
        scratch_shapes=[pltpu.VMEM((QBLOCK, nselp), F32), pltpu.VMEM((rows, HEAD_DIM), BF16),
                        pltpu.VMEM((rows, ncp), F32), pltpu.VMEM((rows, ncp), F32), pltpu.VMEM((rows, ncp), BF16),
                        pltpu.VMEM((FLASH_UNROLL, QBLOCK, tk), F32), pltpu.VMEM((FLASH_UNROLL + 1, tk, 2 * HEAD_DIM), BF16),
                        pltpu.VMEM((FLASH_UNROLL, rows, tk), F32), pltpu.VMEM((FLASH_UNROLL, rows, tk), BF16),
                        pltpu.VMEM((FLASH_UNROLL, rows, LANES), F32), pltpu.VMEM((rows, LANES), F32),
                        pltpu.VMEM((rows, 2 * HEAD_DIM), F32)],
        compiler_params=_params(("parallel", "arbitrary")),
        name="nsa_cmp_sel",
    )(qkv, ks_t, qkv, kv_cmp, kv_cmp, covt, e0, gates)


def _win_body(q_ref, kw_ref, vw_ref, gate_ref, u_ref, o_ref, *, span):
    n = pl.program_id(1)
    rows = NSA_GROUP * QBLOCK
    q4 = _stack_heads(q_ref[...])
    start = pl.multiple_of(jnp.maximum(n * QBLOCK - (span - QBLOCK), 0), QBLOCK)
    s = lax.dot_general(q4, kw_ref[pl.ds(start, span), :], NT_DIMS, preferred_element_type=F32)
    qpos = n * QBLOCK + (lax.broadcasted_iota(jnp.int32, (rows, span), 0) & (QBLOCK - 1))
    delta = qpos - (start + lax.broadcasted_iota(jnp.int32, (rows, span), 1))
    s = jnp.where((delta >= 0) & (delta <= NSA_WINDOW - 1), s, NEG_INF)
    m = jnp.max(s, axis=-1, keepdims=True)
    p = jnp.exp2(s - m)
    l = jnp.sum(p, axis=-1, keepdims=True)
    o_w = jnp.dot(p.astype(BF16), vw_ref[pl.ds(start, span), :], preferred_element_type=F32) / l
    gates = jax.nn.sigmoid(gate_ref[...])
    for g in range(NSA_GROUP):
        sl = slice(g * HEAD_DIM, (g + 1) * HEAD_DIM)
        o_ref[:, sl] = (u_ref[:, sl] + gates[:, 3 * g + 2:3 * g + 3] * o_w[g * QBLOCK:(g + 1) * QBLOCK]
                        ).astype(o_ref.dtype)


def nsa_window(qkv, s_len, q_blk, kw_blk, vw_blk, gates, u):
    nb = s_len // QBLOCK
    span = (-(-(NSA_WINDOW - 1) // QBLOCK) + 1) * QBLOCK
    gw = NSA_GROUP * HEAD_DIM
    return pl.pallas_call(
        functools.partial(_win_body, span=span),
        grid=(NSA_KV_HEADS, nb),
        in_specs=[
            pl.BlockSpec((QBLOCK, gw), lambda h, n: (n, q_blk + h)),
            pl.BlockSpec((s_len, HEAD_DIM), lambda h, n: (0, kw_blk + h)),
            pl.BlockSpec((s_len, HEAD_DIM), lambda h, n: (0, vw_blk + h)),
            pl.BlockSpec((None, QBLOCK, 3 * NSA_GROUP), lambda h, n: (h, n, 0)),
            pl.BlockSpec((QBLOCK, gw), lambda h, n: (n, h)),
        ],
        out_specs=pl.BlockSpec((QBLOCK, gw), lambda h, n: (n, h)),
        out_shape=jax.ShapeDtypeStruct((s_len, NSA_QW), BF16),
        compiler_params=_params(("parallel", "parallel")),
        name="nsa_window",
    )(qkv, qkv, qkv, gates, u)


def _softmax_rows(s_ref, p_ref, a_ref, m_scr, mask_fn):
    rows, tk = s_ref.shape
    nrep = tk // LANES
    chunk = FLASH_CHUNK // tk
    for r in range(0, rows, chunk):
        rs = slice(r, r + chunk)
        s = s_ref[rs, :]
        if mask_fn is not None:
            s = mask_fn(r, s)
        m_old = m_scr[rs, :]
        m_new = jnp.maximum(m_old, jnp.max(s, axis=1, keepdims=True))
        p_ref[rs, :] = jnp.exp2(s - jnp.concatenate([m_new] * nrep, axis=1)).astype(BF16)
        a_ref[rs, :] = jnp.exp2(m_old - m_new)
        m_scr[rs, :] = m_new


def _flash_pipeline(n_full, q_ref, k_tile, v_tile, mask_for, s_scr, p_scr, a_scr, m_scr, acc_scr):
    def scores(t, idx, buf):
        s_scr[idx] = jnp.dot(q_ref[...], k_tile(t, idx, buf), preferred_element_type=F32)

    def softmax(idx, t, masked):
        _softmax_rows(s_scr.at[idx], p_scr.at[idx], a_scr.at[idx], m_scr, mask_for(idx, t, masked))

    def values(idx, t, buf):
        a = a_scr[idx]
        acc_scr[...] = (jnp.concatenate([a, a], axis=1) * acc_scr[...]
                        + jnp.dot(p_scr[idx], v_tile(t, buf), preferred_element_type=F32))

    last = FLASH_UNROLL - 1
    p_scr[last] = jnp.zeros(p_scr.shape[1:], p_scr.dtype)
    a_scr[last] = jnp.ones(a_scr.shape[1:], a_scr.dtype)
    scores(0, 0, 0)

    def run(t0, count, masked_last, lookahead):
        for j in range(count):
            t = t0 + j
            if lookahead or j + 1 < count:
                scores(t + 1, (j + 1) % FLASH_UNROLL, j)
            softmax(j, t, masked_last and j == count - 1)
            values((j - 1) % FLASH_UNROLL, jnp.maximum(t - 1, 0), j)
        if not lookahead:
            values(count - 1, t0 + count - 1, count)

    def body(u, carry):
        run(FLASH_UNROLL * u, FLASH_UNROLL, False, True)
        return carry

    lax.fori_loop(0, n_full // FLASH_UNROLL, body, 0)
    rem = n_full % FLASH_UNROLL
    for r in range(FLASH_UNROLL):
        @pl.when(rem == r)
        def _(r=r):
            run(n_full - r, r + 1, True, False)


def _flash_init(m_scr, acc_scr, ve_scr):
    dv = ve_scr.shape[2] // 2
    m_scr[...] = jnp.full(m_scr.shape, NEG_INF, F32)
    acc_scr[...] = jnp.zeros(acc_scr.shape, F32)
    ve_scr[:, :, dv:] = jnp.ones((ve_scr.shape[0], ve_scr.shape[1], dv), ve_scr.dtype)


def _flash_output(acc_scr):
    dv = acc_scr.shape[1] // 2
    return acc_scr[:, 0:dv] / acc_scr[:, dv:]


def _mla_body(q_ref, knt_ref, krt_ref, v_ref, o_ref, kt_scr, ve_scr, s_scr, p_scr, a_scr, m_scr, acc_scr,
              *, tq, tk):
    qi = pl.program_id(1)
    _flash_init(m_scr, acc_scr, ve_scr)

    def mask_for(slot, t, masked):
        if not masked:
            return None

        def causal(r, s):
            qpos = qi * tq + r + lax.broadcasted_iota(jnp.int32, s.shape, 0)
            kpos = t * tk + lax.broadcasted_iota(jnp.int32, s.shape, 1)
            return jnp.where(kpos <= qpos, s, NEG_INF)
        return causal

    def k_tile(t, slot, buf):
        k0 = pl.multiple_of(t * tk, tk)
        kt_scr[buf, 0:MLA_NOPE_DIM, :] = knt_ref[:, pl.ds(k0, tk)]
        kt_scr[buf, MLA_NOPE_DIM:, :] = krt_ref[:, pl.ds(k0, tk)]
        return kt_scr[buf]

    def v_tile(t, buf):
        ve_scr[buf, :, 0:MLA_V_DIM] = v_ref[pl.ds(pl.multiple_of(t * tk, tk), tk), :]
        return ve_scr[buf]

    _flash_pipeline((qi * tq) // tk, q_ref, k_tile, v_tile, mask_for, s_scr, p_scr, a_scr, m_scr, acc_scr)
    o_ref[...] = _flash_output(acc_scr).astype(o_ref.dtype)


def mla_attention(q, v, k_nope_t, k_rope_t, s_len, tq=512, tk=512):
    tq, tk = min(tq, s_len), min(tk, s_len)
    qw = 2 * LANES
    return pl.pallas_call(
        functools.partial(_mla_body, tq=tq, tk=tk),
        grid=(MLA_HEADS, s_len // tq),
        in_specs=[
            pl.BlockSpec((tq, qw), lambda h, i: (i, h)),
            pl.BlockSpec((MLA_NOPE_DIM, s_len), lambda h, i: (h, 0)),
            pl.BlockSpec((LANES, s_len), lambda h, i: (0, 0)),
            pl.BlockSpec((s_len, MLA_V_DIM), lambda h, i: (0, h)),
        ],
        out_specs=pl.BlockSpec((tq, MLA_V_DIM), lambda h, i: (i, h)),
        out_shape=jax.ShapeDtypeStruct((s_len, MLA_HEADS * MLA_V_DIM), BF16),
        scratch_shapes=[pltpu.VMEM((FLASH_UNROLL, qw, tk), BF16),
                        pltpu.VMEM((FLASH_UNROLL + 1, tk, 2 * MLA_V_DIM), BF16),
                        pltpu.VMEM((FLASH_UNROLL, tq, tk), F32), pltpu.VMEM((FLASH_UNROLL, tq, tk), BF16),
                        pltpu.VMEM((FLASH_UNROLL, tq, LANES), F32), pltpu.VMEM((tq, LANES), F32),
                        pltpu.VMEM((tq, 2 * MLA_V_DIM), F32)],
        compiler_params=_params(("parallel", "parallel")),
        name="mla_attention",
    )(q, k_nope_t, k_rope_t, v)


_IN_SIZES = (DSW_W, DSW_W, DSW_W, NSA_QW, NSA_KVW, NSA_KVW, NSA_KVW, NSA_KVW, NSA_KVW, NSA_KVW,
             3 * NSA_HEADS, MLA_Q_LORA, MLA_KV_LORA, MLA_ROPE_DIM)
_IN_NAMES = ("a_q", "a_k", "a_v", "n_q", "n_kc", "n_vc", "n_ks", "n_vs", "n_kw", "n_vw", "n_gate",
             "m_cq", "m_ckv", "m_kr")
_IN_OFF = dict(zip(_IN_NAMES, np.concatenate([[0], np.cumsum(_IN_SIZES)[:-1]]).tolist()))
_IN_LEN = dict(zip(_IN_NAMES, _IN_SIZES))

_B_ORDER = ("a_q", "a_k", "n_q", "n_ks", "n_kw", "n_vs", "n_vw", "a_v")
_B_ROPE = 2 * DSW_W + NSA_QW + 2 * NSA_KVW
_B_WIDTH = sum(_IN_LEN[k] for k in _B_ORDER)
_B_COL = dict(zip(_B_ORDER, np.concatenate([[0], np.cumsum([_IN_LEN[k] for k in _B_ORDER])[:-1]]).tolist()))
_F_COL = {"m_cq": 0, "m_ckv": MLA_Q_LORA, "n_kc": 2048, "n_vc": 2304, "m_kr": 2560, "n_gate": 2688}
_F_WIDTH = 3072
_HALF_ROPE = MLA_ROPE_DIM // 2


def _prep_w_in_body(w_ref, wb_ref, wf_ref):
    for name in _B_ORDER:
        wb_ref[:, _B_COL[name]:_B_COL[name] + _IN_LEN[name]] = (
            w_ref[:, _IN_OFF[name]:_IN_OFF[name] + _IN_LEN[name]].astype(BF16))
    wf_ref[...] = jnp.zeros(wf_ref.shape, BF16)
    for name in ("m_cq", "m_ckv", "n_kc", "n_vc", "n_gate"):
        wf_ref[:, _F_COL[name]:_F_COL[name] + _IN_LEN[name]] = (
            w_ref[:, _IN_OFF[name]:_IN_OFF[name] + _IN_LEN[name]].astype(BF16))
    kr_src, kr_dst = _IN_OFF["m_kr"], _F_COL["m_kr"]
    for half in range(2):
        wf_ref[:, kr_dst + 2 * half * _HALF_ROPE:kr_dst + (2 * half + 1) * _HALF_ROPE] = (
            w_ref[:, kr_src + half * _HALF_ROPE:kr_src + (half + 1) * _HALF_ROPE].astype(BF16))


def _prep_w_in(w, layer, tm=256):
    nl, k, n = w.shape
    return pl.pallas_call(
        _prep_w_in_body,
        grid=(k // tm,),
        in_specs=[pl.BlockSpec((None, tm, n), lambda i: (layer, i, 0))],
        out_specs=[pl.BlockSpec((tm, _B_WIDTH), lambda i: (i, 0)), pl.BlockSpec((tm, _F_WIDTH), lambda i: (i, 0))],
        out_shape=[jax.ShapeDtypeStruct((k, _B_WIDTH), BF16), jax.ShapeDtypeStruct((k, _F_WIDTH), BF16)],
        compiler_params=_params(("parallel",)),
        name="prep_w_in",
    )(w)


_B_MODES = ([2] * (DSW_W // LANES) + [1] * (DSW_W // LANES) + [2] * (NSA_QW // LANES)
            + [1] * (2 * NSA_KVW // LANES) + [0] * ((_B_WIDTH - _B_ROPE) // LANES))
_F_MODES = [0] * 16 + [1, 1, 0, 0, 2] + [0] * 3
_Q_MODES = [0, 1] * MLA_HEADS
C_HEAD = HEAD_DIM ** -0.5 * LOG2E
C_MLA = (MLA_NOPE_DIM + MLA_ROPE_DIM) ** -0.5 * LOG2E


def _prep_w_uq(w):
    w = w.reshape(MLA_Q_LORA, MLA_HEADS, MLA_NOPE_DIM + MLA_ROPE_DIM)
    z = jnp.zeros((MLA_Q_LORA, MLA_HEADS, _HALF_ROPE), w.dtype)
    w = jnp.concatenate([w[..., :MLA_NOPE_DIM], w[..., MLA_NOPE_DIM:MLA_NOPE_DIM + _HALF_ROPE], z,
                         w[..., MLA_NOPE_DIM + _HALF_ROPE:], z], axis=-1)
    return w.reshape(MLA_Q_LORA, MLA_HEADS * 2 * LANES).astype(BF16)


def _prep_w_ukv(w):
    w = w.reshape(MLA_KV_LORA, MLA_HEADS, 2, MLA_NOPE_DIM).transpose(0, 2, 1, 3)
    return w.reshape(MLA_KV_LORA, 2 * MLA_HEADS * MLA_NOPE_DIM).astype(BF16)


def _rope_tables(positions):
    pos = positions.astype(F32)[:, None]

    def cs(dim):
        inv = ROPE_THETA ** (-jnp.arange(0, dim, 2, dtype=F32) / dim)
        ang = pos * inv
        return jnp.cos(ang), jnp.sin(ang)

    c128, s128 = cs(HEAD_DIM)
    c64, s64 = cs(MLA_ROPE_DIM)
    z = jnp.zeros_like(c64)
    one = jnp.ones((pos.shape[0], LANES), F32)
    zero = jnp.zeros((pos.shape[0], LANES), F32)
    a128, b128 = jnp.concatenate([c128, c128], 1), jnp.concatenate([-s128, s128], 1)
    a64, b64 = jnp.concatenate([c64, z, c64, z], 1), jnp.concatenate([-s64, z, s64, z], 1)
    return {"b": (jnp.stack([one, a128, a128 * C_HEAD]), jnp.stack([zero, b128, b128 * C_HEAD])),
            "f": (jnp.stack([one, a128, a64]), jnp.stack([zero, b128, b64])),
            "q": (jnp.stack([one * C_MLA, a64 * C_MLA]), jnp.stack([zero, b64 * C_MLA]))}


def _mixer(x, sc1, sh1, tabs, layer, w_in, cmp_pe, cmp_w1, cmp_w2, q_norm, kv_norm, w_uq, w_ukv):
    s_len = x.shape[0]
    wb, wf = _prep_w_in(w_in, layer)
    pb = mm_fullk(x, 0, D_MODEL, wb, prologue="mod", p1=sc1, p2=sh1, epilogue="rope", modes=_B_MODES,
                  ta=tabs["b"][0], tb=tabs["b"][1], out_dtype=BF16)
    pf = mm_fullk(x, 0, D_MODEL, wf, prologue="mod", p1=sc1, p2=sh1, epilogue="rope", modes=_F_MODES,
                  ta=tabs["f"][0], tb=tabs["f"][1], out_dtype=F32)

    blk = lambda name: _B_COL[name] // HEAD_DIM
    out_a = dilated_attention(pb, s_len, blk("a_q"), blk("a_k"), blk("a_v"))

    ncp = s_len // NSA_CMP_STRIDE

    def blocks16(name):
        t = pf[:, _F_COL[name]:_F_COL[name] + NSA_KVW]
        return t.reshape(ncp, NSA_CMP_STRIDE, NSA_KV_HEADS, HEAD_DIM).transpose(2, 0, 1, 3).reshape(
            NSA_KV_HEADS, ncp, NSA_CMP_STRIDE * HEAD_DIM)

    kv_cmp = nsa_compress(jnp.stack([blocks16("n_kc"), blocks16("n_vc")]), cmp_pe,
                          cmp_w1.astype(BF16), cmp_w2.astype(BF16))
    gates = pf[:, _F_COL["n_gate"]:_F_COL["n_gate"] + 3 * NSA_HEADS]
    gates = gates.reshape(s_len, NSA_KV_HEADS, 3 * NSA_GROUP).transpose(1, 0, 2)
    gw = NSA_GROUP * HEAD_DIM
    ks_t = pb[:, _B_COL["n_ks"]:_B_COL["n_ks"] + NSA_KVW].T
    u = nsa_cmp_sel(pb, ks_t, s_len, _B_COL["n_q"] // gw, _B_COL["n_vs"] // HEAD_DIM, kv_cmp, gates)
    out_b = nsa_window(pb, s_len, _B_COL["n_q"] // gw, _B_COL["n_kw"] // HEAD_DIM, _B_COL["n_vw"] // HEAD_DIM,
                       gates, u)

    q = mm_fullk(pf, 0, MLA_Q_LORA, _prep_w_uq(w_uq), prologue="rms", p1=q_norm, epilogue="rope",
                 modes=_Q_MODES, ta=tabs["q"][0], tb=tabs["q"][1], out_dtype=BF16)
    v, k_nope_t = mm_fullk(pf, _F_COL["m_ckv"] // MLA_KV_LORA, MLA_KV_LORA, _prep_w_ukv(w_ukv), prologue="rms",
                           p1=kv_norm, epilogue="ksplit", k_cols=MLA_HEADS * MLA_NOPE_DIM, out_dtype=BF16)
    k_rope_t = pf[:, _F_COL["m_kr"]:_F_COL["m_kr"] + LANES].astype(BF16).T
    out_c = mla_attention(q, v, k_nope_t, k_rope_t, s_len)
    return [out_a, out_b, out_c]


def kernel(x, c, positions, w_ada, b_ada, w_in, nsa_cmp_pe, nsa_cmp_w1, nsa_cmp_w2, mla_q_norm, mla_kv_norm,
           mla_w_uq, mla_w_ukv, w_out, ln1_g, ln1_b, mlp_w1, mlp_w2, ln2_g, ln2_b):
    assert x.shape[0] == 1, "kernel handles batch size 1"
    xs = x[0]
    d = xs.shape[1]
    tabs = _rope_tables(positions[0])
    w_out16, mlp_w1_16, mlp_w2_16 = w_out.astype(BF16), mlp_w1.astype(BF16), mlp_w2.astype(BF16)
    for l in range(DEPTH):
        mod = adaln(c, w_ada, b_ada, l)
        sh1, sc1, g1, sh2, sc2, g2 = [mod[:, i * d:(i + 1) * d] for i in range(6)]
        mixed = _mixer(xs, sc1, sh1, tabs, l, w_in, nsa_cmp_pe[l], nsa_cmp_w1[l], nsa_cmp_w2[l],
                       mla_q_norm[l], mla_kv_norm[l], mla_w_uq[l], mla_w_ukv[l])
        xs = mm_ln(mixed, w_out16, l, xs, g1, ln1_g[l], ln1_b[l])
        act = mm_fullk(xs, 0, d, mlp_w1_16, prologue="mod", p1=sc2, p2=sh2, epilogue="relu2", out_dtype=BF16,
                       w_layer=l)
        xs = mm_ln([act], mlp_w2_16, l, xs, g2, ln2_g[l], ln2_b[l])
    return xs[None]
```

```python
import functools

import jax
import jax.numpy as jnp
import numpy as np
from jax import lax
from jax.experimental import pallas as pl
from jax.experimental.pallas import tpu as pltpu

D_MODEL = 4096
DEPTH = 2
HEAD_DIM = 128
ROPE_THETA = 10000.0
QBLOCK = 128
DSW_HEADS = 8
DSW_PATTERNS = ((128, 1), (512, 4), (2048, 16))
NSA_HEADS = 8
NSA_KV_HEADS = 2
NSA_GROUP = NSA_HEADS // NSA_KV_HEADS
NSA_CMP_LEN = 32
NSA_CMP_STRIDE = 16
NSA_CMP_HIDDEN = 256
NSA_SEL_BLOCK = 64
NSA_TOP_N = 16
NSA_WINDOW = 512
MLA_HEADS = 16
MLA_Q_LORA = 1536
MLA_KV_LORA = 512
MLA_NOPE_DIM = 128
MLA_ROPE_DIM = 64
MLA_V_DIM = 128
D_FF = 4 * D_MODEL
ALPHA = (2 * DEPTH) ** 0.25

DSW_W = DSW_HEADS * HEAD_DIM
NSA_QW = NSA_HEADS * HEAD_DIM
NSA_KVW = NSA_KV_HEADS * HEAD_DIM

LANES = 128
V7X_VMEM_LIMIT = 56 * 1024 * 1024
V7X_VMEM_LIMIT_LARGE = 60 * 1024 * 1024

F32 = jnp.float32
BF16 = jnp.bfloat16
NEG_INF = float("-inf")
LOG2E = 1.4426950408889634
FLASH_CHUNK = 64 * 512
FLASH_UNROLL = 4
DIL_SUPER = QBLOCK * max(d for _, d in DSW_PATTERNS)
DIL_MIX_ROWS = 64
NT_DIMS = (((1,), (1,)), ((), ()))


def _params(sem, vmem=V7X_VMEM_LIMIT):
    return pltpu.CompilerParams(dimension_semantics=sem, vmem_limit_bytes=vmem)


def _adaln_body(cb_ref, w_ref, b_ref, o_ref, *, tn):
    cb = cb_ref[...]
    for s in range(tn // LANES):
        sl = slice(s * LANES, (s + 1) * LANES)
        o_ref[:, sl] = jnp.sum(w_ref[:, sl] * cb, axis=0, keepdims=True) + b_ref[:, sl]


def adaln(c, w, b, layer, tn=512):
    nl, d, n = w.shape
    cb = jnp.broadcast_to(c.reshape(d, 1), (d, LANES))
    return pl.pallas_call(
        functools.partial(_adaln_body, tn=tn),
        grid=(n // tn,),
        in_specs=[
            pl.BlockSpec((d, LANES), lambda j: (0, 0)),
            pl.BlockSpec((None, d, tn), lambda j: (layer, 0, j)),
            pl.BlockSpec((None, 1, tn), lambda j: (layer, 0, j)),
        ],
        out_specs=pl.BlockSpec((1, tn), lambda j: (0, j)),
        out_shape=jax.ShapeDtypeStruct((1, n), F32),
        compiler_params=_params(("arbitrary",)),
        name="adaln",
    )(cb, w, b.reshape(nl, 1, n))


def _mm_fullk_body(*refs, prologue, epilogue, nsub, ksplit):
    refs = list(refs)
    mode_ref = refs.pop(0) if epilogue == "rope" else None
    x_ref, p1_ref = refs.pop(0), refs.pop(0)
    p2_ref = refs.pop(0) if prologue == "mod" else None
    w_ref = refs.pop(0)
    if epilogue == "rope":
        ta_ref, tb_ref = refs.pop(0), refs.pop(0)
    if epilogue == "ksplit":
        o_ref, kt_ref, h_scr = refs
    else:
        o_ref, h_scr = refs
    j = pl.program_id(1)

    @pl.when(j == 0)
    def _():
        x = x_ref[...]
        if prologue == "mod":
            h = x * (1.0 + p1_ref[...]) + p2_ref[...]
        else:
            h = x * lax.rsqrt(jnp.mean(x * x, axis=-1, keepdims=True) + 1e-6) * p1_ref[...]
        h_scr[...] = h.astype(BF16)

    acc = jnp.dot(h_scr[...], w_ref[...], preferred_element_type=F32)
    if epilogue == "relu2":
        r = jnp.maximum(acc, 0.0)
        o_ref[...] = (r * r).astype(o_ref.dtype)
    elif epilogue == "rope":
        for s in range(nsub):
            sl = slice(s * LANES, (s + 1) * LANES)
            md = mode_ref[j * nsub + s]
            sub = acc[:, sl]
            o_ref[:, sl] = (sub * ta_ref[md] + pltpu.roll(sub, LANES // 2, 1) * tb_ref[md]).astype(o_ref.dtype)
    elif epilogue == "ksplit":
        @pl.when(j < ksplit)
        def _():
            kt_ref[...] = acc.T.astype(kt_ref.dtype)

        @pl.when(j >= ksplit)
        def _():
            o_ref[...] = acc.astype(o_ref.dtype)
    else:
        o_ref[...] = acc.astype(o_ref.dtype)


def mm_fullk(x, x_col_block, k, w, *, prologue, p1, p2=None, epilogue="none", modes=None, ta=None, tb=None,
             out_dtype=BF16, tm=512, tn=1024, w_layer=None, k_cols=0):
    m = x.shape[0]
    n = w.shape[-1]
    tm = min(tm, m)
    assert m % tm == 0 and n % tn == 0 and w.shape[-2] == k
    nsub = tn // LANES
    rope = epilogue == "rope"
    npre = 1 if rope else 0

    def im(f):
        return (lambda i, j, *_: f(i, j))

    in_specs = [pl.BlockSpec((tm, k), im(lambda i, j: (i, x_col_block))),
                pl.BlockSpec((1, k), im(lambda i, j: (0, 0)))]
    args = [x, p1.reshape(1, k)]
    if prologue == "mod":
        in_specs.append(pl.BlockSpec((1, k), im(lambda i, j: (0, 0))))
        args.append(p2.reshape(1, k))
    if w_layer is None:
        in_specs.append(pl.BlockSpec((k, tn), im(lambda i, j: (0, j))))
    else:
        in_specs.append(pl.BlockSpec((None, k, tn), im(lambda i, j: (w_layer, 0, j))))
    args.append(w)
    if rope:
        nmode = ta.shape[0]
        in_specs += [pl.BlockSpec((nmode, tm, LANES), im(lambda i, j: (0, i, 0)))] * 2
        args += [ta, tb]
    ksplit = k_cols // tn
    if epilogue == "ksplit":
        assert k_cols % tn == 0 and 0 < k_cols < n
        out_specs = [pl.BlockSpec((tm, tn), im(lambda i, j: (i, jnp.maximum(j - ksplit, 0)))),
                     pl.BlockSpec((tn, tm), im(lambda i, j: (jnp.minimum(j, ksplit - 1), i)))]
        out_shape = [jax.ShapeDtypeStruct((m, n - k_cols), out_dtype), jax.ShapeDtypeStruct((k_cols, m), out_dtype)]
    else:
        out_specs = pl.BlockSpec((tm, tn), im(lambda i, j: (i, j)))
        out_shape = jax.ShapeDtypeStruct((m, n), out_dtype)
    grid_spec = pltpu.PrefetchScalarGridSpec(
        num_scalar_prefetch=npre,
        grid=(m // tm, n // tn),
        in_specs=in_specs,
        out_specs=out_specs,
        scratch_shapes=[pltpu.VMEM((tm, k), BF16)],
    )
    fn = pl.pallas_call(
        functools.partial(_mm_fullk_body, prologue=prologue, epilogue=epilogue, nsub=nsub, ksplit=ksplit),
        grid_spec=grid_spec,
        out_shape=out_shape,
        compiler_params=_params(("parallel", "arbitrary")),
        name="mm_fullk_" + prologue + "_" + epilogue,
    )
    if rope:
        return fn(jnp.asarray(modes, jnp.int32), *args)
    return fn(*args)


def _mm_ln_body(*refs, nk, bounds):
    lhs_refs = refs[:len(bounds)]
    w_ref, x_ref, g_ref, lng_ref, lnb_ref, o_ref, acc_ref = refs[len(bounds):]
    kk = pl.program_id(1)

    def lhs():
        val = lhs_refs[-1][...]
        for ref, hi in zip(reversed(lhs_refs[:-1]), reversed(bounds[:-1])):
            val = jnp.where(kk < hi, ref[...], val)
        return val

    @pl.when(kk == 0)
    def _():
        acc_ref[...] = jnp.dot(lhs(), w_ref[...], preferred_element_type=F32)

    @pl.when(kk > 0)
    def _():
        acc_ref[...] += jnp.dot(lhs(), w_ref[...], preferred_element_type=F32)

    @pl.when(kk == nk - 1)
    def _():
        z = ALPHA * x_ref[...] + (1.0 + g_ref[...]) * acc_ref[...]
        mu = jnp.mean(z, axis=-1, keepdims=True)
        zc = z - mu
        var = jnp.mean(zc * zc, axis=-1, keepdims=True)
        o_ref[...] = zc * lax.rsqrt(var + 1e-5) * lng_ref[...] + lnb_ref[...]


def mm_ln(lhs_parts, w, w_layer, x, gate, ln_g, ln_b, tm=512, tk=512, x_buffers=2):
    m = lhs_parts[0].shape[0]
    k = sum(part.shape[1] for part in lhs_parts)
    n = w.shape[2]
    bounds, lhs_specs = [], []
    for part in lhs_parts:
        lo = bounds[-1] if bounds else 0
        hi = lo + part.shape[1] // tk
        bounds.append(hi)
        lhs_specs.append(pl.BlockSpec((tm, tk), lambda i, kk, lo=lo, hi=hi: (i, jnp.clip(kk - lo, 0, hi - lo - 1))))
    tm = min(tm, m)
    nk = k // tk
    row = lambda i, kk: (0, 0)
    return pl.pallas_call(
        functools.partial(_mm_ln_body, nk=nk, bounds=tuple(bounds)),
        grid=(m // tm, nk),
        in_specs=lhs_specs + [
            pl.BlockSpec((None, tk, n), lambda i, kk: (w_layer, kk, 0)),
            pl.BlockSpec((tm, n), lambda i, kk: (i, 0), pipeline_mode=pl.Buffered(x_buffers)),
            pl.BlockSpec((1, n), row), pl.BlockSpec((1, n), row), pl.BlockSpec((1, n), row),
        ],
        out_specs=pl.BlockSpec((tm, n), lambda i, kk: (i, 0)),
        out_shape=jax.ShapeDtypeStruct((m, n), F32),
        scratch_shapes=[pltpu.VMEM((tm, n), F32)],
        compiler_params=_params(("parallel", "arbitrary"), V7X_VMEM_LIMIT_LARGE),
        name="mm_ln",
    )(*lhs_parts, w, x, gate.reshape(1, n), ln_g.reshape(1, n), ln_b.reshape(1, n))


def _dilated_body(q_ref, kc_ref, kp_ref, vc_ref, vp_ref, o_ref, q32, kc32, kp32, vc32, vp32, od_scr, ld_scr):
    n = pl.program_id(1)
    q32[...] = q_ref[...].astype(F32)
    kc32[...] = kc_ref[...].astype(F32)
    kp32[...] = kp_ref[...].astype(F32)
    vc32[...] = vc_ref[...].astype(F32)
    vp32[...] = vp_ref[...].astype(F32)
    qi = lax.broadcasted_iota(jnp.int32, (QBLOCK, 2 * QBLOCK), 0)
    ki = lax.broadcasted_iota(jnp.int32, (QBLOCK, 2 * QBLOCK), 1)
    delta = QBLOCK + qi - ki
    in_band = (delta >= 0) & (delta <= QBLOCK)
    in_band_first = in_band & ((ki >= QBLOCK) | (n > 0))

    for pi, (window, dil) in enumerate(DSW_PATTERNS):
        assert window // dil == QBLOCK
        nblk = DIL_SUPER // (QBLOCK * dil)
        for mb in range(nblk):
            for r in range(dil):
                def rows(b):
                    return pl.ds(b * QBLOCK * dil + r, QBLOCK, stride=dil) if dil > 1 else pl.ds(b * QBLOCK, QBLOCK)

                cur = rows(mb)
                if mb > 0:
                    k_prev, v_prev, valid = kc32[rows(mb - 1), :], vc32[rows(mb - 1), :], in_band
                else:
                    k_prev, v_prev, valid = kp32[rows(nblk - 1), :], vp32[rows(nblk - 1), :], in_band_first
                k = jnp.concatenate([k_prev, kc32[cur, :]], axis=0).astype(BF16)
                v = jnp.concatenate([v_prev, vc32[cur, :]], axis=0).astype(BF16)
                s = lax.dot_general(q32[cur, :].astype(BF16), k, NT_DIMS, preferred_element_type=F32)
                s = jnp.where(valid, s, NEG_INF)
                m = jnp.max(s, axis=-1, keepdims=True)
                p = jnp.exp2(s - m)
                l = jnp.sum(p, axis=-1, keepdims=True)
                od_scr[pi, cur, :] = jnp.dot(p.astype(BF16), v, preferred_element_type=F32) / l
                ld_scr[pi, cur, :] = jnp.broadcast_to(m + jnp.log2(l), (QBLOCK, HEAD_DIM))

    npat = len(DSW_PATTERNS)
    for c0 in range(0, DIL_SUPER, DIL_MIX_ROWS):
        rs = slice(c0, c0 + DIL_MIX_ROWS)
        ls = [ld_scr[i, rs, :] for i in range(npat)]
        mx = functools.reduce(jnp.maximum, ls)
        ws = [jnp.exp2(t - mx) for t in ls]
        num = sum(ws[i] * od_scr[i, rs, :] for i in range(npat))
        o_ref[rs, :] = (num / sum(ws)).astype(o_ref.dtype)


def dilated_attention(qkv, s_len, q_blk, k_blk, v_blk):
    assert s_len % DIL_SUPER == 0
    cur = lambda blk: (lambda h, n: (n, blk + h))
    prev = lambda blk: (lambda h, n: (jnp.maximum(n - 1, 0), blk + h))
    bs = lambda f: pl.BlockSpec((DIL_SUPER, HEAD_DIM), f)
    f32_rows = pltpu.VMEM((DIL_SUPER, HEAD_DIM), F32)
    per_pattern = pltpu.VMEM((len(DSW_PATTERNS), DIL_SUPER, HEAD_DIM), F32)
    return pl.pallas_call(
        _dilated_body,
        grid=(DSW_HEADS, s_len // DIL_SUPER),
        in_specs=[bs(cur(q_blk)), bs(cur(k_blk)), bs(prev(k_blk)), bs(cur(v_blk)), bs(prev(v_blk))],
        out_specs=pl.BlockSpec((DIL_SUPER, HEAD_DIM), lambda h, n: (n, h)),
        out_shape=jax.ShapeDtypeStruct((s_len, DSW_W), BF16),
        scratch_shapes=[f32_rows] * 5 + [per_pattern] * 2,
        compiler_params=_params(("parallel", "parallel")),
        name="dilated_attention",
    )(qkv, qkv, qkv, qkv, qkv)


def _cmp_body(x_ref, pe_ref, w1_ref, w2_ref, o_ref, *, ncp):
    x = x_ref[...]
    half = NSA_CMP_STRIDE * HEAD_DIM
    a = jnp.dot((x + pe_ref[0:1, :]).astype(BF16), w1_ref[0:half, :], preferred_element_type=F32)
    b = jnp.dot((x + pe_ref[1:2, :]).astype(BF16), w1_ref[half:2 * half, :], preferred_element_type=F32)
    hid = a + pltpu.roll(b, ncp - 1, 0)
    act = jax.nn.gelu(hid)
    o_ref[...] = jnp.dot(act.astype(BF16), w2_ref[...], preferred_element_type=F32).astype(o_ref.dtype)


def nsa_compress(xs, pe, w1, w2):
    ncp = xs.shape[2]
    half = NSA_CMP_STRIDE * HEAD_DIM
    return pl.pallas_call(
        functools.partial(_cmp_body, ncp=ncp),
        grid=(2, NSA_KV_HEADS),
        in_specs=[
            pl.BlockSpec((None, None, ncp, half), lambda a, h: (a, h, 0, 0)),
            pl.BlockSpec((None, 2, half), lambda a, h: (a, 0, 0)),
            pl.BlockSpec((None, 2 * half, NSA_CMP_HIDDEN), lambda a, h: (a, 0, 0)),
            pl.BlockSpec((None, NSA_CMP_HIDDEN, HEAD_DIM), lambda a, h: (a, 0, 0)),
        ],
        out_specs=pl.BlockSpec((None, None, ncp, HEAD_DIM), lambda a, h: (a, h, 0, 0)),
        out_shape=jax.ShapeDtypeStruct((2, NSA_KV_HEADS, ncp, HEAD_DIM), BF16),
        compiler_params=_params(("parallel", "parallel")),
        name="nsa_compress",
    )(xs, pe.reshape(2, 2, half), w1, w2)


def _stack_heads(q):
    return jnp.concatenate([q[:, g * HEAD_DIM:(g + 1) * HEAD_DIM] for g in range(NSA_GROUP)], axis=0)


def _nsa_body(q_ref, kst_ref, vs_ref, kc_ref, vc_ref, covt_ref, e0_ref, gate_ref, u_ref,
              sel_scr, q4_scr, sc_scr, pf_scr, pc_scr, mk_scr, ve_scr, s_scr, p_scr, a_scr, m_scr, acc_scr,
              *, ncp, nselp, ntop, tk):
    n = pl.program_id(1)
    rows = NSA_GROUP * QBLOCK
    q4_scr[...] = _stack_heads(q_ref[...])
    q4 = q4_scr[...]

    sc_scr[...] = lax.dot_general(q4, kc_ref[...], NT_DIMS, preferred_element_type=F32)
    chunk = min(FLASH_CHUNK // ncp, rows)
    cend = lax.broadcasted_iota(jnp.int32, (chunk, ncp), 1) * NSA_CMP_STRIDE + (NSA_CMP_LEN - 1)
    for r in range(0, rows, chunk):
        rs = slice(r, r + chunk)
        qpos_r = n * QBLOCK + ((r + lax.broadcasted_iota(jnp.int32, (chunk, ncp), 0)) & (QBLOCK - 1))
        s = jnp.where(cend <= qpos_r, sc_scr[rs, :], NEG_INF)
        m = jnp.max(s, axis=-1, keepdims=True)
        e = jnp.exp2(s - jnp.where(m == NEG_INF, 0.0, m))
        l = jnp.sum(e, axis=-1, keepdims=True)
        p = e / jnp.where(l > 0, l, 1.0)
        pf_scr[rs, :] = p
        pc_scr[rs, :] = p.astype(BF16)
    o_c = jnp.dot(pc_scr[...], vc_ref[...], preferred_element_type=F32)

    psum = pf_scr[0:QBLOCK, :]
    for g in range(1, NSA_GROUP):
        psum = psum + pf_scr[g * QBLOCK:(g + 1) * QBLOCK, :]
    p_hi = psum.astype(BF16)
    p_lo = (psum - p_hi.astype(F32)).astype(BF16)
    covt = covt_ref[...]
    imp_t = (lax.dot_general(covt, p_hi, NT_DIMS, preferred_element_type=F32)
             + lax.dot_general(covt, p_lo, NT_DIMS, preferred_element_type=F32))

    jblk = lax.broadcasted_iota(jnp.int32, (nselp, QBLOCK), 0)
    qpos_c = n * QBLOCK + lax.broadcasted_iota(jnp.int32, (nselp, QBLOCK), 1)
    cur = qpos_c // NSA_SEL_BLOCK
    forced = (jblk == 0) | (jblk == cur) | (jblk == cur - 1)
    valid = jblk * NSA_SEL_BLOCK <= qpos_c
    score0 = jnp.where(valid, jnp.where(forced, jnp.inf, imp_t), NEG_INF)

    def pick(_, carry):
        score, picked = carry
        mx = jnp.max(score, axis=0, keepdims=True)
        idx = jnp.min(jnp.where(score == mx, jblk, nselp), axis=0, keepdims=True)
        hit = jblk == idx
        return jnp.where(hit, NEG_INF, score), jnp.where(hit, 1.0, picked)

    _, picked = lax.fori_loop(0, ntop, pick, (score0, jnp.zeros((nselp, QBLOCK), F32)))
    sel_scr[...] = jnp.where(valid, picked, 0.0).T

    blocks_per_tile = tk // NSA_SEL_BLOCK
    _flash_init(m_scr, acc_scr, ve_scr)

    def k_tile(t, slot, buf):
        shift = (nselp - t * blocks_per_tile) % nselp
        sel_t = pltpu.roll(sel_scr[...], shift, 1)[:, 0:LANES].astype(BF16)
        mk_scr[slot] = jnp.dot(sel_t, e0_ref[...], preferred_element_type=F32)
        return kst_ref[:, pl.ds(pl.multiple_of(t * tk, tk), tk)]

    def v_tile(t, buf):
        ve_scr[buf, :, 0:HEAD_DIM] = vs_ref[pl.ds(pl.multiple_of(t * tk, tk), tk), :]
        return ve_scr[buf]

    def mask_for(slot, t, masked):
        def mask_fn(r, s):
            rq = r % QBLOCK
            ok = mk_scr[slot, rq:rq + s.shape[0], :] > 0.5
            if masked:
                qpos = n * QBLOCK + rq + lax.broadcasted_iota(jnp.int32, s.shape, 0)
                ok = ok & (t * tk + lax.broadcasted_iota(jnp.int32, s.shape, 1) <= qpos)
            return jnp.where(ok, s, NEG_INF)
        return mask_fn

    _flash_pipeline((n * QBLOCK + QBLOCK - 1) // tk, q4_scr, k_tile, v_tile, mask_for,
                    s_scr, p_scr, a_scr, m_scr, acc_scr)
    o_s = _flash_output(acc_scr)

    gates = jax.nn.sigmoid(gate_ref[...])
    for g in range(NSA_GROUP):
        rs = slice(g * QBLOCK, (g + 1) * QBLOCK)
        u_ref[:, g * HEAD_DIM:(g + 1) * HEAD_DIM] = (gates[:, 3 * g:3 * g + 1] * o_c[rs]
                                                     + gates[:, 3 * g + 1:3 * g + 2] * o_s[rs])


def nsa_cmp_sel(qkv, ks_t, s_len, q_blk, vs_blk, kv_cmp, gates, tk=512):
    nb = s_len // QBLOCK
    ncp = s_len // NSA_CMP_STRIDE
    nsel = s_len // NSA_SEL_BLOCK
    nselp = -(-nsel // LANES) * LANES
    ntop = min(NSA_TOP_N, nsel)
    ci = np.arange(ncp)[None, :] * NSA_CMP_STRIDE
    sj = np.arange(nselp)[:, None] * NSA_SEL_BLOCK
    cov = (ci < sj + NSA_SEL_BLOCK) & (ci + NSA_CMP_LEN > sj) & (np.arange(ncp)[None, :] < ncp - 1) & (sj < s_len)
    covt = jnp.asarray(cov.astype(np.float32), BF16)
    e0 = jnp.asarray((np.arange(tk)[None, :] // NSA_SEL_BLOCK == np.arange(LANES)[:, None]).astype(np.float32), BF16)
    gw = NSA_GROUP * HEAD_DIM
    rows = NSA_GROUP * QBLOCK
    return pl.pallas_call(
        functools.partial(_nsa_body, ncp=ncp, nselp=nselp, ntop=ntop, tk=tk),
        grid=(NSA_KV_HEADS, nb),
        in_specs=[
            pl.BlockSpec((QBLOCK, gw), lambda h, n: (n, q_blk + h)),
            pl.BlockSpec((HEAD_DIM, s_len), lambda h, n: (h, 0)),
            pl.BlockSpec((s_len, HEAD_DIM), lambda h, n: (0, vs_blk + h)),
            pl.BlockSpec((None, None, ncp, HEAD_DIM), lambda h, n: (0, h, 0, 0)),
            pl.BlockSpec((None, None, ncp, HEAD_DIM), lambda h, n: (1, h, 0, 0)),
            pl.BlockSpec((nselp, ncp), lambda h, n: (0, 0)),
            pl.BlockSpec((LANES, tk), lambda h, n: (0, 0)),
            pl.BlockSpec((None, QBLOCK, 3 * NSA_GROUP), lambda h, n: (h, n, 0)),
        ],
        out_specs=pl.BlockSpec((QBLOCK, gw), lambda h, n: (n, h)),
        out_shape=jax.ShapeDtypeStruct((s_len, NSA_QW), F32),
        scratch_shapes=[pltpu.VMEM((QBLOCK, nselp), F32), pltpu.VMEM((rows, HEAD_DIM), BF16),
                        pltpu.VMEM((rows, ncp), F32), pltpu.VMEM((rows, ncp), F32), pltpu.VMEM((rows, ncp), BF16),
                        pltpu.VMEM((FLASH_UNROLL, QBLOCK, tk), F32), pltpu.VMEM((FLASH_UNROLL + 1, tk, 2 * HEAD_DIM), BF16),
                        pltpu.VMEM((FLASH_UNROLL, rows, tk), F32), pltpu.VMEM((FLASH_UNROLL, rows, tk), BF16),
                        pltpu.VMEM((FLASH_UNROLL, rows, LANES), F32), pltpu.VMEM((rows, LANES), F32),
                        pltpu.VMEM((rows, 2 * HEAD_DIM), F32)],
        compiler_params=_params(("parallel", "arbitrary")),
        name="nsa_cmp_sel",
    )(qkv, ks_t, qkv, kv_cmp, kv_cmp, covt, e0, gates)


def _win_body(q_ref, kw_ref, vw_ref, gate_ref, u_ref, o_ref, *, span):
    n = pl.program_id(1)
    rows = NSA_GROUP * QBLOCK
    q4 = _stack_heads(q_ref[...])
    start = pl.multiple_of(jnp.maximum(n * QBLOCK - (span - QBLOCK), 0), QBLOCK)
    s = lax.dot_general(q4, kw_ref[pl.ds(start, span), :], NT_DIMS, preferred_element_type=F32)
    qpos = n * QBLOCK + (lax.broadcasted_iota(jnp.int32, (rows, span), 0) & (QBLOCK - 1))
    delta = qpos - (start + lax.broadcasted_iota(jnp.int32, (rows, span), 1))
    s = jnp.where((delta >= 0) & (delta <= NSA_WINDOW - 1), s, NEG_INF)
    m = jnp.max(s, axis=-1, keepdims=True)
    p = jnp.exp2(s - m)
    l = jnp.sum(p, axis=-1, keepdims=True)
    o_w = jnp.dot(p.astype(BF16), vw_ref[pl.ds(start, span), :], preferred_element_type=F32) / l
    gates = jax.nn.sigmoid(gate_ref[...])
    for g in range(NSA_GROUP):
        sl = slice(g * HEAD_DIM, (g + 1) * HEAD_DIM)
        o_ref[:, sl] = (u_ref[:, sl] + gates[:, 3 * g + 2:3 * g + 3] * o_w[g * QBLOCK:(g + 1) * QBLOCK]
                        ).astype(o_ref.dtype)


def nsa_window(qkv, s_len, q_blk, kw_blk, vw_blk, gates, u):
    nb = s_len // QBLOCK
    span = (-(-(NSA_WINDOW - 1) // QBLOCK) + 1) * QBLOCK
    gw = NSA_GROUP * HEAD_DIM
    return pl.pallas_call(
        functools.partial(_win_body, span=span),
        grid=(NSA_KV_HEADS, nb),
        in_specs=[
            pl.BlockSpec((QBLOCK, gw), lambda h, n: (n, q_blk + h)),
            pl.BlockSpec((s_len, HEAD_DIM), lambda h, n: (0, kw_blk + h)),
            pl.BlockSpec((s_len, HEAD_DIM), lambda h, n: (0, vw_blk + h)),
            pl.BlockSpec((None, QBLOCK, 3 * NSA_GROUP), lambda h, n: (h, n, 0)),
            pl.BlockSpec((QBLOCK, gw), lambda h, n: (n, h)),
        ],
        out_specs=pl.BlockSpec((QBLOCK, gw), lambda h, n: (n, h)),
        out_shape=jax.ShapeDtypeStruct((s_len, NSA_QW), BF16),
        compiler_params=_params(("parallel", "parallel")),
        name="nsa_window",
    )(qkv, qkv, qkv, gates, u)


def _softmax_rows(s_ref, p_ref, a_ref, m_scr, mask_fn):
    rows, tk = s_ref.shape
    nrep = tk // LANES
    chunk = FLASH_CHUNK // tk
    for r in range(0, rows, chunk):
        rs = slice(r, r + chunk)
        s = s_ref[rs, :]
        if mask_fn is not None:
            s = mask_fn(r, s)
        m_old = m_scr[rs, :]
        m_new = jnp.maximum(m_old, jnp.max(s, axis=1, keepdims=True))
        p_ref[rs, :] = jnp.exp2(s - jnp.concatenate([m_new] * nrep, axis=1)).astype(BF16)
        a_ref[rs, :] = jnp.exp2(m_old - m_new)
        m_scr[rs, :] = m_new


def _flash_pipeline(n_full, q_ref, k_tile, v_tile, mask_for, s_scr, p_scr, a_scr, m_scr, acc_scr):
    def scores(t, idx, buf):
        s_scr[idx] = jnp.dot(q_ref[...], k_tile(t, idx, buf), preferred_element_type=F32)

    def softmax(idx, t, masked):
        _softmax_rows(s_scr.at[idx], p_scr.at[idx], a_scr.at[idx], m_scr, mask_for(idx, t, masked))

    def values(idx, t, buf):
        a = a_scr[idx]
        acc_scr[...] = (jnp.concatenate([a, a], axis=1) * acc_scr[...]
                        + jnp.dot(p_scr[idx], v_tile(t, buf), preferred_element_type=F32))

    last = FLASH_UNROLL - 1
    p_scr[last] = jnp.zeros(p_scr.shape[1:], p_scr.dtype)
    a_scr[last] = jnp.ones(a_scr.shape[1:], a_scr.dtype)
    scores(0, 0, 0)

    def run(t0, count, masked_last, lookahead):
        for j in range(count):
            t = t0 + j
            if lookahead or j + 1 < count:
                scores(t + 1, (j + 1) % FLASH_UNROLL, j)
            softmax(j, t, masked_last and j == count - 1)
            values((j - 1) % FLASH_UNROLL, jnp.maximum(t - 1, 0), j)
        if not lookahead:
            values(count - 1, t0 + count - 1, count)

    def body(u, carry):
        run(FLASH_UNROLL * u, FLASH_UNROLL, False, True)
        return carry

    lax.fori_loop(0, n_full // FLASH_UNROLL, body, 0)
    rem = n_full % FLASH_UNROLL
    for r in range(FLASH_UNROLL):
        @pl.when(rem == r)
        def _(r=r):
            run(n_full - r, r + 1, True, False)


def _flash_init(m_scr, acc_scr, ve_scr):
    dv = ve_scr.shape[2] // 2
    m_scr[...] = jnp.full(m_scr.shape, NEG_INF, F32)
    acc_scr[...] = jnp.zeros(acc_scr.shape, F32)
    ve_scr[:, :, dv:] = jnp.ones((ve_scr.shape[0], ve_scr.shape[1], dv), ve_scr.dtype)


def _flash_output(acc_scr):
    dv = acc_scr.shape[1] // 2
    return acc_scr[:, 0:dv] / acc_scr[:, dv:]


def _mla_body(q_ref, knt_ref, krt_ref, v_ref, o_ref, kt_scr, ve_scr, s_scr, p_scr, a_scr, m_scr, acc_scr,
              *, tq, tk):
    qi = pl.program_id(1)
    _flash_init(m_scr, acc_scr, ve_scr)

    def mask_for(slot, t, masked):
        if not masked:
            return None

        def causal(r, s):
            qpos = qi * tq + r + lax.broadcasted_iota(jnp.int32, s.shape, 0)
            kpos = t * tk + lax.broadcasted_iota(jnp.int32, s.shape, 1)
            return jnp.where(kpos <= qpos, s, NEG_INF)
        return causal

    def k_tile(t, slot, buf):
        k0 = pl.multiple_of(t * tk, tk)
        kt_scr[buf, 0:MLA_NOPE_DIM, :] = knt_ref[:, pl.ds(k0, tk)]
        kt_scr[buf, MLA_NOPE_DIM:, :] = krt_ref[:, pl.ds(k0, tk)]
        return kt_scr[buf]

    def v_tile(t, buf):
        ve_scr[buf, :, 0:MLA_V_DIM] = v_ref[pl.ds(pl.multiple_of(t * tk, tk), tk), :]
        return ve_scr[buf]

    _flash_pipeline((qi * tq) // tk, q_ref, k_tile, v_tile, mask_for, s_scr, p_scr, a_scr, m_scr, acc_scr)
    o_ref[...] = _flash_output(acc_scr).astype(o_ref.dtype)


def mla_attention(q, v, k_nope_t, k_rope_t, s_len, tq=512, tk=512):
    tq, tk = min(tq, s_len), min(tk, s_len)
    qw = 2 * LANES
    return pl.pallas_call(
        functools.partial(_mla_body, tq=tq, tk=tk),
        grid=(MLA_HEADS, s_len // tq),
        in_specs=[
            pl.BlockSpec((tq, qw), lambda h, i: (i, h)),
            pl.BlockSpec((MLA_NOPE_DIM, s_len), lambda h, i: (h, 0)),
            pl.BlockSpec((LANES, s_len), lambda h, i: (0, 0)),
            pl.BlockSpec((s_len, MLA_V_DIM), lambda h, i: (0, h)),
        ],
        out_specs=pl.BlockSpec((tq, MLA_V_DIM), lambda h, i: (i, h)),
        out_shape=jax.ShapeDtypeStruct((s_len, MLA_HEADS * MLA_V_DIM), BF16),
        scratch_shapes=[pltpu.VMEM((FLASH_UNROLL, qw, tk), BF16),
                        pltpu.VMEM((FLASH_UNROLL + 1, tk, 2 * MLA_V_DIM), BF16),
                        pltpu.VMEM((FLASH_UNROLL, tq, tk), F32), pltpu.VMEM((FLASH_UNROLL, tq, tk), BF16),
                        pltpu.VMEM((FLASH_UNROLL, tq, LANES), F32), pltpu.VMEM((tq, LANES), F32),
                        pltpu.VMEM((tq, 2 * MLA_V_DIM), F32)],
        compiler_params=_params(("parallel", "parallel")),
        name="mla_attention",
    )(q, k_nope_t, k_rope_t, v)


_IN_SIZES = (DSW_W, DSW_W, DSW_W, NSA_QW, NSA_KVW, NSA_KVW, NSA_KVW, NSA_KVW, NSA_KVW, NSA_KVW,
             3 * NSA_HEADS, MLA_Q_LORA, MLA_KV_LORA, MLA_ROPE_DIM)
_IN_NAMES = ("a_q", "a_k", "a_v", "n_q", "n_kc", "n_vc", "n_ks", "n_vs", "n_kw", "n_vw", "n_gate",
             "m_cq", "m_ckv", "m_kr")
_IN_OFF = dict(zip(_IN_NAMES, np.concatenate([[0], np.cumsum(_IN_SIZES)[:-1]]).tolist()))
_IN_LEN = dict(zip(_IN_NAMES, _IN_SIZES))

_B_ORDER = ("a_q", "a_k", "n_q", "n_ks", "n_kw", "n_vs", "n_vw", "a_v")
_B_ROPE = 2 * DSW_W + NSA_QW + 2 * NSA_KVW
_B_WIDTH = sum(_IN_LEN[k] for k in _B_ORDER)
_B_COL = dict(zip(_B_ORDER, np.concatenate([[0], np.cumsum([_IN_LEN[k] for k in _B_ORDER])[:-1]]).tolist()))
_F_COL = {"m_cq": 0, "m_ckv": MLA_Q_LORA, "n_kc": 2048, "n_vc": 2304, "m_kr": 2560, "n_gate": 2688}
_F_WIDTH = 3072
_HALF_ROPE = MLA_ROPE_DIM // 2


def _prep_w_in_body(w_ref, wb_ref, wf_ref):
    for name in _B_ORDER:
        wb_ref[:, _B_COL[name]:_B_COL[name] + _IN_LEN[name]] = (
            w_ref[:, _IN_OFF[name]:_IN_OFF[name] + _IN_LEN[name]].astype(BF16))
    wf_ref[...] = jnp.zeros(wf_ref.shape, BF16)
    for name in ("m_cq", "m_ckv", "n_kc", "n_vc", "n_gate"):
        wf_ref[:, _F_COL[name]:_F_COL[name] + _IN_LEN[name]] = (
            w_ref[:, _IN_OFF[name]:_IN_OFF[name] + _IN_LEN[name]].astype(BF16))
    kr_src, kr_dst = _IN_OFF["m_kr"], _F_COL["m_kr"]
    for half in range(2):
        wf_ref[:, kr_dst + 2 * half * _HALF_ROPE:kr_dst + (2 * half + 1) * _HALF_ROPE] = (
            w_ref[:, kr_src + half * _HALF_ROPE:kr_src + (half + 1) * _HALF_ROPE].astype(BF16))


def _prep_w_in(w, layer, tm=256):
    nl, k, n = w.shape
    return pl.pallas_call(
        _prep_w_in_body,
        grid=(k // tm,),
        in_specs=[pl.BlockSpec((None, tm, n), lambda i: (layer, i, 0))],
        out_specs=[pl.BlockSpec((tm, _B_WIDTH), lambda i: (i, 0)), pl.BlockSpec((tm, _F_WIDTH), lambda i: (i, 0))],
        out_shape=[jax.ShapeDtypeStruct((k, _B_WIDTH), BF16), jax.ShapeDtypeStruct((k, _F_WIDTH), BF16)],
        compiler_params=_params(("parallel",)),
        name="prep_w_in",
    )(w)


_B_MODES = ([2] * (DSW_W // LANES) + [1] * (DSW_W // LANES) + [2] * (NSA_QW // LANES)
            + [1] * (2 * NSA_KVW // LANES) + [0] * ((_B_WIDTH - _B_ROPE) // LANES))
_F_MODES = [0] * 16 + [1, 1, 0, 0, 2] + [0] * 3
_Q_MODES = [0, 1] * MLA_HEADS
C_HEAD = HEAD_DIM ** -0.5 * LOG2E
C_MLA = (MLA_NOPE_DIM + MLA_ROPE_DIM) ** -0.5 * LOG2E


def _prep_w_uq(w):
    w = w.reshape(MLA_Q_LORA, MLA_HEADS, MLA_NOPE_DIM + MLA_ROPE_DIM)
    z = jnp.zeros((MLA_Q_LORA, MLA_HEADS, _HALF_ROPE), w.dtype)
    w = jnp.concatenate([w[..., :MLA_NOPE_DIM], w[..., MLA_NOPE_DIM:MLA_NOPE_DIM + _HALF_ROPE], z,
                         w[..., MLA_NOPE_DIM + _HALF_ROPE:], z], axis=-1)
    return w.reshape(MLA_Q_LORA, MLA_HEADS * 2 * LANES).astype(BF16)


def _prep_w_ukv(w):
    w = w.reshape(MLA_KV_LORA, MLA_HEADS, 2, MLA_NOPE_DIM).transpose(0, 2, 1, 3)
    return w.reshape(MLA_KV_LORA, 2 * MLA_HEADS * MLA_NOPE_DIM).astype(BF16)


def _rope_tables(positions):
    pos = positions.astype(F32)[:, None]

    def cs(dim):
        inv = ROPE_THETA ** (-jnp.arange(0, dim, 2, dtype=F32) / dim)
        ang = pos * inv
        return jnp.cos(ang), jnp.sin(ang)

    c128, s128 = cs(HEAD_DIM)
    c64, s64 = cs(MLA_ROPE_DIM)
    z = jnp.zeros_like(c64)
    one = jnp.ones((pos.shape[0], LANES), F32)
    zero = jnp.zeros((pos.shape[0], LANES), F32)
    a128, b128 = jnp.concatenate([c128, c128], 1), jnp.concatenate([-s128, s128], 1)
    a64, b64 = jnp.concatenate([c64, z, c64, z], 1), jnp.concatenate([-s64, z, s64, z], 1)
    return {"b": (jnp.stack([one, a128, a128 * C_HEAD]), jnp.stack([zero, b128, b128 * C_HEAD])),
            "f": (jnp.stack([one, a128, a64]), jnp.stack([zero, b128, b64])),
            "q": (jnp.stack([one * C_MLA, a64 * C_MLA]), jnp.stack([zero, b64 * C_MLA]))}


def _mixer(x, sc1, sh1, tabs, layer, w_in, cmp_pe, cmp_w1, cmp_w2, q_norm, kv_norm, w_uq, w_ukv):
    s_len = x.shape[0]
    wb, wf = _prep_w_in(w_in, layer)
    pb = mm_fullk(x, 0, D_MODEL, wb, prologue="mod", p1=sc1, p2=sh1, epilogue="rope", modes=_B_MODES,
                  ta=tabs["b"][0], tb=tabs["b"][1], out_dtype=BF16)
    pf = mm_fullk(x, 0, D_MODEL, wf, prologue="mod", p1=sc1, p2=sh1, epilogue="rope", modes=_F_MODES,
                  ta=tabs["f"][0], tb=tabs["f"][1], out_dtype=F32)

    blk = lambda name: _B_COL[name] // HEAD_DIM
    out_a = dilated_attention(pb, s_len, blk("a_q"), blk("a_k"), blk("a_v"))

    ncp = s_len // NSA_CMP_STRIDE

    def blocks16(name):
        t = pf[:, _F_COL[name]:_F_COL[name] + NSA_KVW]
        return t.reshape(ncp, NSA_CMP_STRIDE, NSA_KV_HEADS, HEAD_DIM).transpose(2, 0, 1, 3).reshape(
            NSA_KV_HEADS, ncp, NSA_CMP_STRIDE * HEAD_DIM)

    kv_cmp = nsa_compress(jnp.stack([blocks16("n_kc"), blocks16("n_vc")]), cmp_pe,
                          cmp_w1.astype(BF16), cmp_w2.astype(BF16))
    gates = pf[:, _F_COL["n_gate"]:_F_COL["n_gate"] + 3 * NSA_HEADS]
    gates = gates.reshape(s_len, NSA_KV_HEADS, 3 * NSA_GROUP).transpose(1, 0, 2)
    gw = NSA_GROUP * HEAD_DIM
    ks_t = pb[:, _B_COL["n_ks"]:_B_COL["n_ks"] + NSA_KVW].T
    u = nsa_cmp_sel(pb, ks_t, s_len, _B_COL["n_q"] // gw, _B_COL["n_vs"] // HEAD_DIM, kv_cmp, gates)
    out_b = nsa_window(pb, s_len, _B_COL["n_q"] // gw, _B_COL["n_kw"] // HEAD_DIM, _B_COL["n_vw"] // HEAD_DIM,
                       gates, u)

    q = mm_fullk(pf, 0, MLA_Q_LORA, _prep_w_uq(w_uq), prologue="rms", p1=q_norm, epilogue="rope",
                 modes=_Q_MODES, ta=tabs["q"][0], tb=tabs["q"][1], out_dtype=BF16)
    v, k_nope_t = mm_fullk(pf, _F_COL["m_ckv"] // MLA_KV_LORA, MLA_KV_LORA, _prep_w_ukv(w_ukv), prologue="rms",
                           p1=kv_norm, epilogue="ksplit", k_cols=MLA_HEADS * MLA_NOPE_DIM, out_dtype=BF16)
    k_rope_t = pf[:, _F_COL["m_kr"]:_F_COL["m_kr"] + LANES].astype(BF16).T
    out_c = mla_attention(q, v, k_nope_t, k_rope_t, s_len)
    return [out_a, out_b, out_c]


def kernel(x, c, positions, w_ada, b_ada, w_in, nsa_cmp_pe, nsa_cmp_w1, nsa_cmp_w2, mla_q_norm, mla_kv_norm,
           mla_w_uq, mla_w_ukv, w_out, ln1_g, ln1_b, mlp_w1, mlp_w2, ln2_g, ln2_b):
    assert x.shape[0] == 1, "kernel handles batch size 1"
    xs = x[0]
    d = xs.shape[1]
    tabs = _rope_tables(positions[0])
    w_out16, mlp_w1_16, mlp_w2_16 = w_out.astype(BF16), mlp_w1.astype(BF16), mlp_w2.astype(BF16)
    for l in range(DEPTH):
        mod = adaln(c, w_ada, b_ada, l)
        sh1, sc1, g1, sh2, sc2, g2 = [mod[:, i * d:(i + 1) * d] for i in range(6)]
        mixed = _mixer(xs, sc1, sh1, tabs, l, w_in, nsa_cmp_pe[l], nsa_cmp_w1[l], nsa_cmp_w2[l],
                       mla_q_norm[l], mla_kv_norm[l], mla_w_uq[l], mla_w_ukv[l])
        xs = mm_ln(mixed, w_out16, l, xs, g1, ln1_g[l], ln1_b[l])
        act = mm_fullk(xs, 0, d, mlp_w1_16, prologue="mod", p1=sc2, p2=sh2, epilogue="relu2", out_dtype=BF16,
                       w_layer=l)
        xs = mm_ln([act], mlp_w2_16, l, xs, g2, ln2_g[l], ln2_b[l], tk=1024, x_buffers=1)
    return xs[None]
```

```python
import functools

import jax
import jax.numpy as jnp
import numpy as np
from jax import lax
from jax.experimental import pallas as pl
from jax.experimental.pallas import tpu as pltpu

D_MODEL = 4096
DEPTH = 2
HEAD_DIM = 128
ROPE_THETA = 10000.0
QBLOCK = 128
DSW_HEADS = 8
DSW_PATTERNS = ((128, 1), (512, 4), (2048, 16))
NSA_HEADS = 8
NSA_KV_HEADS = 2
NSA_GROUP = NSA_HEADS // NSA_KV_HEADS
NSA_CMP_LEN = 32
NSA_CMP_STRIDE = 16
NSA_CMP_HIDDEN = 256
NSA_SEL_BLOCK = 64
NSA_TOP_N = 16
NSA_WINDOW = 512
MLA_HEADS = 16
MLA_Q_LORA = 1536
MLA_KV_LORA = 512
MLA_NOPE_DIM = 128
MLA_ROPE_DIM = 64
MLA_V_DIM = 128
D_FF = 4 * D_MODEL
ALPHA = (2 * DEPTH) ** 0.25

DSW_W = DSW_HEADS * HEAD_DIM
NSA_QW = NSA_HEADS * HEAD_DIM
NSA_KVW = NSA_KV_HEADS * HEAD_DIM

LANES = 128
V7X_VMEM_LIMIT = 56 * 1024 * 1024
V7X_VMEM_LIMIT_LARGE = 60 * 1024 * 1024

F32 = jnp.float32
BF16 = jnp.bfloat16
NEG_INF = float("-inf")
LOG2E = 1.4426950408889634
FLASH_CHUNK = 64 * 512
FLASH_UNROLL = 8
DIL_SUPER = QBLOCK * max(d for _, d in DSW_PATTERNS)
DIL_MIX_ROWS = 64
NT_DIMS = (((1,), (1,)), ((), ()))


def _params(sem, vmem=V7X_VMEM_LIMIT):
    return pltpu.CompilerParams(dimension_semantics=sem, vmem_limit_bytes=vmem)


def _adaln_body(cb_ref, w_ref, b_ref, o_ref, *, tn):
    cb = cb_ref[...]
    for s in range(tn // LANES):
        sl = slice(s * LANES, (s + 1) * LANES)
        o_ref[:, sl] = jnp.sum(w_ref[:, sl] * cb, axis=0, keepdims=True) + b_ref[:, sl]


def adaln(c, w, b, layer, tn=512):
    nl, d, n = w.shape
    cb = jnp.broadcast_to(c.reshape(d, 1), (d, LANES))
    return pl.pallas_call(
        functools.partial(_adaln_body, tn=tn),
        grid=(n // tn,),
        in_specs=[
            pl.BlockSpec((d, LANES), lambda j: (0, 0)),
            pl.BlockSpec((None, d, tn), lambda j: (layer, 0, j)),
            pl.BlockSpec((None, 1, tn), lambda j: (layer, 0, j)),
        ],
        out_specs=pl.BlockSpec((1, tn), lambda j: (0, j)),
        out_shape=jax.ShapeDtypeStruct((1, n), F32),
        compiler_params=_params(("arbitrary",)),
        name="adaln",
    )(cb, w, b.reshape(nl, 1, n))


def _mm_fullk_body(*refs, prologue, epilogue, nsub, ksplit):
    refs = list(refs)
    mode_ref = refs.pop(0) if epilogue == "rope" else None
    x_ref, p1_ref = refs.pop(0), refs.pop(0)
    p2_ref = refs.pop(0) if prologue == "mod" else None
    w_ref = refs.pop(0)
    if epilogue == "rope":
        ta_ref, tb_ref = refs.pop(0), refs.pop(0)
    if epilogue == "ksplit":
        o_ref, kt_ref, h_scr = refs
    else:
        o_ref, h_scr = refs
    j = pl.program_id(1)

    @pl.when(j == 0)
    def _():
        x = x_ref[...]
        if prologue == "mod":
            h = x * (1.0 + p1_ref[...]) + p2_ref[...]
        else:
            h = x * lax.rsqrt(jnp.mean(x * x, axis=-1, keepdims=True) + 1e-6) * p1_ref[...]
        h_scr[...] = h.astype(BF16)

    acc = jnp.dot(h_scr[...], w_ref[...], preferred_element_type=F32)
    if epilogue == "relu2":
        r = jnp.maximum(acc, 0.0)
        o_ref[...] = (r * r).astype(o_ref.dtype)
    elif epilogue == "rope":
        for s in range(nsub):
            sl = slice(s * LANES, (s + 1) * LANES)
            md = mode_ref[j * nsub + s]
            sub = acc[:, sl]
            o_ref[:, sl] = (sub * ta_ref[md] + pltpu.roll(sub, LANES // 2, 1) * tb_ref[md]).astype(o_ref.dtype)
    elif epilogue == "ksplit":
        @pl.when(j < ksplit)
        def _():
            kt_ref[...] = acc.T.astype(kt_ref.dtype)

        @pl.when(j >= ksplit)
        def _():
            o_ref[...] = acc.astype(o_ref.dtype)
    else:
        o_ref[...] = acc.astype(o_ref.dtype)


def mm_fullk(x, x_col_block, k, w, *, prologue, p1, p2=None, epilogue="none", modes=None, ta=None, tb=None,
             out_dtype=BF16, tm=512, tn=1024, w_layer=None, k_cols=0):
    m = x.shape[0]
    n = w.shape[-1]
    tm = min(tm, m)
    assert m % tm == 0 and n % tn == 0 and w.shape[-2] == k
    nsub = tn // LANES
    rope = epilogue == "rope"
    npre = 1 if rope else 0

    def im(f):
        return (lambda i, j, *_: f(i, j))

    in_specs = [pl.BlockSpec((tm, k), im(lambda i, j: (i, x_col_block))),
                pl.BlockSpec((1, k), im(lambda i, j: (0, 0)))]
    args = [x, p1.reshape(1, k)]
    if prologue == "mod":
        in_specs.append(pl.BlockSpec((1, k), im(lambda i, j: (0, 0))))
        args.append(p2.reshape(1, k))
    if w_layer is None:
        in_specs.append(pl.BlockSpec((k, tn), im(lambda i, j: (0, j))))
    else:
        in_specs.append(pl.BlockSpec((None, k, tn), im(lambda i, j: (w_layer, 0, j))))
    args.append(w)
    if rope:
        nmode = ta.shape[0]
        in_specs += [pl.BlockSpec((nmode, tm, LANES), im(lambda i, j: (0, i, 0)))] * 2
        args += [ta, tb]
    ksplit = k_cols // tn
    if epilogue == "ksplit":
        assert k_cols % tn == 0 and 0 < k_cols < n
        out_specs = [pl.BlockSpec((tm, tn), im(lambda i, j: (i, jnp.maximum(j - ksplit, 0)))),
                     pl.BlockSpec((tn, tm), im(lambda i, j: (jnp.minimum(j, ksplit - 1), i)))]
        out_shape = [jax.ShapeDtypeStruct((m, n - k_cols), out_dtype), jax.ShapeDtypeStruct((k_cols, m), out_dtype)]
    else:
        out_specs = pl.BlockSpec((tm, tn), im(lambda i, j: (i, j)))
        out_shape = jax.ShapeDtypeStruct((m, n), out_dtype)
    grid_spec = pltpu.PrefetchScalarGridSpec(
        num_scalar_prefetch=npre,
        grid=(m // tm, n // tn),
        in_specs=in_specs,
        out_specs=out_specs,
        scratch_shapes=[pltpu.VMEM((tm, k), BF16)],
    )
    fn = pl.pallas_call(
        functools.partial(_mm_fullk_body, prologue=prologue, epilogue=epilogue, nsub=nsub, ksplit=ksplit),
        grid_spec=grid_spec,
        out_shape=out_shape,
        compiler_params=_params(("parallel", "arbitrary")),
        name="mm_fullk_" + prologue + "_" + epilogue,
    )
    if rope:
        return fn(jnp.asarray(modes, jnp.int32), *args)
    return fn(*args)


def _mm_ln_body(*refs, nk, bounds):
    lhs_refs = refs[:len(bounds)]
    w_ref, x_ref, g_ref, lng_ref, lnb_ref, o_ref, acc_ref = refs[len(bounds):]
    kk = pl.program_id(1)

    def lhs():
        val = lhs_refs[-1][...]
        for ref, hi in zip(reversed(lhs_refs[:-1]), reversed(bounds[:-1])):
            val = jnp.where(kk < hi, ref[...], val)
        return val

    @pl.when(kk == 0)
    def _():
        acc_ref[...] = jnp.dot(lhs(), w_ref[...], preferred_element_type=F32)

    @pl.when(kk > 0)
    def _():
        acc_ref[...] += jnp.dot(lhs(), w_ref[...], preferred_element_type=F32)

    @pl.when(kk == nk - 1)
    def _():
        z = ALPHA * x_ref[...] + (1.0 + g_ref[...]) * acc_ref[...]
        mu = jnp.mean(z, axis=-1, keepdims=True)
        zc = z - mu
        var = jnp.mean(zc * zc, axis=-1, keepdims=True)
        o_ref[...] = zc * lax.rsqrt(var + 1e-5) * lng_ref[...] + lnb_ref[...]


def mm_ln(lhs_parts, w, w_layer, x, gate, ln_g, ln_b, tm=512, tk=512, x_buffers=2):
    m = lhs_parts[0].shape[0]
    k = sum(part.shape[1] for part in lhs_parts)
    n = w.shape[2]
    bounds, lhs_specs = [], []
    for part in lhs_parts:
        lo = bounds[-1] if bounds else 0
        hi = lo + part.shape[1] // tk
        bounds.append(hi)
        lhs_specs.append(pl.BlockSpec((tm, tk), lambda i, kk, lo=lo, hi=hi: (i, jnp.clip(kk - lo, 0, hi - lo - 1))))
    tm = min(tm, m)
    nk = k // tk
    row = lambda i, kk: (0, 0)
    return pl.pallas_call(
        functools.partial(_mm_ln_body, nk=nk, bounds=tuple(bounds)),
        grid=(m // tm, nk),
        in_specs=lhs_specs + [
            pl.BlockSpec((None, tk, n), lambda i, kk: (w_layer, kk, 0)),
            pl.BlockSpec((tm, n), lambda i, kk: (i, 0), pipeline_mode=pl.Buffered(x_buffers)),
            pl.BlockSpec((1, n), row), pl.BlockSpec((1, n), row), pl.BlockSpec((1, n), row),
        ],
        out_specs=pl.BlockSpec((tm, n), lambda i, kk: (i, 0)),
        out_shape=jax.ShapeDtypeStruct((m, n), F32),
        scratch_shapes=[pltpu.VMEM((tm, n), F32)],
        compiler_params=_params(("parallel", "arbitrary"), V7X_VMEM_LIMIT_LARGE),
        name="mm_ln",
    )(*lhs_parts, w, x, gate.reshape(1, n), ln_g.reshape(1, n), ln_b.reshape(1, n))


def _dilated_body(q_ref, kc_ref, kp_ref, vc_ref, vp_ref, o_ref, q32, kc32, kp32, vc32, vp32, od_scr, ld_scr):
    n = pl.program_id(1)
    q32[...] = q_ref[...].astype(F32)
    kc32[...] = kc_ref[...].astype(F32)
    kp32[...] = kp_ref[...].astype(F32)
    vc32[...] = vc_ref[...].astype(F32)
    vp32[...] = vp_ref[...].astype(F32)
    qi = lax.broadcasted_iota(jnp.int32, (QBLOCK, 2 * QBLOCK), 0)
    ki = lax.broadcasted_iota(jnp.int32, (QBLOCK, 2 * QBLOCK), 1)
    delta = QBLOCK + qi - ki
    in_band = (delta >= 0) & (delta <= QBLOCK)
    in_band_first = in_band & ((ki >= QBLOCK) | (n > 0))

    for pi, (window, dil) in enumerate(DSW_PATTERNS):
        assert window // dil == QBLOCK
        nblk = DIL_SUPER // (QBLOCK * dil)
        for mb in range(nblk):
            for r in range(dil):
                def rows(b):
                    return pl.ds(b * QBLOCK * dil + r, QBLOCK, stride=dil) if dil > 1 else pl.ds(b * QBLOCK, QBLOCK)

                cur = rows(mb)
                if mb > 0:
                    k_prev, v_prev, valid = kc32[rows(mb - 1), :], vc32[rows(mb - 1), :], in_band
                else:
                    k_prev, v_prev, valid = kp32[rows(nblk - 1), :], vp32[rows(nblk - 1), :], in_band_first
                k = jnp.concatenate([k_prev, kc32[cur, :]], axis=0).astype(BF16)
                v = jnp.concatenate([v_prev, vc32[cur, :]], axis=0).astype(BF16)
                s = lax.dot_general(q32[cur, :].astype(BF16), k, NT_DIMS, preferred_element_type=F32)
                s = jnp.where(valid, s, NEG_INF)
                m = jnp.max(s, axis=-1, keepdims=True)
                p = jnp.exp2(s - m)
                l = jnp.sum(p, axis=-1, keepdims=True)
                od_scr[pi, cur, :] = jnp.dot(p.astype(BF16), v, preferred_element_type=F32) / l
                ld_scr[pi, cur, :] = jnp.broadcast_to(m + jnp.log2(l), (QBLOCK, HEAD_DIM))

    npat = len(DSW_PATTERNS)
    for c0 in range(0, DIL_SUPER, DIL_MIX_ROWS):
        rs = slice(c0, c0 + DIL_MIX_ROWS)
        ls = [ld_scr[i, rs, :] for i in range(npat)]
        mx = functools.reduce(jnp.maximum, ls)
        ws = [jnp.exp2(t - mx) for t in ls]
        num = sum(ws[i] * od_scr[i, rs, :] for i in range(npat))
        o_ref[rs, :] = (num / sum(ws)).astype(o_ref.dtype)


def dilated_attention(qkv, s_len, q_blk, k_blk, v_blk):
    assert s_len % DIL_SUPER == 0
    cur = lambda blk: (lambda h, n: (n, blk + h))
    prev = lambda blk: (lambda h, n: (jnp.maximum(n - 1, 0), blk + h))
    bs = lambda f: pl.BlockSpec((DIL_SUPER, HEAD_DIM), f)
    f32_rows = pltpu.VMEM((DIL_SUPER, HEAD_DIM), F32)
    per_pattern = pltpu.VMEM((len(DSW_PATTERNS), DIL_SUPER, HEAD_DIM), F32)
    return pl.pallas_call(
        _dilated_body,
        grid=(DSW_HEADS, s_len // DIL_SUPER),
        in_specs=[bs(cur(q_blk)), bs(cur(k_blk)), bs(prev(k_blk)), bs(cur(v_blk)), bs(prev(v_blk))],
        out_specs=pl.BlockSpec((DIL_SUPER, HEAD_DIM), lambda h, n: (n, h)),
        out_shape=jax.ShapeDtypeStruct((s_len, DSW_W), BF16),
        scratch_shapes=[f32_rows] * 5 + [per_pattern] * 2,
        compiler_params=_params(("parallel", "parallel")),
        name="dilated_attention",
    )(qkv, qkv, qkv, qkv, qkv)


def _cmp_body(x_ref, pe_ref, w1_ref, w2_ref, o_ref, *, ncp):
    x = x_ref[...]
    half = NSA_CMP_STRIDE * HEAD_DIM
    a = jnp.dot((x + pe_ref[0:1, :]).astype(BF16), w1_ref[0:half, :], preferred_element_type=F32)
    b = jnp.dot((x + pe_ref[1:2, :]).astype(BF16), w1_ref[half:2 * half, :], preferred_element_type=F32)
    hid = a + pltpu.roll(b, ncp - 1, 0)
    act = jax.nn.gelu(hid)
    o_ref[...] = jnp.dot(act.astype(BF16), w2_ref[...], preferred_element_type=F32).astype(o_ref.dtype)


def nsa_compress(xs, pe, w1, w2):
    ncp = xs.shape[2]
    half = NSA_CMP_STRIDE * HEAD_DIM
    return pl.pallas_call(
        functools.partial(_cmp_body, ncp=ncp),
        grid=(2, NSA_KV_HEADS),
        in_specs=[
            pl.BlockSpec((None, None, ncp, half), lambda a, h: (a, h, 0, 0)),
            pl.BlockSpec((None, 2, half), lambda a, h: (a, 0, 0)),
            pl.BlockSpec((None, 2 * half, NSA_CMP_HIDDEN), lambda a, h: (a, 0, 0)),
            pl.BlockSpec((None, NSA_CMP_HIDDEN, HEAD_DIM), lambda a, h: (a, 0, 0)),
        ],
        out_specs=pl.BlockSpec((None, None, ncp, HEAD_DIM), lambda a, h: (a, h, 0, 0)),
        out_shape=jax.ShapeDtypeStruct((2, NSA_KV_HEADS, ncp, HEAD_DIM), BF16),
        compiler_params=_params(("parallel", "parallel")),
        name="nsa_compress",
    )(xs, pe.reshape(2, 2, half), w1, w2)


def _stack_heads(q):
    return jnp.concatenate([q[:, g * HEAD_DIM:(g + 1) * HEAD_DIM] for g in range(NSA_GROUP)], axis=0)


def _nsa_body(q_ref, kst_ref, vs_ref, kc_ref, vc_ref, covt_ref, e0_ref, gate_ref, u_ref,
              sel_scr, q4_scr, sc_scr, pf_scr, pc_scr, mk_scr, ve_scr, s_scr, p_scr, a_scr, m_scr, acc_scr,
              *, ncp, nselp, ntop, tk):
    n = pl.program_id(1)
    rows = NSA_GROUP * QBLOCK
    q4_scr[...] = _stack_heads(q_ref[...])
    q4 = q4_scr[...]

    sc_scr[...] = lax.dot_general(q4, kc_ref[...], NT_DIMS, preferred_element_type=F32)
    chunk = min(FLASH_CHUNK // ncp, rows)
    cend = lax.broadcasted_iota(jnp.int32, (chunk, ncp), 1) * NSA_CMP_STRIDE + (NSA_CMP_LEN - 1)
    for r in range(0, rows, chunk):
        rs = slice(r, r + chunk)
        qpos_r = n * QBLOCK + ((r + lax.broadcasted_iota(jnp.int32, (chunk, ncp), 0)) & (QBLOCK - 1))
        s = jnp.where(cend <= qpos_r, sc_scr[rs, :], NEG_INF)
        m = jnp.max(s, axis=-1, keepdims=True)
        e = jnp.exp2(s - jnp.where(m == NEG_INF, 0.0, m))
        l = jnp.sum(e, axis=-1, keepdims=True)
        p = e / jnp.where(l > 0, l, 1.0)
        pf_scr[rs, :] = p
        pc_scr[rs, :] = p.astype(BF16)
    o_c = jnp.dot(pc_scr[...], vc_ref[...], preferred_element_type=F32)

    psum = pf_scr[0:QBLOCK, :]
    for g in range(1, NSA_GROUP):
        psum = psum + pf_scr[g * QBLOCK:(g + 1) * QBLOCK, :]
    p_hi = psum.astype(BF16)
    p_lo = (psum - p_hi.astype(F32)).astype(BF16)
    covt = covt_ref[...]
    imp_t = (lax.dot_general(covt, p_hi, NT_DIMS, preferred_element_type=F32)
             + lax.dot_general(covt, p_lo, NT_DIMS, preferred_element_type=F32))

    jblk = lax.broadcasted_iota(jnp.int32, (nselp, QBLOCK), 0)
    qpos_c = n * QBLOCK + lax.broadcasted_iota(jnp.int32, (nselp, QBLOCK), 1)
    cur = qpos_c // NSA_SEL_BLOCK
    forced = (jblk == 0) | (jblk == cur) | (jblk == cur - 1)
    valid = jblk * NSA_SEL_BLOCK <= qpos_c
    score0 = jnp.where(valid, jnp.where(forced, jnp.inf, imp_t), NEG_INF)

    def pick(_, carry):
        score, picked = carry
        mx = jnp.max(score, axis=0, keepdims=True)
        idx = jnp.min(jnp.where(score == mx, jblk, nselp), axis=0, keepdims=True)
        hit = jblk == idx
        return jnp.where(hit, NEG_INF, score), jnp.where(hit, 1.0, picked)

    _, picked = lax.fori_loop(0, ntop, pick, (score0, jnp.zeros((nselp, QBLOCK), F32)))
    sel_scr[...] = jnp.where(valid, picked, 0.0).T

    blocks_per_tile = tk // NSA_SEL_BLOCK
    _flash_init(m_scr, acc_scr, ve_scr)

    def k_tile(t, slot, buf):
        shift = (nselp - t * blocks_per_tile) % nselp
        sel_t = pltpu.roll(sel_scr[...], shift, 1)[:, 0:LANES].astype(BF16)
        mk_scr[slot] = jnp.dot(sel_t, e0_ref[...], preferred_element_type=F32)
        return kst_ref[:, pl.ds(pl.multiple_of(t * tk, tk), tk)]

    def v_tile(t, buf):
        ve_scr[buf, :, 0:HEAD_DIM] = vs_ref[pl.ds(pl.multiple_of(t * tk, tk), tk), :]
        return ve_scr[buf]

    def mask_for(slot, t, masked):
        def mask_fn(r, s):
            rq = r % QBLOCK
            ok = mk_scr[slot, rq:rq + s.shape[0], :] > 0.5
            if masked:
                qpos = n * QBLOCK + rq + lax.broadcasted_iota(jnp.int32, s.shape, 0)
                ok = ok & (t * tk + lax.broadcasted_iota(jnp.int32, s.shape, 1) <= qpos)
            return jnp.where(ok, s, NEG_INF)
        return mask_fn

    _flash_pipeline((n * QBLOCK + QBLOCK - 1) // tk, q4_scr, k_tile, v_tile, mask_for,
                    s_scr, p_scr, a_scr, m_scr, acc_scr)
    o_s = _flash_output(acc_scr)

    gates = jax.nn.sigmoid(gate_ref[...])
    for g in range(NSA_GROUP):
        rs = slice(g * QBLOCK, (g + 1) * QBLOCK)
        u_ref[:, g * HEAD_DIM:(g + 1) * HEAD_DIM] = (gates[:, 3 * g:3 * g + 1] * o_c[rs]
                                                     + gates[:, 3 * g + 1:3 * g + 2] * o_s[rs])


def nsa_cmp_sel(qkv, ks_t, s_len, q_blk, vs_blk, kv_cmp, gates, tk=512):
    nb = s_len // QBLOCK
    ncp = s_len // NSA_CMP_STRIDE
    nsel = s_len // NSA_SEL_BLOCK
    nselp = -(-nsel // LANES) * LANES
    ntop = min(NSA_TOP_N, nsel)
    ci = np.arange(ncp)[None, :] * NSA_CMP_STRIDE
    sj = np.arange(nselp)[:, None] * NSA_SEL_BLOCK
    cov = (ci < sj + NSA_SEL_BLOCK) & (ci + NSA_CMP_LEN > sj) & (np.arange(ncp)[None, :] < ncp - 1) & (sj < s_len)
    covt = jnp.asarray(cov.astype(np.float32), BF16)
    e0 = jnp.asarray((np.arange(tk)[None, :] // NSA_SEL_BLOCK == np.arange(LANES)[:, None]).astype(np.float32), BF16)
    gw = NSA_GROUP * HEAD_DIM
    rows = NSA_GROUP * QBLOCK
    return pl.pallas_call(
        functools.partial(_nsa_body, ncp=ncp, nselp=nselp, ntop=ntop, tk=tk),
        grid=(NSA_KV_HEADS, nb),
        in_specs=[
            pl.BlockSpec((QBLOCK, gw), lambda h, n: (n, q_blk + h)),
            pl.BlockSpec((HEAD_DIM, s_len), lambda h, n: (h, 0)),
            pl.BlockSpec((s_len, HEAD_DIM), lambda h, n: (0, vs_blk + h)),
            pl.BlockSpec((None, None, ncp, HEAD_DIM), lambda h, n: (0, h, 0, 0)),
            pl.BlockSpec((None, None, ncp, HEAD_DIM), lambda h, n: (1, h, 0, 0)),
            pl.BlockSpec((nselp, ncp), lambda h, n: (0, 0)),
            pl.BlockSpec((LANES, tk), lambda h, n: (0, 0)),
            pl.BlockSpec((None, QBLOCK, 3 * NSA_GROUP), lambda h, n: (h, n, 0)),
        ],
        out_specs=pl.BlockSpec((QBLOCK, gw), lambda h, n: (n, h)),
        out_shape=jax.ShapeDtypeStruct((s_len, NSA_QW), F32),
        scratch_shapes=[pltpu.VMEM((QBLOCK, nselp), F32), pltpu.VMEM((rows, HEAD_DIM), BF16),
                        pltpu.VMEM((rows, ncp), F32), pltpu.VMEM((rows, ncp), F32), pltpu.VMEM((rows, ncp), BF16),
                        pltpu.VMEM((FLASH_UNROLL, QBLOCK, tk), F32), pltpu.VMEM((FLASH_UNROLL + 1, tk, 2 * HEAD_DIM), BF16),
                        pltpu.VMEM((FLASH_UNROLL, rows, tk), F32), pltpu.VMEM((FLASH_UNROLL, rows, tk), BF16),
                        pltpu.VMEM((FLASH_UNROLL, rows, LANES), F32), pltpu.VMEM((rows, LANES), F32),
                        pltpu.VMEM((rows, 2 * HEAD_DIM), F32)],
        compiler_params=_params(("parallel", "arbitrary")),
        name="nsa_cmp_sel",
    )(qkv, ks_t, qkv, kv_cmp, kv_cmp, covt, e0, gates)


def _win_body(q_ref, kw_ref, vw_ref, gate_ref, u_ref, o_ref, *, span):
    n = pl.program_id(1)
    rows = NSA_GROUP * QBLOCK
    q4 = _stack_heads(q_ref[...])
    start = pl.multiple_of(jnp.maximum(n * QBLOCK - (span - QBLOCK), 0), QBLOCK)
    s = lax.dot_general(q4, kw_ref[pl.ds(start, span), :], NT_DIMS, preferred_element_type=F32)
    qpos = n * QBLOCK + (lax.broadcasted_iota(jnp.int32, (rows, span), 0) & (QBLOCK - 1))
    delta = qpos - (start + lax.broadcasted_iota(jnp.int32, (rows, span), 1))
    s = jnp.where((delta >= 0) & (delta <= NSA_WINDOW - 1), s, NEG_INF)
    m = jnp.max(s, axis=-1, keepdims=True)
    p = jnp.exp2(s - m)
    l = jnp.sum(p, axis=-1, keepdims=True)
    o_w = jnp.dot(p.astype(BF16), vw_ref[pl.ds(start, span), :], preferred_element_type=F32) / l
    gates = jax.nn.sigmoid(gate_ref[...])
    for g in range(NSA_GROUP):
        sl = slice(g * HEAD_DIM, (g + 1) * HEAD_DIM)
        o_ref[:, sl] = (u_ref[:, sl] + gates[:, 3 * g + 2:3 * g + 3] * o_w[g * QBLOCK:(g + 1) * QBLOCK]
                        ).astype(o_ref.dtype)


def nsa_window(qkv, s_len, q_blk, kw_blk, vw_blk, gates, u):
    nb = s_len // QBLOCK
    span = (-(-(NSA_WINDOW - 1) // QBLOCK) + 1) * QBLOCK
    gw = NSA_GROUP * HEAD_DIM
    return pl.pallas_call(
        functools.partial(_win_body, span=span),
        grid=(NSA_KV_HEADS, nb),
        in_specs=[
            pl.BlockSpec((QBLOCK, gw), lambda h, n: (n, q_blk + h)),
            pl.BlockSpec((s_len, HEAD_DIM), lambda h, n: (0, kw_blk + h)),
            pl.BlockSpec((s_len, HEAD_DIM), lambda h, n: (0, vw_blk + h)),
            pl.BlockSpec((None, QBLOCK, 3 * NSA_GROUP), lambda h, n: (h, n, 0)),
            pl.BlockSpec((QBLOCK, gw), lambda h, n: (n, h)),
        ],
        out_specs=pl.BlockSpec((QBLOCK, gw), lambda h, n: (n, h)),
        out_shape=jax.ShapeDtypeStruct((s_len, NSA_QW), BF16),
        compiler_params=_params(("parallel", "parallel")),
        name="nsa_window",
    )(qkv, qkv, qkv, gates, u)


def _softmax_rows(s_ref, p_ref, a_ref, m_scr, mask_fn):
    rows, tk = s_ref.shape
    nrep = tk // LANES
    chunk = FLASH_CHUNK // tk
    for r in range(0, rows, chunk):
        rs = slice(r, r + chunk)
        s = s_ref[rs, :]
        if mask_fn is not None:
            s = mask_fn(r, s)
        m_old = m_scr[rs, :]
        m_new = jnp.maximum(m_old, jnp.max(s, axis=1, keepdims=True))
        p_ref[rs, :] = jnp.exp2(s - jnp.concatenate([m_new] * nrep, axis=1)).astype(BF16)
        a_ref[rs, :] = jnp.exp2(m_old - m_new)
        m_scr[rs, :] = m_new


def _flash_pipeline(n_full, q_ref, k_tile, v_tile, mask_for, s_scr, p_scr, a_scr, m_scr, acc_scr):
    def scores(t, idx, buf):
        s_scr[idx] = jnp.dot(q_ref[...], k_tile(t, idx, buf), preferred_element_type=F32)

    def softmax(idx, t, masked):
        _softmax_rows(s_scr.at[idx], p_scr.at[idx], a_scr.at[idx], m_scr, mask_for(idx, t, masked))

    def values(idx, t, buf):
        a = a_scr[idx]
        acc_scr[...] = (jnp.concatenate([a, a], axis=1) * acc_scr[...]
                        + jnp.dot(p_scr[idx], v_tile(t, buf), preferred_element_type=F32))

    last = FLASH_UNROLL - 1
    p_scr[last] = jnp.zeros(p_scr.shape[1:], p_scr.dtype)
    a_scr[last] = jnp.ones(a_scr.shape[1:], a_scr.dtype)
    scores(0, 0, 0)

    def run(t0, count, masked_last, lookahead):
        for j in range(count):
            t = t0 + j
            if lookahead or j + 1 < count:
                scores(t + 1, (j + 1) % FLASH_UNROLL, j)
            softmax(j, t, masked_last and j == count - 1)
            values((j - 1) % FLASH_UNROLL, jnp.maximum(t - 1, 0), j)
        if not lookahead:
            values(count - 1, t0 + count - 1, count)

    def body(u, carry):
        run(FLASH_UNROLL * u, FLASH_UNROLL, False, True)
        return carry

    lax.fori_loop(0, n_full // FLASH_UNROLL, body, 0)
    rem = n_full % FLASH_UNROLL
    for r in range(FLASH_UNROLL):
        @pl.when(rem == r)
        def _(r=r):
            run(n_full - r, r + 1, True, False)


def _flash_init(m_scr, acc_scr, ve_scr):
    dv = ve_scr.shape[2] // 2
    m_scr[...] = jnp.full(m_scr.shape, NEG_INF, F32)
    acc_scr[...] = jnp.zeros(acc_scr.shape, F32)
    ve_scr[:, :, dv:] = jnp.ones((ve_scr.shape[0], ve_scr.shape[1], dv), ve_scr.dtype)


def _flash_output(acc_scr):
    dv = acc_scr.shape[1] // 2
    return acc_scr[:, 0:dv] / acc_scr[:, dv:]


def _mla_body(q_ref, knt_ref, krt_ref, v_ref, o_ref, kt_scr, ve_scr, s_scr, p_scr, a_scr, m_scr, acc_scr,
              *, tq, tk):
    qi = pl.program_id(1)
    _flash_init(m_scr, acc_scr, ve_scr)

    def mask_for(slot, t, masked):
        if not masked:
            return None

        def causal(r, s):
            qpos = qi * tq + r + lax.broadcasted_iota(jnp.int32, s.shape, 0)
            kpos = t * tk + lax.broadcasted_iota(jnp.int32, s.shape, 1)
            return jnp.where(kpos <= qpos, s, NEG_INF)
        return causal

    def k_tile(t, slot, buf):
        k0 = pl.multiple_of(t * tk, tk)
        kt_scr[buf, 0:MLA_NOPE_DIM, :] = knt_ref[:, pl.ds(k0, tk)]
        kt_scr[buf, MLA_NOPE_DIM:, :] = krt_ref[:, pl.ds(k0, tk)]
        return kt_scr[buf]

    def v_tile(t, buf):
        ve_scr[buf, :, 0:MLA_V_DIM] = v_ref[pl.ds(pl.multiple_of(t * tk, tk), tk), :]
        return ve_scr[buf]

    _flash_pipeline((qi * tq) // tk, q_ref, k_tile, v_tile, mask_for, s_scr, p_scr, a_scr, m_scr, acc_scr)
    o_ref[...] = _flash_output(acc_scr).astype(o_ref.dtype)


def mla_attention(q, v, k_nope_t, k_rope_t, s_len, tq=512, tk=512):
    tq, tk = min(tq, s_len), min(tk, s_len)
    qw = 2 * LANES
    return pl.pallas_call(
        functools.partial(_mla_body, tq=tq, tk=tk),
        grid=(MLA_HEADS, s_len // tq),
        in_specs=[
            pl.BlockSpec((tq, qw), lambda h, i: (i, h)),
            pl.BlockSpec((MLA_NOPE_DIM, s_len), lambda h, i: (h, 0)),
            pl.BlockSpec((LANES, s_len), lambda h, i: (0, 0)),
            pl.BlockSpec((s_len, MLA_V_DIM), lambda h, i: (0, h)),
        ],
        out_specs=pl.BlockSpec((tq, MLA_V_DIM), lambda h, i: (i, h)),
        out_shape=jax.ShapeDtypeStruct((s_len, MLA_HEADS * MLA_V_DIM), BF16),
        scratch_shapes=[pltpu.VMEM((FLASH_UNROLL, qw, tk), BF16),
                        pltpu.VMEM((FLASH_UNROLL + 1, tk, 2 * MLA_V_DIM), BF16),
                        pltpu.VMEM((FLASH_UNROLL, tq, tk), F32), pltpu.VMEM((FLASH_UNROLL, tq, tk), BF16),
                        pltpu.VMEM((FLASH_UNROLL, tq, LANES), F32), pltpu.VMEM((tq, LANES), F32),
                        pltpu.VMEM((tq, 2 * MLA_V_DIM), F32)],
        compiler_params=_params(("parallel", "parallel")),
        name="mla_attention",
    )(q, k_nope_t, k_rope_t, v)


_IN_SIZES = (DSW_W, DSW_W, DSW_W, NSA_QW, NSA_KVW, NSA_KVW, NSA_KVW, NSA_KVW, NSA_KVW, NSA_KVW,
             3 * NSA_HEADS, MLA_Q_LORA, MLA_KV_LORA, MLA_ROPE_DIM)
_IN_NAMES = ("a_q", "a_k", "a_v", "n_q", "n_kc", "n_vc", "n_ks", "n_vs", "n_kw", "n_vw", "n_gate",
             "m_cq", "m_ckv", "m_kr")
_IN_OFF = dict(zip(_IN_NAMES, np.concatenate([[0], np.cumsum(_IN_SIZES)[:-1]]).tolist()))
_IN_LEN = dict(zip(_IN_NAMES, _IN_SIZES))

_B_ORDER = ("a_q", "a_k", "n_q", "n_ks", "n_kw", "n_vs", "n_vw", "a_v")
_B_ROPE = 2 * DSW_W + NSA_QW + 2 * NSA_KVW
_B_WIDTH = sum(_IN_LEN[k] for k in _B_ORDER)
_B_COL = dict(zip(_B_ORDER, np.concatenate([[0], np.cumsum([_IN_LEN[k] for k in _B_ORDER])[:-1]]).tolist()))
_F_COL = {"m_cq": 0, "m_ckv": MLA_Q_LORA, "n_kc": 2048, "n_vc": 2304, "m_kr": 2560, "n_gate": 2688}
_F_WIDTH = 3072
_HALF_ROPE = MLA_ROPE_DIM // 2


def _prep_w_in_body(w_ref, wb_ref, wf_ref):
    for name in _B_ORDER:
        wb_ref[:, _B_COL[name]:_B_COL[name] + _IN_LEN[name]] = (
            w_ref[:, _IN_OFF[name]:_IN_OFF[name] + _IN_LEN[name]].astype(BF16))
    wf_ref[...] = jnp.zeros(wf_ref.shape, BF16)
    for name in ("m_cq", "m_ckv", "n_kc", "n_vc", "n_gate"):
        wf_ref[:, _F_COL[name]:_F_COL[name] + _IN_LEN[name]] = (
            w_ref[:, _IN_OFF[name]:_IN_OFF[name] + _IN_LEN[name]].astype(BF16))
    kr_src, kr_dst = _IN_OFF["m_kr"], _F_COL["m_kr"]
    for half in range(2):
        wf_ref[:, kr_dst + 2 * half * _HALF_ROPE:kr_dst + (2 * half + 1) * _HALF_ROPE] = (
            w_ref[:, kr_src + half * _HALF_ROPE:kr_src + (half + 1) * _HALF_ROPE].astype(BF16))


def _prep_w_in(w, layer, tm=256):
    nl, k, n = w.shape
    return pl.pallas_call(
        _prep_w_in_body,
        grid=(k // tm,),
        in_specs=[pl.BlockSpec((None, tm, n), lambda i: (layer, i, 0))],
        out_specs=[pl.BlockSpec((tm, _B_WIDTH), lambda i: (i, 0)), pl.BlockSpec((tm, _F_WIDTH), lambda i: (i, 0))],
        out_shape=[jax.ShapeDtypeStruct((k, _B_WIDTH), BF16), jax.ShapeDtypeStruct((k, _F_WIDTH), BF16)],
        compiler_params=_params(("parallel",)),
        name="prep_w_in",
    )(w)


_B_MODES = ([2] * (DSW_W // LANES) + [1] * (DSW_W // LANES) + [2] * (NSA_QW // LANES)
            + [1] * (2 * NSA_KVW // LANES) + [0] * ((_B_WIDTH - _B_ROPE) // LANES))
_F_MODES = [0] * 16 + [1, 1, 0, 0, 2] + [0] * 3
_Q_MODES = [0, 1] * MLA_HEADS
C_HEAD = HEAD_DIM ** -0.5 * LOG2E
C_MLA = (MLA_NOPE_DIM + MLA_ROPE_DIM) ** -0.5 * LOG2E


def _prep_w_uq(w):
    w = w.reshape(MLA_Q_LORA, MLA_HEADS, MLA_NOPE_DIM + MLA_ROPE_DIM)
    z = jnp.zeros((MLA_Q_LORA, MLA_HEADS, _HALF_ROPE), w.dtype)
    w = jnp.concatenate([w[..., :MLA_NOPE_DIM], w[..., MLA_NOPE_DIM:MLA_NOPE_DIM + _HALF_ROPE], z,
                         w[..., MLA_NOPE_DIM + _HALF_ROPE:], z], axis=-1)
    return w.reshape(MLA_Q_LORA, MLA_HEADS * 2 * LANES).astype(BF16)


def _prep_w_ukv(w):
    w = w.reshape(MLA_KV_LORA, MLA_HEADS, 2, MLA_NOPE_DIM).transpose(0, 2, 1, 3)
    return w.reshape(MLA_KV_LORA, 2 * MLA_HEADS * MLA_NOPE_DIM).astype(BF16)


def _rope_tables(positions):
    pos = positions.astype(F32)[:, None]

    def cs(dim):
        inv = ROPE_THETA ** (-jnp.arange(0, dim, 2, dtype=F32) / dim)
        ang = pos * inv
        return jnp.cos(ang), jnp.sin(ang)

    c128, s128 = cs(HEAD_DIM)
    c64, s64 = cs(MLA_ROPE_DIM)
    z = jnp.zeros_like(c64)
    one = jnp.ones((pos.shape[0], LANES), F32)
    zero = jnp.zeros((pos.shape[0], LANES), F32)
    a128, b128 = jnp.concatenate([c128, c128], 1), jnp.concatenate([-s128, s128], 1)
    a64, b64 = jnp.concatenate([c64, z, c64, z], 1), jnp.concatenate([-s64, z, s64, z], 1)
    return {"b": (jnp.stack([one, a128, a128 * C_HEAD]), jnp.stack([zero, b128, b128 * C_HEAD])),
            "f": (jnp.stack([one, a128, a64]), jnp.stack([zero, b128, b64])),
            "q": (jnp.stack([one * C_MLA, a64 * C_MLA]), jnp.stack([zero, b64 * C_MLA]))}


def _mixer(x, sc1, sh1, tabs, layer, w_in, cmp_pe, cmp_w1, cmp_w2, q_norm, kv_norm, w_uq, w_ukv):
    s_len = x.shape[0]
    wb, wf = _prep_w_in(w_in, layer)
    pb = mm_fullk(x, 0, D_MODEL, wb, prologue="mod", p1=sc1, p2=sh1, epilogue="rope", modes=_B_MODES,
                  ta=tabs["b"][0], tb=tabs["b"][1], out_dtype=BF16)
    pf = mm_fullk(x, 0, D_MODEL, wf, prologue="mod", p1=sc1, p2=sh1, epilogue="rope", modes=_F_MODES,
                  ta=tabs["f"][0], tb=tabs["f"][1], out_dtype=F32)

    blk = lambda name: _B_COL[name] // HEAD_DIM
    out_a = dilated_attention(pb, s_len, blk("a_q"), blk("a_k"), blk("a_v"))

    ncp = s_len // NSA_CMP_STRIDE

    def blocks16(name):
        t = pf[:, _F_COL[name]:_F_COL[name] + NSA_KVW]
        return t.reshape(ncp, NSA_CMP_STRIDE, NSA_KV_HEADS, HEAD_DIM).transpose(2, 0, 1, 3).reshape(
            NSA_KV_HEADS, ncp, NSA_CMP_STRIDE * HEAD_DIM)

    kv_cmp = nsa_compress(jnp.stack([blocks16("n_kc"), blocks16("n_vc")]), cmp_pe,
                          cmp_w1.astype(BF16), cmp_w2.astype(BF16))
    gates = pf[:, _F_COL["n_gate"]:_F_COL["n_gate"] + 3 * NSA_HEADS]
    gates = gates.reshape(s_len, NSA_KV_HEADS, 3 * NSA_GROUP).transpose(1, 0, 2)
    gw = NSA_GROUP * HEAD_DIM
    ks_t = pb[:, _B_COL["n_ks"]:_B_COL["n_ks"] + NSA_KVW].T
    u = nsa_cmp_sel(pb, ks_t, s_len, _B_COL["n_q"] // gw, _B_COL["n_vs"] // HEAD_DIM, kv_cmp, gates)
    out_b = nsa_window(pb, s_len, _B_COL["n_q"] // gw, _B_COL["n_kw"] // HEAD_DIM, _B_COL["n_vw"] // HEAD_DIM,
                       gates, u)

    q = mm_fullk(pf, 0, MLA_Q_LORA, _prep_w_uq(w_uq), prologue="rms", p1=q_norm, epilogue="rope",
                 modes=_Q_MODES, ta=tabs["q"][0], tb=tabs["q"][1], out_dtype=BF16)
    v, k_nope_t = mm_fullk(pf, _F_COL["m_ckv"] // MLA_KV_LORA, MLA_KV_LORA, _prep_w_ukv(w_ukv), prologue="rms",
                           p1=kv_norm, epilogue="ksplit", k_cols=MLA_HEADS * MLA_NOPE_DIM, out_dtype=BF16)
    k_rope_t = pf[:, _F_COL["m_kr"]:_F_COL["m_kr"] + LANES].astype(BF16).T
    out_c = mla_attention(q, v, k_nope_t, k_rope_t, s_len)
    return [out_a, out_b, out_c]


def kernel(x, c, positions, w_ada, b_ada, w_in, nsa_cmp_pe, nsa_cmp_w1, nsa_cmp_w2, mla_q_norm, mla_kv_norm,
           mla_w_uq, mla_w_ukv, w_out, ln1_g, ln1_b, mlp_w1, mlp_w2, ln2_g, ln2_b):
    assert x.shape[0] == 1, "kernel handles batch size 1"
    xs = x[0]
    d = xs.shape[1]
    tabs = _rope_tables(positions[0])
    w_out16, mlp_w1_16, mlp_w2_16 = w_out.astype(BF16), mlp_w1.astype(BF16), mlp_w2.astype(BF16)
    for l in range(DEPTH):
        mod = adaln(c, w_ada, b_ada, l)
        sh1, sc1, g1, sh2, sc2, g2 = [mod[:, i * d:(i + 1) * d] for i in range(6)]
        mixed = _mixer(xs, sc1, sh1, tabs, l, w_in, nsa_cmp_pe[l], nsa_cmp_w1[l], nsa_cmp_w2[l],
                       mla_q_norm[l], mla_kv_norm[l], mla_w_uq[l], mla_w_ukv[l])
        xs = mm_ln(mixed, w_out16, l, xs, g1, ln1_g[l], ln1_b[l])
        act = mm_fullk(xs, 0, d, mlp_w1_16, prologue="mod", p1=sc2, p2=sh2, epilogue="relu2", out_dtype=BF16,
                       w_layer=l)
        xs = mm_ln([act], mlp_w2_16, l, xs, g2, ln2_g[l], ln2_b[l], tk=1024, x_buffers=1)
    return xs[None]
```

```python
import functools

import jax
import jax.numpy as jnp
import numpy as np
from jax import lax
from jax.experimental import pallas as pl
from jax.experimental.pallas import tpu as pltpu

D_MODEL = 4096
DEPTH = 2
HEAD_DIM = 128
ROPE_THETA = 10000.0
QBLOCK = 128
DSW_HEADS = 8
DSW_PATTERNS = ((128, 1), (512, 4), (2048, 16))
NSA_HEADS = 8
NSA_KV_HEADS = 2
NSA_GROUP = NSA_HEADS // NSA_KV_HEADS
NSA_CMP_LEN = 32
NSA_CMP_STRIDE = 16
NSA_CMP_HIDDEN = 256
NSA_SEL_BLOCK = 64
NSA_TOP_N = 16
NSA_WINDOW = 512
MLA_HEADS = 16
MLA_Q_LORA = 1536
MLA_KV_LORA = 512
MLA_NOPE_DIM = 128
MLA_ROPE_DIM = 64
MLA_V_DIM = 128
D_FF = 4 * D_MODEL
ALPHA = (2 * DEPTH) ** 0.25

DSW_W = DSW_HEADS * HEAD_DIM
NSA_QW = NSA_HEADS * HEAD_DIM
NSA_KVW = NSA_KV_HEADS * HEAD_DIM

LANES = 128
V7X_VMEM_LIMIT = 56 * 1024 * 1024
V7X_VMEM_LIMIT_LARGE = 60 * 1024 * 1024

F32 = jnp.float32
BF16 = jnp.bfloat16
NEG_INF = float("-inf")
LOG2E = 1.4426950408889634
FLASH_CHUNK = 64 * 512
FLASH_UNROLL = 8
DIL_SUPER = QBLOCK * max(d for _, d in DSW_PATTERNS)
DIL_MIX_ROWS = 64
NT_DIMS = (((1,), (1,)), ((), ()))


def _params(sem, vmem=V7X_VMEM_LIMIT):
    return pltpu.CompilerParams(dimension_semantics=sem, vmem_limit_bytes=vmem)


def _adaln_body(cb_ref, w_ref, b_ref, o_ref, *, tn):
    cb = cb_ref[...]
    for s in range(tn // LANES):
        sl = slice(s * LANES, (s + 1) * LANES)
        o_ref[:, sl] = jnp.sum(w_ref[:, sl] * cb, axis=0, keepdims=True) + b_ref[:, sl]


def adaln(c, w, b, layer, tn=512):
    nl, d, n = w.shape
    cb = jnp.broadcast_to(c.reshape(d, 1), (d, LANES))
    return pl.pallas_call(
        functools.partial(_adaln_body, tn=tn),
        grid=(n // tn,),
        in_specs=[
            pl.BlockSpec((d, LANES), lambda j: (0, 0)),
            pl.BlockSpec((None, d, tn), lambda j: (layer, 0, j)),
            pl.BlockSpec((None, 1, tn), lambda j: (layer, 0, j)),
        ],
        out_specs=pl.BlockSpec((1, tn), lambda j: (0, j)),
        out_shape=jax.ShapeDtypeStruct((1, n), F32),
        compiler_params=_params(("arbitrary",)),
        name="adaln",
    )(cb, w, b.reshape(nl, 1, n))


def _mm_fullk_body(*refs, prologue, epilogue, nsub, ksplit):
    refs = list(refs)
    mode_ref = refs.pop(0) if epilogue == "rope" else None
    x_ref, p1_ref = refs.pop(0), refs.pop(0)
    p2_ref = refs.pop(0) if prologue == "mod" else None
    w_ref = refs.pop(0)
    if epilogue == "rope":
        ta_ref, tb_ref = refs.pop(0), refs.pop(0)
    if epilogue == "ksplit":
        o_ref, kt_ref, h_scr = refs
    else:
        o_ref, h_scr = refs
    j = pl.program_id(1)

    @pl.when(j == 0)
    def _():
        x = x_ref[...]
        if prologue == "mod":
            h = x * (1.0 + p1_ref[...]) + p2_ref[...]
        else:
            h = x * lax.rsqrt(jnp.mean(x * x, axis=-1, keepdims=True) + 1e-6) * p1_ref[...]
        h_scr[...] = h.astype(BF16)

    acc = jnp.dot(h_scr[...], w_ref[...], preferred_element_type=F32)
    if epilogue == "relu2":
        r = jnp.maximum(acc, 0.0)
        o_ref[...] = (r * r).astype(o_ref.dtype)
    elif epilogue == "rope":
        for s in range(nsub):
            sl = slice(s * LANES, (s + 1) * LANES)
            md = mode_ref[j * nsub + s]
            sub = acc[:, sl]
            o_ref[:, sl] = (sub * ta_ref[md] + pltpu.roll(sub, LANES // 2, 1) * tb_ref[md]).astype(o_ref.dtype)
    elif epilogue == "ksplit":
        @pl.when(j < ksplit)
        def _():
            kt_ref[...] = acc.T.astype(kt_ref.dtype)

        @pl.when(j >= ksplit)
        def _():
            o_ref[...] = acc.astype(o_ref.dtype)
    else:
        o_ref[...] = acc.astype(o_ref.dtype)


def mm_fullk(x, x_col_block, k, w, *, prologue, p1, p2=None, epilogue="none", modes=None, ta=None, tb=None,
             out_dtype=BF16, tm=512, tn=1024, w_layer=None, k_cols=0):
    m = x.shape[0]
    n = w.shape[-1]
    tm = min(tm, m)
    assert m % tm == 0 and n % tn == 0 and w.shape[-2] == k
    nsub = tn // LANES
    rope = epilogue == "rope"
    npre = 1 if rope else 0

    def im(f):
        return (lambda i, j, *_: f(i, j))

    in_specs = [pl.BlockSpec((tm, k), im(lambda i, j: (i, x_col_block))),
                pl.BlockSpec((1, k), im(lambda i, j: (0, 0)))]
    args = [x, p1.reshape(1, k)]
    if prologue == "mod":
        in_specs.append(pl.BlockSpec((1, k), im(lambda i, j: (0, 0))))
        args.append(p2.reshape(1, k))
    if w_layer is None:
        in_specs.append(pl.BlockSpec((k, tn), im(lambda i, j: (0, j))))
    else:
        in_specs.append(pl.BlockSpec((None, k, tn), im(lambda i, j: (w_layer, 0, j))))
    args.append(w)
    if rope:
        nmode = ta.shape[0]
        in_specs += [pl.BlockSpec((nmode, tm, LANES), im(lambda i, j: (0, i, 0)))] * 2
        args += [ta, tb]
    ksplit = k_cols // tn
    if epilogue == "ksplit":
        assert k_cols % tn == 0 and 0 < k_cols < n
        out_specs = [pl.BlockSpec((tm, tn), im(lambda i, j: (i, jnp.maximum(j - ksplit, 0)))),
                     pl.BlockSpec((tn, tm), im(lambda i, j: (jnp.minimum(j, ksplit - 1), i)))]
        out_shape = [jax.ShapeDtypeStruct((m, n - k_cols), out_dtype), jax.ShapeDtypeStruct((k_cols, m), out_dtype)]
    else:
        out_specs = pl.BlockSpec((tm, tn), im(lambda i, j: (i, j)))
        out_shape = jax.ShapeDtypeStruct((m, n), out_dtype)
    grid_spec = pltpu.PrefetchScalarGridSpec(
        num_scalar_prefetch=npre,
        grid=(m // tm, n // tn),
        in_specs=in_specs,
        out_specs=out_specs,
        scratch_shapes=[pltpu.VMEM((tm, k), BF16)],
    )
    fn = pl.pallas_call(
        functools.partial(_mm_fullk_body, prologue=prologue, epilogue=epilogue, nsub=nsub, ksplit=ksplit),
        grid_spec=grid_spec,
        out_shape=out_shape,
        compiler_params=_params(("parallel", "arbitrary")),
        name="mm_fullk_" + prologue + "_" + epilogue,
    )
    if rope:
        return fn(jnp.asarray(modes, jnp.int32), *args)
    return fn(*args)


def _w_in_body(mode_ref, x_ref, sc_ref, sh_ref, w_ref, ta_ref, tb_ref, ob_ref, of_ref, h_scr, *, nsub, nb):
    j = pl.program_id(1)

    @pl.when(j == 0)
    def _():
        h_scr[...] = (x_ref[...] * (1.0 + sc_ref[...]) + sh_ref[...]).astype(BF16)

    acc = jnp.dot(h_scr[...], w_ref[...], preferred_element_type=F32)
    for s in range(nsub):
        sl = slice(s * LANES, (s + 1) * LANES)
        md = mode_ref[j * nsub + s]
        sub = acc[:, sl]
        val = sub * ta_ref[md] + pltpu.roll(sub, LANES // 2, 1) * tb_ref[md]

        @pl.when(j < nb)
        def _():
            ob_ref[:, sl] = val.astype(ob_ref.dtype)

        @pl.when(j >= nb)
        def _():
            of_ref[:, sl] = val


def w_in_projection(x, sc, sh, w, ta, tb, tm=512, tn=1024):
    m, k = x.shape
    tm = min(tm, m)
    nb, nf = _B_WIDTH // tn, _F_WIDTH // tn
    im = lambda f: (lambda i, j, *_: f(i, j))
    row = pl.BlockSpec((1, k), im(lambda i, j: (0, 0)))
    tab = pl.BlockSpec((ta.shape[0], tm, LANES), im(lambda i, j: (0, i, 0)))
    grid_spec = pltpu.PrefetchScalarGridSpec(
        num_scalar_prefetch=1,
        grid=(m // tm, nb + nf),
        in_specs=[pl.BlockSpec((tm, k), im(lambda i, j: (i, 0))), row, row,
                  pl.BlockSpec((k, tn), im(lambda i, j: (0, j))), tab, tab],
        out_specs=[pl.BlockSpec((tm, tn), im(lambda i, j: (i, jnp.minimum(j, nb - 1)))),
                   pl.BlockSpec((tm, tn), im(lambda i, j: (i, jnp.maximum(j - nb, 0))))],
        scratch_shapes=[pltpu.VMEM((tm, k), BF16)],
    )
    return pl.pallas_call(
        functools.partial(_w_in_body, nsub=tn // LANES, nb=nb),
        grid_spec=grid_spec,
        out_shape=[jax.ShapeDtypeStruct((m, _B_WIDTH), BF16), jax.ShapeDtypeStruct((m, _F_WIDTH), F32)],
        compiler_params=_params(("parallel", "arbitrary")),
        name="w_in_projection",
    )(jnp.asarray(_B_MODES + [3 if md == 2 else md for md in _F_MODES], jnp.int32), x, sc.reshape(1, k),
      sh.reshape(1, k), w, ta, tb)


def _mm_ln_body(*refs, nk, bounds):
    lhs_refs = refs[:len(bounds)]
    w_ref, x_ref, g_ref, lng_ref, lnb_ref, o_ref, acc_ref = refs[len(bounds):]
    kk = pl.program_id(1)

    def lhs():
        val = lhs_refs[-1][...]
        for ref, hi in zip(reversed(lhs_refs[:-1]), reversed(bounds[:-1])):
            val = jnp.where(kk < hi, ref[...], val)
        return val

    @pl.when(kk == 0)
    def _():
        acc_ref[...] = jnp.dot(lhs(), w_ref[...], preferred_element_type=F32)

    @pl.when(kk > 0)
    def _():
        acc_ref[...] += jnp.dot(lhs(), w_ref[...], preferred_element_type=F32)

    @pl.when(kk == nk - 1)
    def _():
        z = ALPHA * x_ref[...] + (1.0 + g_ref[...]) * acc_ref[...]
        mu = jnp.mean(z, axis=-1, keepdims=True)
        zc = z - mu
        var = jnp.mean(zc * zc, axis=-1, keepdims=True)
        o_ref[...] = zc * lax.rsqrt(var + 1e-5) * lng_ref[...] + lnb_ref[...]


def mm_ln(lhs_parts, w, w_layer, x, gate, ln_g, ln_b, tm=512, tk=512, x_buffers=2):
    m = lhs_parts[0].shape[0]
    k = sum(part.shape[1] for part in lhs_parts)
    n = w.shape[2]
    bounds, lhs_specs = [], []
    for part in lhs_parts:
        lo = bounds[-1] if bounds else 0
        hi = lo + part.shape[1] // tk
        bounds.append(hi)
        lhs_specs.append(pl.BlockSpec((tm, tk), lambda i, kk, lo=lo, hi=hi: (i, jnp.clip(kk - lo, 0, hi - lo - 1))))
    tm = min(tm, m)
    nk = k // tk
    row = lambda i, kk: (0, 0)
    return pl.pallas_call(
        functools.partial(_mm_ln_body, nk=nk, bounds=tuple(bounds)),
        grid=(m // tm, nk),
        in_specs=lhs_specs + [
            pl.BlockSpec((None, tk, n), lambda i, kk: (w_layer, kk, 0)),
            pl.BlockSpec((tm, n), lambda i, kk: (i, 0), pipeline_mode=pl.Buffered(x_buffers)),
            pl.BlockSpec((1, n), row), pl.BlockSpec((1, n), row), pl.BlockSpec((1, n), row),
        ],
        out_specs=pl.BlockSpec((tm, n), lambda i, kk: (i, 0)),
        out_shape=jax.ShapeDtypeStruct((m, n), F32),
        scratch_shapes=[pltpu.VMEM((tm, n), F32)],
        compiler_params=_params(("parallel", "arbitrary"), V7X_VMEM_LIMIT_LARGE),
        name="mm_ln",
    )(*lhs_parts, w, x, gate.reshape(1, n), ln_g.reshape(1, n), ln_b.reshape(1, n))


def _dilated_body(q_ref, kc_ref, kp_ref, vc_ref, vp_ref, o_ref, q32, kc32, kp32, vc32, vp32, od_scr, ld_scr):
    n = pl.program_id(1)
    q32[...] = q_ref[...].astype(F32)
    kc32[...] = kc_ref[...].astype(F32)
    kp32[...] = kp_ref[...].astype(F32)
    vc32[...] = vc_ref[...].astype(F32)
    vp32[...] = vp_ref[...].astype(F32)
    qi = lax.broadcasted_iota(jnp.int32, (QBLOCK, 2 * QBLOCK), 0)
    ki = lax.broadcasted_iota(jnp.int32, (QBLOCK, 2 * QBLOCK), 1)
    delta = QBLOCK + qi - ki
    in_band = (delta >= 0) & (delta <= QBLOCK)
    in_band_first = in_band & ((ki >= QBLOCK) | (n > 0))

    for pi, (window, dil) in enumerate(DSW_PATTERNS):
        assert window // dil == QBLOCK
        nblk = DIL_SUPER // (QBLOCK * dil)
        for mb in range(nblk):
            for r in range(dil):
                def rows(b):
                    return pl.ds(b * QBLOCK * dil + r, QBLOCK, stride=dil) if dil > 1 else pl.ds(b * QBLOCK, QBLOCK)

                cur = rows(mb)
                if mb > 0:
                    k_prev, v_prev, valid = kc32[rows(mb - 1), :], vc32[rows(mb - 1), :], in_band
                else:
                    k_prev, v_prev, valid = kp32[rows(nblk - 1), :], vp32[rows(nblk - 1), :], in_band_first
                k = jnp.concatenate([k_prev, kc32[cur, :]], axis=0).astype(BF16)
                v = jnp.concatenate([v_prev, vc32[cur, :]], axis=0).astype(BF16)
                s = lax.dot_general(q32[cur, :].astype(BF16), k, NT_DIMS, preferred_element_type=F32)
                s = jnp.where(valid, s, NEG_INF)
                m = jnp.max(s, axis=-1, keepdims=True)
                p = jnp.exp2(s - m)
                l = jnp.sum(p, axis=-1, keepdims=True)
                od_scr[pi, cur, :] = jnp.dot(p.astype(BF16), v, preferred_element_type=F32) / l
                ld_scr[pi, cur, :] = jnp.broadcast_to(m + jnp.log2(l), (QBLOCK, HEAD_DIM))

    npat = len(DSW_PATTERNS)
    for c0 in range(0, DIL_SUPER, DIL_MIX_ROWS):
        rs = slice(c0, c0 + DIL_MIX_ROWS)
        ls = [ld_scr[i, rs, :] for i in range(npat)]
        mx = functools.reduce(jnp.maximum, ls)
        ws = [jnp.exp2(t - mx) for t in ls]
        num = sum(ws[i] * od_scr[i, rs, :] for i in range(npat))
        o_ref[rs, :] = (num / sum(ws)).astype(o_ref.dtype)


def dilated_attention(qkv, s_len, q_blk, k_blk, v_blk):
    assert s_len % DIL_SUPER == 0
    cur = lambda blk: (lambda h, n: (n, blk + h))
    prev = lambda blk: (lambda h, n: (jnp.maximum(n - 1, 0), blk + h))
    bs = lambda f: pl.BlockSpec((DIL_SUPER, HEAD_DIM), f)
    f32_rows = pltpu.VMEM((DIL_SUPER, HEAD_DIM), F32)
    per_pattern = pltpu.VMEM((len(DSW_PATTERNS), DIL_SUPER, HEAD_DIM), F32)
    return pl.pallas_call(
        _dilated_body,
        grid=(DSW_HEADS, s_len // DIL_SUPER),
        in_specs=[bs(cur(q_blk)), bs(cur(k_blk)), bs(prev(k_blk)), bs(cur(v_blk)), bs(prev(v_blk))],
        out_specs=pl.BlockSpec((DIL_SUPER, HEAD_DIM), lambda h, n: (n, h)),
        out_shape=jax.ShapeDtypeStruct((s_len, DSW_W), BF16),
        scratch_shapes=[f32_rows] * 5 + [per_pattern] * 2,
        compiler_params=_params(("parallel", "parallel")),
        name="dilated_attention",
    )(qkv, qkv, qkv, qkv, qkv)


def _cmp_body(x_ref, pe_ref, w1_ref, w2_ref, o_ref, *, ncp):
    x = x_ref[...]
    half = NSA_CMP_STRIDE * HEAD_DIM
    a = jnp.dot((x + pe_ref[0:1, :]).astype(BF16), w1_ref[0:half, :], preferred_element_type=F32)
    b = jnp.dot((x + pe_ref[1:2, :]).astype(BF16), w1_ref[half:2 * half, :], preferred_element_type=F32)
    hid = a + pltpu.roll(b, ncp - 1, 0)
    act = jax.nn.gelu(hid)
    o_ref[...] = jnp.dot(act.astype(BF16), w2_ref[...], preferred_element_type=F32).astype(o_ref.dtype)


def nsa_compress(xs, pe, w1, w2):
    ncp = xs.shape[2]
    half = NSA_CMP_STRIDE * HEAD_DIM
    return pl.pallas_call(
        functools.partial(_cmp_body, ncp=ncp),
        grid=(2, NSA_KV_HEADS),
        in_specs=[
            pl.BlockSpec((None, None, ncp, half), lambda a, h: (a, h, 0, 0)),
            pl.BlockSpec((None, 2, half), lambda a, h: (a, 0, 0)),
            pl.BlockSpec((None, 2 * half, NSA_CMP_HIDDEN), lambda a, h: (a, 0, 0)),
            pl.BlockSpec((None, NSA_CMP_HIDDEN, HEAD_DIM), lambda a, h: (a, 0, 0)),
        ],
        out_specs=pl.BlockSpec((None, None, ncp, HEAD_DIM), lambda a, h: (a, h, 0, 0)),
        out_shape=jax.ShapeDtypeStruct((2, NSA_KV_HEADS, ncp, HEAD_DIM), BF16),
        compiler_params=_params(("parallel", "parallel")),
        name="nsa_compress",
    )(xs, pe.reshape(2, 2, half), w1, w2)


def _stack_heads(q):
    return jnp.concatenate([q[:, g * HEAD_DIM:(g + 1) * HEAD_DIM] for g in range(NSA_GROUP)], axis=0)


def _nsa_body(q_ref, kst_ref, vs_ref, kc_ref, vc_ref, covt_ref, e0_ref, gate_ref, u_ref,
              sel_scr, q4_scr, sc_scr, pf_scr, pc_scr, mk_scr, ve_scr, s_scr, p_scr, a_scr, m_scr, acc_scr,
              *, ncp, nselp, ntop, tk):
    n = pl.program_id(1)
    rows = NSA_GROUP * QBLOCK
    q4_scr[...] = _stack_heads(q_ref[...])
    q4 = q4_scr[...]

    sc_scr[...] = lax.dot_general(q4, kc_ref[...], NT_DIMS, preferred_element_type=F32)
    chunk = min(FLASH_CHUNK // ncp, rows)
    cend = lax.broadcasted_iota(jnp.int32, (chunk, ncp), 1) * NSA_CMP_STRIDE + (NSA_CMP_LEN - 1)
    for r in range(0, rows, chunk):
        rs = slice(r, r + chunk)
        qpos_r = n * QBLOCK + ((r + lax.broadcasted_iota(jnp.int32, (chunk, ncp), 0)) & (QBLOCK - 1))
        s = jnp.where(cend <= qpos_r, sc_scr[rs, :], NEG_INF)
        m = jnp.max(s, axis=-1, keepdims=True)
        e = jnp.exp2(s - jnp.where(m == NEG_INF, 0.0, m))
        l = jnp.sum(e, axis=-1, keepdims=True)
        p = e / jnp.where(l > 0, l, 1.0)
        pf_scr[rs, :] = p
        pc_scr[rs, :] = p.astype(BF16)
    o_c = jnp.dot(pc_scr[...], vc_ref[...], preferred_element_type=F32)

    psum = pf_scr[0:QBLOCK, :]
    for g in range(1, NSA_GROUP):
        psum = psum + pf_scr[g * QBLOCK:(g + 1) * QBLOCK, :]
    p_hi = psum.astype(BF16)
    p_lo = (psum - p_hi.astype(F32)).astype(BF16)
    covt = covt_ref[...]
    imp_t = (lax.dot_general(covt, p_hi, NT_DIMS, preferred_element_type=F32)
             + lax.dot_general(covt, p_lo, NT_DIMS, preferred_element_type=F32))

    jblk = lax.broadcasted_iota(jnp.int32, (nselp, QBLOCK), 0)
    qpos_c = n * QBLOCK + lax.broadcasted_iota(jnp.int32, (nselp, QBLOCK), 1)
    cur = qpos_c // NSA_SEL_BLOCK
    forced = (jblk == 0) | (jblk == cur) | (jblk == cur - 1)
    valid = jblk * NSA_SEL_BLOCK <= qpos_c
    score0 = jnp.where(valid, jnp.where(forced, jnp.inf, imp_t), NEG_INF)

    def pick(_, carry):
        score, picked = carry
        mx = jnp.max(score, axis=0, keepdims=True)
        idx = jnp.min(jnp.where(score == mx, jblk, nselp), axis=0, keepdims=True)
        hit = jblk == idx
        return jnp.where(hit, NEG_INF, score), jnp.where(hit, 1.0, picked)

    _, picked = lax.fori_loop(0, ntop, pick, (score0, jnp.zeros((nselp, QBLOCK), F32)))
    sel_scr[...] = jnp.where(valid, picked, 0.0).T

    blocks_per_tile = tk // NSA_SEL_BLOCK
    _flash_init(m_scr, acc_scr, ve_scr)

    def k_tile(t, slot, buf):
        shift = (nselp - t * blocks_per_tile) % nselp
        sel_t = pltpu.roll(sel_scr[...], shift, 1)[:, 0:LANES].astype(BF16)
        mk_scr[slot] = jnp.dot(sel_t, e0_ref[...], preferred_element_type=F32)
        return kst_ref[:, pl.ds(pl.multiple_of(t * tk, tk), tk)]

    def v_tile(t, buf):
        ve_scr[buf, :, 0:HEAD_DIM] = vs_ref[pl.ds(pl.multiple_of(t * tk, tk), tk), :]
        return ve_scr[buf]

    def mask_for(slot, t, masked):
        def mask_fn(r, s):
            rq = r % QBLOCK
            ok = mk_scr[slot, rq:rq + s.shape[0], :] > 0.5
            if masked:
                qpos = n * QBLOCK + rq + lax.broadcasted_iota(jnp.int32, s.shape, 0)
                ok = ok & (t * tk + lax.broadcasted_iota(jnp.int32, s.shape, 1) <= qpos)
            return jnp.where(ok, s, NEG_INF)
        return mask_fn

    _flash_pipeline((n * QBLOCK + QBLOCK - 1) // tk, q4_scr, k_tile, v_tile, mask_for,
                    s_scr, p_scr, a_scr, m_scr, acc_scr)
    o_s = _flash_output(acc_scr)

    gates = jax.nn.sigmoid(gate_ref[...])
    for g in range(NSA_GROUP):
        rs = slice(g * QBLOCK, (g + 1) * QBLOCK)
        u_ref[:, g * HEAD_DIM:(g + 1) * HEAD_DIM] = (gates[:, 3 * g:3 * g + 1] * o_c[rs]
                                                     + gates[:, 3 * g + 1:3 * g + 2] * o_s[rs])


def nsa_cmp_sel(qkv, ks_t, s_len, q_blk, vs_blk, kv_cmp, gates, tk=512):
    nb = s_len // QBLOCK
    ncp = s_len // NSA_CMP_STRIDE
    nsel = s_len // NSA_SEL_BLOCK
    nselp = -(-nsel // LANES) * LANES
    ntop = min(NSA_TOP_N, nsel)
    ci = np.arange(ncp)[None, :] * NSA_CMP_STRIDE
    sj = np.arange(nselp)[:, None] * NSA_SEL_BLOCK
    cov = (ci < sj + NSA_SEL_BLOCK) & (ci + NSA_CMP_LEN > sj) & (np.arange(ncp)[None, :] < ncp - 1) & (sj < s_len)
    covt = jnp.asarray(cov.astype(np.float32), BF16)
    e0 = jnp.asarray((np.arange(tk)[None, :] // NSA_SEL_BLOCK == np.arange(LANES)[:, None]).astype(np.float32), BF16)
    gw = NSA_GROUP * HEAD_DIM
    rows = NSA_GROUP * QBLOCK
    return pl.pallas_call(
        functools.partial(_nsa_body, ncp=ncp, nselp=nselp, ntop=ntop, tk=tk),
        grid=(NSA_KV_HEADS, nb),
        in_specs=[
            pl.BlockSpec((QBLOCK, gw), lambda h, n: (n, q_blk + h)),
            pl.BlockSpec((HEAD_DIM, s_len), lambda h, n: (h, 0)),
            pl.BlockSpec((s_len, HEAD_DIM), lambda h, n: (0, vs_blk + h)),
            pl.BlockSpec((None, None, ncp, HEAD_DIM), lambda h, n: (0, h, 0, 0)),
            pl.BlockSpec((None, None, ncp, HEAD_DIM), lambda h, n: (1, h, 0, 0)),
            pl.BlockSpec((nselp, ncp), lambda h, n: (0, 0)),
            pl.BlockSpec((LANES, tk), lambda h, n: (0, 0)),
            pl.BlockSpec((None, QBLOCK, 3 * NSA_GROUP), lambda h, n: (h, n, 0)),
        ],
        out_specs=pl.BlockSpec((QBLOCK, gw), lambda h, n: (n, h)),
        out_shape=jax.ShapeDtypeStruct((s_len, NSA_QW), F32),
        scratch_shapes=[pltpu.VMEM((QBLOCK, nselp), F32), pltpu.VMEM((rows, HEAD_DIM), BF16),
                        pltpu.VMEM((rows, ncp), F32), pltpu.VMEM((rows, ncp), F32), pltpu.VMEM((rows, ncp), BF16),
                        pltpu.VMEM((FLASH_UNROLL, QBLOCK, tk), F32), pltpu.VMEM((FLASH_UNROLL + 1, tk, 2 * HEAD_DIM), BF16),
                        pltpu.VMEM((FLASH_UNROLL, rows, tk), F32), pltpu.VMEM((FLASH_UNROLL, rows, tk), BF16),
                        pltpu.VMEM((FLASH_UNROLL, rows, LANES), F32), pltpu.VMEM((rows, LANES), F32),
                        pltpu.VMEM((rows, 2 * HEAD_DIM), F32)],
        compiler_params=_params(("parallel", "arbitrary")),
        name="nsa_cmp_sel",
    )(qkv, ks_t, qkv, kv_cmp, kv_cmp, covt, e0, gates)


def _win_body(q_ref, kw_ref, vw_ref, gate_ref, u_ref, o_ref, *, span):
    n = pl.program_id(1)
    rows = NSA_GROUP * QBLOCK
    q4 = _stack_heads(q_ref[...])
    start = pl.multiple_of(jnp.maximum(n * QBLOCK - (span - QBLOCK), 0), QBLOCK)
    s = lax.dot_general(q4, kw_ref[pl.ds(start, span), :], NT_DIMS, preferred_element_type=F32)
    qpos = n * QBLOCK + (lax.broadcasted_iota(jnp.int32, (rows, span), 0) & (QBLOCK - 1))
    delta = qpos - (start + lax.broadcasted_iota(jnp.int32, (rows, span), 1))
    s = jnp.where((delta >= 0) & (delta <= NSA_WINDOW - 1), s, NEG_INF)
    m = jnp.max(s, axis=-1, keepdims=True)
    p = jnp.exp2(s - m)
    l = jnp.sum(p, axis=-1, keepdims=True)
    o_w = jnp.dot(p.astype(BF16), vw_ref[pl.ds(start, span), :], preferred_element_type=F32) / l
    gates = jax.nn.sigmoid(gate_ref[...])
    for g in range(NSA_GROUP):
        sl = slice(g * HEAD_DIM, (g + 1) * HEAD_DIM)
        o_ref[:, sl] = (u_ref[:, sl] + gates[:, 3 * g + 2:3 * g + 3] * o_w[g * QBLOCK:(g + 1) * QBLOCK]
                        ).astype(o_ref.dtype)


def nsa_window(qkv, s_len, q_blk, kw_blk, vw_blk, gates, u):
    nb = s_len // QBLOCK
    span = (-(-(NSA_WINDOW - 1) // QBLOCK) + 1) * QBLOCK
    gw = NSA_GROUP * HEAD_DIM
    return pl.pallas_call(
        functools.partial(_win_body, span=span),
        grid=(NSA_KV_HEADS, nb),
        in_specs=[
            pl.BlockSpec((QBLOCK, gw), lambda h, n: (n, q_blk + h)),
            pl.BlockSpec((s_len, HEAD_DIM), lambda h, n: (0, kw_blk + h)),
            pl.BlockSpec((s_len, HEAD_DIM), lambda h, n: (0, vw_blk + h)),
            pl.BlockSpec((None, QBLOCK, 3 * NSA_GROUP), lambda h, n: (h, n, 0)),
            pl.BlockSpec((QBLOCK, gw), lambda h, n: (n, h)),
        ],
        out_specs=pl.BlockSpec((QBLOCK, gw), lambda h, n: (n, h)),
        out_shape=jax.ShapeDtypeStruct((s_len, NSA_QW), BF16),
        compiler_params=_params(("parallel", "parallel")),
        name="nsa_window",
    )(qkv, qkv, qkv, gates, u)


def _softmax_rows(s_ref, p_ref, a_ref, m_scr, mask_fn):
    rows, tk = s_ref.shape
    nrep = tk // LANES
    chunk = FLASH_CHUNK // tk
    for r in range(0, rows, chunk):
        rs = slice(r, r + chunk)
        s = s_ref[rs, :]
        if mask_fn is not None:
            s = mask_fn(r, s)
        m_old = m_scr[rs, :]
        m_new = jnp.maximum(m_old, jnp.max(s, axis=1, keepdims=True))
        p_ref[rs, :] = jnp.exp2(s - jnp.concatenate([m_new] * nrep, axis=1)).astype(BF16)
        a_ref[rs, :] = jnp.exp2(m_old - m_new)
        m_scr[rs, :] = m_new


def _flash_pipeline(n_full, q_ref, k_tile, v_tile, mask_for, s_scr, p_scr, a_scr, m_scr, acc_scr):
    def scores(t, idx, buf):
        s_scr[idx] = jnp.dot(q_ref[...], k_tile(t, idx, buf), preferred_element_type=F32)

    def softmax(idx, t, masked):
        _softmax_rows(s_scr.at[idx], p_scr.at[idx], a_scr.at[idx], m_scr, mask_for(idx, t, masked))

    def values(idx, t, buf):
        a = a_scr[idx]
        acc_scr[...] = (jnp.concatenate([a, a], axis=1) * acc_scr[...]
                        + jnp.dot(p_scr[idx], v_tile(t, buf), preferred_element_type=F32))

    last = FLASH_UNROLL - 1
    p_scr[last] = jnp.zeros(p_scr.shape[1:], p_scr.dtype)
    a_scr[last] = jnp.ones(a_scr.shape[1:], a_scr.dtype)
    scores(0, 0, 0)

    def run(t0, count, masked_last, lookahead):
        for j in range(count):
            t = t0 + j
            if lookahead or j + 1 < count:
                scores(t + 1, (j + 1) % FLASH_UNROLL, j)
            softmax(j, t, masked_last and j == count - 1)
            values((j - 1) % FLASH_UNROLL, jnp.maximum(t - 1, 0), j)
        if not lookahead:
            values(count - 1, t0 + count - 1, count)

    def body(u, carry):
        run(FLASH_UNROLL * u, FLASH_UNROLL, False, True)
        return carry

    lax.fori_loop(0, n_full // FLASH_UNROLL, body, 0)
    rem = n_full % FLASH_UNROLL
    for r in range(FLASH_UNROLL):
        @pl.when(rem == r)
        def _(r=r):
            run(n_full - r, r + 1, True, False)


def _flash_init(m_scr, acc_scr, ve_scr):
    dv = ve_scr.shape[2] // 2
    m_scr[...] = jnp.full(m_scr.shape, NEG_INF, F32)
    acc_scr[...] = jnp.zeros(acc_scr.shape, F32)
    ve_scr[:, :, dv:] = jnp.ones((ve_scr.shape[0], ve_scr.shape[1], dv), ve_scr.dtype)


def _flash_output(acc_scr):
    dv = acc_scr.shape[1] // 2
    return acc_scr[:, 0:dv] / acc_scr[:, dv:]


def _mla_body(q_ref, knt_ref, krt_ref, v_ref, o_ref, kt_scr, ve_scr, s_scr, p_scr, a_scr, m_scr, acc_scr,
              *, tq, tk):
    qi = pl.program_id(1)
    _flash_init(m_scr, acc_scr, ve_scr)

    def mask_for(slot, t, masked):
        if not masked:
            return None

        def causal(r, s):
            qpos = qi * tq + r + lax.broadcasted_iota(jnp.int32, s.shape, 0)
            kpos = t * tk + lax.broadcasted_iota(jnp.int32, s.shape, 1)
            return jnp.where(kpos <= qpos, s, NEG_INF)
        return causal

    def k_tile(t, slot, buf):
        k0 = pl.multiple_of(t * tk, tk)
        kt_scr[buf, 0:MLA_NOPE_DIM, :] = knt_ref[:, pl.ds(k0, tk)]
        kt_scr[buf, MLA_NOPE_DIM:, :] = krt_ref[:, pl.ds(k0, tk)]
        return kt_scr[buf]

    def v_tile(t, buf):
        ve_scr[buf, :, 0:MLA_V_DIM] = v_ref[pl.ds(pl.multiple_of(t * tk, tk), tk), :]
        return ve_scr[buf]

    _flash_pipeline((qi * tq) // tk, q_ref, k_tile, v_tile, mask_for, s_scr, p_scr, a_scr, m_scr, acc_scr)
    o_ref[...] = _flash_output(acc_scr).astype(o_ref.dtype)


def mla_attention(q, v, k_nope_t, k_rope_t, s_len, tq=512, tk=512):
    tq, tk = min(tq, s_len), min(tk, s_len)
    qw = 2 * LANES
    return pl.pallas_call(
        functools.partial(_mla_body, tq=tq, tk=tk),
        grid=(MLA_HEADS, s_len // tq),
        in_specs=[
            pl.BlockSpec((tq, qw), lambda h, i: (i, h)),
            pl.BlockSpec((MLA_NOPE_DIM, s_len), lambda h, i: (h, 0)),
            pl.BlockSpec((LANES, s_len), lambda h, i: (0, 0)),
            pl.BlockSpec((s_len, MLA_V_DIM), lambda h, i: (0, h)),
        ],
        out_specs=pl.BlockSpec((tq, MLA_V_DIM), lambda h, i: (i, h)),
        out_shape=jax.ShapeDtypeStruct((s_len, MLA_HEADS * MLA_V_DIM), BF16),
        scratch_shapes=[pltpu.VMEM((FLASH_UNROLL, qw, tk), BF16),
                        pltpu.VMEM((FLASH_UNROLL + 1, tk, 2 * MLA_V_DIM), BF16),
                        pltpu.VMEM((FLASH_UNROLL, tq, tk), F32), pltpu.VMEM((FLASH_UNROLL, tq, tk), BF16),
                        pltpu.VMEM((FLASH_UNROLL, tq, LANES), F32), pltpu.VMEM((tq, LANES), F32),
                        pltpu.VMEM((tq, 2 * MLA_V_DIM), F32)],
        compiler_params=_params(("parallel", "parallel")),
        name="mla_attention",
    )(q, k_nope_t, k_rope_t, v)


_IN_SIZES = (DSW_W, DSW_W, DSW_W, NSA_QW, NSA_KVW, NSA_KVW, NSA_KVW, NSA_KVW, NSA_KVW, NSA_KVW,
             3 * NSA_HEADS, MLA_Q_LORA, MLA_KV_LORA, MLA_ROPE_DIM)
_IN_NAMES = ("a_q", "a_k", "a_v", "n_q", "n_kc", "n_vc", "n_ks", "n_vs", "n_kw", "n_vw", "n_gate",
             "m_cq", "m_ckv", "m_kr")
_IN_OFF = dict(zip(_IN_NAMES, np.concatenate([[0], np.cumsum(_IN_SIZES)[:-1]]).tolist()))
_IN_LEN = dict(zip(_IN_NAMES, _IN_SIZES))

_B_ORDER = ("a_q", "a_k", "n_q", "n_ks", "n_kw", "n_vs", "n_vw", "a_v")
_B_ROPE = 2 * DSW_W + NSA_QW + 2 * NSA_KVW
_B_WIDTH = sum(_IN_LEN[k] for k in _B_ORDER)
_B_COL = dict(zip(_B_ORDER, np.concatenate([[0], np.cumsum([_IN_LEN[k] for k in _B_ORDER])[:-1]]).tolist()))
_F_COL = {"m_cq": 0, "m_ckv": MLA_Q_LORA, "n_kc": 2048, "n_vc": 2304, "m_kr": 2560, "n_gate": 2688}
_F_WIDTH = 3072
_HALF_ROPE = MLA_ROPE_DIM // 2


def _prep_w_in_body(w_ref, wb_ref):
    for name in _B_ORDER:
        wb_ref[:, _B_COL[name]:_B_COL[name] + _IN_LEN[name]] = (
            w_ref[:, _IN_OFF[name]:_IN_OFF[name] + _IN_LEN[name]].astype(BF16))
    wb_ref[:, _B_WIDTH:] = jnp.zeros((wb_ref.shape[0], _F_WIDTH), BF16)
    for name in ("m_cq", "m_ckv", "n_kc", "n_vc", "n_gate"):
        wb_ref[:, _B_WIDTH + _F_COL[name]:_B_WIDTH + _F_COL[name] + _IN_LEN[name]] = (
            w_ref[:, _IN_OFF[name]:_IN_OFF[name] + _IN_LEN[name]].astype(BF16))
    kr_src, kr_dst = _IN_OFF["m_kr"], _B_WIDTH + _F_COL["m_kr"]
    for half in range(2):
        wb_ref[:, kr_dst + 2 * half * _HALF_ROPE:kr_dst + (2 * half + 1) * _HALF_ROPE] = (
            w_ref[:, kr_src + half * _HALF_ROPE:kr_src + (half + 1) * _HALF_ROPE].astype(BF16))


def _prep_w_in(w, layer, tm=256):
    nl, k, n = w.shape
    return pl.pallas_call(
        _prep_w_in_body,
        grid=(k // tm,),
        in_specs=[pl.BlockSpec((None, tm, n), lambda i: (layer, i, 0))],
        out_specs=pl.BlockSpec((tm, _B_WIDTH + _F_WIDTH), lambda i: (i, 0)),
        out_shape=jax.ShapeDtypeStruct((k, _B_WIDTH + _F_WIDTH), BF16),
        compiler_params=_params(("parallel",)),
        name="prep_w_in",
    )(w)


_B_MODES = ([2] * (DSW_W // LANES) + [1] * (DSW_W // LANES) + [2] * (NSA_QW // LANES)
            + [1] * (2 * NSA_KVW // LANES) + [0] * ((_B_WIDTH - _B_ROPE) // LANES))
_F_MODES = [0] * 16 + [1, 1, 0, 0, 2] + [0] * 3
_Q_MODES = [0, 1] * MLA_HEADS
C_HEAD = HEAD_DIM ** -0.5 * LOG2E
C_MLA = (MLA_NOPE_DIM + MLA_ROPE_DIM) ** -0.5 * LOG2E


def _prep_w_uq(w):
    w = w.reshape(MLA_Q_LORA, MLA_HEADS, MLA_NOPE_DIM + MLA_ROPE_DIM)
    z = jnp.zeros((MLA_Q_LORA, MLA_HEADS, _HALF_ROPE), w.dtype)
    w = jnp.concatenate([w[..., :MLA_NOPE_DIM], w[..., MLA_NOPE_DIM:MLA_NOPE_DIM + _HALF_ROPE], z,
                         w[..., MLA_NOPE_DIM + _HALF_ROPE:], z], axis=-1)
    return w.reshape(MLA_Q_LORA, MLA_HEADS * 2 * LANES).astype(BF16)


def _prep_w_ukv(w):
    w = w.reshape(MLA_KV_LORA, MLA_HEADS, 2, MLA_NOPE_DIM).transpose(0, 2, 1, 3)
    return w.reshape(MLA_KV_LORA, 2 * MLA_HEADS * MLA_NOPE_DIM).astype(BF16)


def _rope_tables(positions):
    pos = positions.astype(F32)[:, None]

    def cs(dim):
        inv = ROPE_THETA ** (-jnp.arange(0, dim, 2, dtype=F32) / dim)
        ang = pos * inv
        return jnp.cos(ang), jnp.sin(ang)

    c128, s128 = cs(HEAD_DIM)
    c64, s64 = cs(MLA_ROPE_DIM)
    z = jnp.zeros_like(c64)
    one = jnp.ones((pos.shape[0], LANES), F32)
    zero = jnp.zeros((pos.shape[0], LANES), F32)
    a128, b128 = jnp.concatenate([c128, c128], 1), jnp.concatenate([-s128, s128], 1)
    a64, b64 = jnp.concatenate([c64, z, c64, z], 1), jnp.concatenate([-s64, z, s64, z], 1)
    return {"in": (jnp.stack([one, a128, a128 * C_HEAD, a64]), jnp.stack([zero, b128, b128 * C_HEAD, b64])),
            "q": (jnp.stack([one * C_MLA, a64 * C_MLA]), jnp.stack([zero, b64 * C_MLA]))}


def _mixer(x, sc1, sh1, tabs, layer, w_in, cmp_pe, cmp_w1, cmp_w2, q_norm, kv_norm, w_uq, w_ukv):
    s_len = x.shape[0]
    pb, pf = w_in_projection(x, sc1, sh1, _prep_w_in(w_in, layer), *tabs["in"])

    blk = lambda name: _B_COL[name] // HEAD_DIM
    out_a = dilated_attention(pb, s_len, blk("a_q"), blk("a_k"), blk("a_v"))

    ncp = s_len // NSA_CMP_STRIDE

    def blocks16(name):
        t = pf[:, _F_COL[name]:_F_COL[name] + NSA_KVW]
        return t.reshape(ncp, NSA_CMP_STRIDE, NSA_KV_HEADS, HEAD_DIM).transpose(2, 0, 1, 3).reshape(
            NSA_KV_HEADS, ncp, NSA_CMP_STRIDE * HEAD_DIM)

    kv_cmp = nsa_compress(jnp.stack([blocks16("n_kc"), blocks16("n_vc")]), cmp_pe,
                          cmp_w1.astype(BF16), cmp_w2.astype(BF16))
    gates = pf[:, _F_COL["n_gate"]:_F_COL["n_gate"] + 3 * NSA_HEADS]
    gates = gates.reshape(s_len, NSA_KV_HEADS, 3 * NSA_GROUP).transpose(1, 0, 2)
    gw = NSA_GROUP * HEAD_DIM
    ks_t = pb[:, _B_COL["n_ks"]:_B_COL["n_ks"] + NSA_KVW].T
    u = nsa_cmp_sel(pb, ks_t, s_len, _B_COL["n_q"] // gw, _B_COL["n_vs"] // HEAD_DIM, kv_cmp, gates)
    out_b = nsa_window(pb, s_len, _B_COL["n_q"] // gw, _B_COL["n_kw"] // HEAD_DIM, _B_COL["n_vw"] // HEAD_DIM,
                       gates, u)

    q = mm_fullk(pf, 0, MLA_Q_LORA, _prep_w_uq(w_uq), prologue="rms", p1=q_norm, epilogue="rope",
                 modes=_Q_MODES, ta=tabs["q"][0], tb=tabs["q"][1], out_dtype=BF16)
    v, k_nope_t = mm_fullk(pf, _F_COL["m_ckv"] // MLA_KV_LORA, MLA_KV_LORA, _prep_w_ukv(w_ukv), prologue="rms",
                           p1=kv_norm, epilogue="ksplit", k_cols=MLA_HEADS * MLA_NOPE_DIM, out_dtype=BF16)
    k_rope_t = pf[:, _F_COL["m_kr"]:_F_COL["m_kr"] + LANES].astype(BF16).T
    out_c = mla_attention(q, v, k_nope_t, k_rope_t, s_len)
    return [out_a, out_b, out_c]


def kernel(x, c, positions, w_ada, b_ada, w_in, nsa_cmp_pe, nsa_cmp_w1, nsa_cmp_w2, mla_q_norm, mla_kv_norm,
           mla_w_uq, mla_w_ukv, w_out, ln1_g, ln1_b, mlp_w1, mlp_w2, ln2_g, ln2_b):
    assert x.shape[0] == 1, "kernel handles batch size 1"
    xs = x[0]
    d = xs.shape[1]
    tabs = _rope_tables(positions[0])
    w_out16, mlp_w1_16, mlp_w2_16 = w_out.astype(BF16), mlp_w1.astype(BF16), mlp_w2.astype(BF16)
    for l in range(DEPTH):
        mod = adaln(c, w_ada, b_ada, l)
        sh1, sc1, g1, sh2, sc2, g2 = [mod[:, i * d:(i + 1) * d] for i in range(6)]
        mixed = _mixer(xs, sc1, sh1, tabs, l, w_in, nsa_cmp_pe[l], nsa_cmp_w1[l], nsa_cmp_w2[l],
                       mla_q_norm[l], mla_kv_norm[l], mla_w_uq[l], mla_w_ukv[l])
        xs = mm_ln(mixed, w_out16, l, xs, g1, ln1_g[l], ln1_b[l])
        act = mm_fullk(xs, 0, d, mlp_w1_16, prologue="mod", p1=sc2, p2=sh2, epilogue="relu2", out_dtype=BF16,
                       w_layer=l)
        xs = mm_ln([act], mlp_w2_16, l, xs, g2, ln2_g[l], ln2_b[l], tk=1024, x_buffers=1)
    return xs[None]
```
